```python
import math
import jax, jax.numpy as jnp
from jax import lax
import numpy as np

D_MODEL = 2048
BATCH = 16
SEQ = 2048
DEPTH = 2

GRID_W = 64
CTX_LEN = 256
MIX_W = D_MODEL
LIN_W = MIX_W // 2
POOL_W = MIX_W - LIN_W
N_LIN_HEADS = 8
HEAD_DIM = LIN_W // N_LIN_HEADS
N_DIR = 2
N_POOL_GROUPS = 4
POOL_GROUP_W = POOL_W // N_POOL_GROUPS
POOL_SIZES = (2, 4, 8, 16)
CONV_K = 5
CHUNK = 64
D_FF = 4 * D_MODEL
N_MOD = 6
GATE_START = 3 * LIN_W
DECAY_START = 4 * LIN_W
POOL_START = DECAY_START + 2 * N_DIR * N_LIN_HEADS
IN_COLS = POOL_START + POOL_W
EPS = 1e-6

kernel_name = 'hybrid_deltanet_pool_dit_block'


def rmsnorm(x, w):
    xf = x.astype(jnp.float32)
    y = xf * lax.rsqrt(jnp.mean(xf * xf, axis=-1, keepdims=True) + EPS)
    return y.astype(x.dtype) * w


def l2norm(x):
    xf = x.astype(jnp.float32)
    return (xf * lax.rsqrt(jnp.sum(xf * xf, axis=-1, keepdims=True) + EPS)).astype(x.dtype)


def modulate(xn, shift, scale):
    return xn * (1 + scale) + shift


def short_conv(u, w):
    t = u.shape[1]
    pad = CONV_K // 2
    up = jnp.pad(u, ((0, 0), (pad, pad), (0, 0)))
    out = up[:, 0:t] * w[0]
    for i in range(1, CONV_K):
        out = out + up[:, i:i + t] * w[i]
    return out


def gated_delta_chunked(q, k, v, g, beta, s0):
    b, t, h, dk = k.shape
    dv = v.shape[-1]
    n = t // CHUNK
    f32 = jnp.float32

    def to_chunks(a):
        a = a.astype(f32).reshape((b, n, CHUNK) + a.shape[2:])
        return jnp.moveaxis(a, (1, 2), (0, 3))

    kc, vc, gch, bc = to_chunks(k), to_chunks(v), to_chunks(g), to_chunks(beta)
    gcum = jnp.cumsum(gch, axis=-1)
    incl = jnp.tril(jnp.ones((CHUNK, CHUNK), dtype=bool))
    strict = jnp.tril(jnp.ones((CHUNK, CHUNK), dtype=bool), -1)
    diff = gcum[..., :, None] - gcum[..., None, :]
    decay = jnp.where(incl, jnp.exp(jnp.where(incl, diff, 0.0)), 0.0)
    kb = kc * bc[..., None]
    m = jnp.where(strict, jnp.einsum('nbhid,nbhjd->nbhij', kb, kc) * decay, 0.0)
    tmat = m + jnp.eye(CHUNK, dtype=f32)
    u = lax.linalg.triangular_solve(tmat, vc * bc[..., None], left_side=True, lower=True, unit_diagonal=True)
    w = lax.linalg.triangular_solve(tmat, kb * jnp.exp(gcum)[..., None], left_side=True, lower=True, unit_diagonal=True)
    g_last = gcum[..., -1]
    k_tail = kc * jnp.exp(g_last[..., None] - gcum)[..., None]
    if q is None:
        xs = (u, w, k_tail, g_last, None, None)
    else:
        qc = to_chunks(q)
        intra = jnp.einsum('nbhid,nbhjd->nbhij', qc, kc) * decay
        xs = (u, w, k_tail, g_last, qc * jnp.exp(gcum)[..., None], intra)

    def step(s, inp):
        u_i, w_i, kt_i, gl_i, qd_i, at_i = inp
        v_new = u_i - jnp.einsum('bhck,bhkv->bhcv', w_i, s)
        s_new = s * jnp.exp(gl_i)[..., None, None] + jnp.einsum('bhck,bhcv->bhkv', kt_i, v_new)
        if qd_i is None:
            return s_new, None
        o_i = jnp.einsum('bhck,bhkv->bhcv', qd_i, s) + jnp.einsum('bhij,bhjv->bhiv', at_i, v_new)
        return s_new, o_i

    s_fin, o = lax.scan(step, s0.astype(f32), xs)
    if o is None:
        return None, s_fin
    o = jnp.moveaxis(o, (0, 3), (1, 2)).reshape(b, t, h, dv).astype(v.dtype)
    return o, s_fin


def deltanet_inputs(proj, conv_w, a_log, dt_bias):
    b, t, _ = proj.shape
    qkv = jax.nn.silu(short_conv(proj[..., :GATE_START], conv_w)).reshape(b, t, 3, N_LIN_HEADS, HEAD_DIM)
    q = l2norm(qkv[:, :, 0]) * HEAD_DIM ** -0.5
    k = l2norm(qkv[:, :, 1])
    v = qkv[:, :, 2]
    z = proj[..., GATE_START:DECAY_START].reshape(b, t, N_LIN_HEADS, HEAD_DIM)
    ab = proj[..., DECAY_START:POOL_START].astype(jnp.float32).reshape(b, t, 2, N_DIR, N_LIN_HEADS)
    g = -jnp.exp(a_log.astype(jnp.float32)) * jax.nn.softplus(ab[:, :, 0] + dt_bias.astype(jnp.float32))
    beta = jax.nn.sigmoid(ab[:, :, 1])
    return q, k, v, z, g, beta


def gated_head_norm(o, z, w):
    return (rmsnorm(o, w) * jax.nn.silu(z)).reshape(o.shape[0], o.shape[1], LIN_W)


def flip_t(a):
    return jnp.flip(a, axis=1)


def box_mean(u, size, axis):
    n = u.shape[axis]
    idx = jnp.arange(n)
    lo = jnp.clip(idx - size // 2, 0, n - 1)
    hi = jnp.clip(idx + size - 1 - size // 2, 0, n - 1)
    pad = [(0, 0)] * u.ndim
    pad[axis] = (1, 0)
    csum = jnp.pad(jnp.cumsum(u.astype(jnp.float32), axis=axis), pad)
    total = jnp.take(csum, hi + 1, axis=axis) - jnp.take(csum, lo, axis=axis)
    shape = [1] * u.ndim
    shape[axis] = n
    cnt = (hi - lo + 1).astype(jnp.float32).reshape(shape)
    return (total / cnt).astype(u.dtype)


def pool_mixer(p, pool_w, pool_scale, on_grid):
    b, t, _ = p.shape
    outs = []
    for gi, size in enumerate(POOL_SIZES):
        pg = p[..., gi * POOL_GROUP_W:(gi + 1) * POOL_GROUP_W]
        if on_grid:
            rows = t // GRID_W
            pg2 = pg.reshape(b, rows, GRID_W, POOL_GROUP_W)
            mean = box_mean(box_mean(pg2, size, 1), size, 2).reshape(b, t, POOL_GROUP_W)
        else:
            mean = box_mean(pg, size, 1)
        outs.append((mean - pg) @ pool_w[gi])
    return jnp.concatenate(outs, axis=-1) * pool_scale


def sq_relu_mlp(xn, w_up, w_down):
    return jnp.square(jax.nn.relu(xn @ w_up)) @ w_down


def trunk_layer(x, h, mod, mod_c, norm1_w, norm2_w, w_in, conv_w, a_log, dt_bias, onorm_w,
                pool_w, pool_scale, w_out, w_up, w_down, ctx_out):
    b = x.shape[0]
    xn = modulate(rmsnorm(x, norm1_w), mod[..., 0, :], mod[..., 1, :])
    hn = modulate(rmsnorm(h, norm1_w), mod_c[..., 0, :], mod_c[..., 1, :])
    px = xn @ w_in
    ph = hn @ w_in

    qx, kx, vx, zx, gx, bx = deltanet_inputs(px, conv_w, a_log, dt_bias)
    qh, kh, vh, zh, gh, bh = deltanet_inputs(ph, conv_w, a_log, dt_bias)
    s0 = jnp.zeros((b, N_LIN_HEADS, HEAD_DIM, HEAD_DIM), jnp.float32)
    oh_f, sh_f = gated_delta_chunked(qh if ctx_out else None, kh, vh, gh[:, :, 0], bh[:, :, 0], s0)
    oh_b, sh_b = gated_delta_chunked(flip_t(qh) if ctx_out else None, flip_t(kh), flip_t(vh),
                                     flip_t(gh[:, :, 1]), flip_t(bh[:, :, 1]), s0)
    ox_f, _ = gated_delta_chunked(qx, kx, vx, gx[:, :, 0], bx[:, :, 0], sh_f)
    ox_b, _ = gated_delta_chunked(flip_t(qx), flip_t(kx), flip_t(vx), flip_t(gx[:, :, 1]), flip_t(bx[:, :, 1]), sh_b)
    lin_x = gated_head_norm(ox_f + flip_t(ox_b), zx, onorm_w)

    pool_x = pool_mixer(px[..., POOL_START:], pool_w, pool_scale, True)

    x = x + mod[..., 2, :] * (jnp.concatenate([lin_x, pool_x], axis=-1) @ w_out)
    xn2 = modulate(rmsnorm(x, norm2_w), mod[..., 3, :], mod[..., 4, :])
    x = x + mod[..., 5, :] * sq_relu_mlp(xn2, w_up, w_down)

    if not ctx_out:
        return x, None
    lin_h = gated_head_norm(oh_f + flip_t(oh_b), zh, onorm_w)
    pool_h = pool_mixer(ph[..., POOL_START:], pool_w, pool_scale, False)
    h = h + mod_c[..., 2, :] * (jnp.concatenate([lin_h, pool_h], axis=-1) @ w_out)
    hn2 = modulate(rmsnorm(h, norm2_w), mod_c[..., 3, :], mod_c[..., 4, :])
    h = h + mod_c[..., 5, :] * sq_relu_mlp(hn2, w_up, w_down)
    return x, h


def setup_inputs(seed: int = 0) -> dict:
    key = jax.random.key(seed)
    ks = jax.random.split(key, 20)
    f32 = jnp.float32

    def nrm(k, shape, s):
        return jax.random.normal(k, shape, f32) * s

    x = nrm(ks[0], (BATCH, SEQ, D_MODEL), 1.0)
    c = nrm(ks[1], (BATCH, D_MODEL), 1.0)
    ctx = nrm(ks[2], (BATCH, CTX_LEN, D_MODEL), 1.0)
    c_ctx = nrm(ks[3], (D_MODEL,), 1.0)
    w_ada = nrm(ks[4], (DEPTH, D_MODEL, N_MOD * D_MODEL), 0.5 * D_MODEL ** -0.5)
    b_ada = nrm(ks[5], (DEPTH, N_MOD * D_MODEL), 0.02)
    norm1_w = 1.0 + nrm(ks[6], (DEPTH, D_MODEL), 0.05)
    norm2_w = 1.0 + nrm(ks[7], (DEPTH, D_MODEL), 0.05)
    w_in = nrm(ks[8], (DEPTH, D_MODEL, IN_COLS), D_MODEL ** -0.5)
    conv_w = nrm(ks[9], (DEPTH, CONV_K, 3 * LIN_W), CONV_K ** -0.5)
    a_log = jnp.log(jax.random.uniform(ks[10], (DEPTH, N_DIR, N_LIN_HEADS), f32, 1.0, 16.0))
    dt = jnp.exp(jax.random.uniform(ks[11], (DEPTH, N_DIR, N_LIN_HEADS), f32, math.log(1e-3), math.log(1e-1)))
    dt_bias = dt + jnp.log(-jnp.expm1(-dt))
    onorm_w = 1.0 + nrm(ks[12], (DEPTH, HEAD_DIM), 0.05)
    pool_w = nrm(ks[13], (DEPTH, N_POOL_GROUPS, POOL_GROUP_W, POOL_GROUP_W), POOL_GROUP_W ** -0.5)
    pool_scale = 1.0 + nrm(ks[14], (DEPTH, POOL_W), 0.1)
    w_out = nrm(ks[15], (DEPTH, MIX_W, D_MODEL), MIX_W ** -0.5)
    w_up = nrm(ks[16], (DEPTH, D_MODEL, D_FF), D_MODEL ** -0.5)
    w_down = nrm(ks[17], (DEPTH, D_FF, D_MODEL), D_FF ** -0.5)
    final_norm_w = 1.0 + nrm(ks[18], (D_MODEL,), 0.05)
    return {'x': x, 'c': c, 'ctx': ctx, 'c_ctx': c_ctx, 'w_ada': w_ada, 'b_ada': b_ada,
            'norm1_w': norm1_w, 'norm2_w': norm2_w, 'w_in': w_in, 'conv_w': conv_w,
            'a_log': a_log, 'dt_bias': dt_bias, 'onorm_w': onorm_w, 'pool_w': pool_w,
            'pool_scale': pool_scale, 'w_out': w_out, 'w_up': w_up, 'w_down': w_down,
            'final_norm_w': final_norm_w}


def reference(x, c, ctx, c_ctx, w_ada, b_ada, norm1_w, norm2_w, w_in, conv_w, a_log, dt_bias,
              onorm_w, pool_w, pool_scale, w_out, w_up, w_down, final_norm_w):
    b = x.shape[0]
    sc = jax.nn.silu(c)
    scc = jax.nn.silu(c_ctx)
    h = ctx
    for l in range(DEPTH):
        mod = (sc @ w_ada[l] + b_ada[l]).reshape(b, 1, N_MOD, D_MODEL)
        mod_c = (scc @ w_ada[l] + b_ada[l]).reshape(N_MOD, D_MODEL)
        x, h = trunk_layer(x, h, mod, mod_c, norm1_w[l], norm2_w[l], w_in[l], conv_w[l], a_log[l],
                           dt_bias[l], onorm_w[l], pool_w[l], pool_scale[l], w_out[l], w_up[l],
                           w_down[l], l < DEPTH - 1)
    return rmsnorm(x, final_norm_w)
```

```python
import functools
import math

import jax
import jax.numpy as jnp
from jax import lax
from jax.experimental import pallas as pl
from jax.experimental.pallas import tpu as pltpu

F32 = jnp.float32
BF16 = jnp.bfloat16

EPS = 1e-6
N_MOD = 6
HEAD_DIM = 128
N_DIR = 2
CONV_K = 5
GRID_W = 64
POOL_SIZES = (2, 4, 8, 16)
CHUNK = 128
INV_BLOCK = 16
GATE_SLOTS = 8
HEADS_PER_STEP = 2
ROW_SLAB = 256
MOD_ROWS = 24
VMEM_LIMIT = 56 * 1024 * 1024


def _silu(x):
    return x * (1.0 / (1.0 + jnp.exp(-x)))


def _softplus(x):
    return jnp.maximum(x, 0.0) + jnp.log1p(jnp.exp(-jnp.abs(x)))


def _cparams(n_axes):
    return pltpu.CompilerParams(dimension_semantics=("arbitrary",) * n_axes,
                                vmem_limit_bytes=VMEM_LIMIT)


def _ada_kernel(c_ref, w_ref, b_ref, o_ref):
    sc = _silu(c_ref[...]).astype(BF16)
    o_ref[...] = jnp.dot(sc, w_ref[...].astype(BF16), preferred_element_type=F32) + b_ref[...]


def _ada(c_all, w_ada, b_ada):
    depth, d, n = w_ada.shape
    tn = 1024
    return pl.pallas_call(
        _ada_kernel,
        out_shape=jax.ShapeDtypeStruct((depth, MOD_ROWS, n), F32),
        grid=(depth, n // tn),
        in_specs=[pl.BlockSpec((MOD_ROWS, d), lambda l, j: (0, 0)),
                  pl.BlockSpec((None, d, tn), lambda l, j: (l, 0, j)),
                  pl.BlockSpec((None, 1, tn), lambda l, j: (l, 0, j))],
        out_specs=pl.BlockSpec((None, MOD_ROWS, tn), lambda l, j: (l, 0, j)),
        compiler_params=_cparams(2),
        name="ada_mod",
    )(c_all, w_ada, b_ada.reshape(depth, 1, n))


def _rms(x):
    return x * lax.rsqrt(jnp.mean(x * x, axis=-1, keepdims=True) + EPS)


def _norm_mod_to(x_ref, nw_ref, mod_ref, i_shift, i_scale, out_ref, rows=128):
    nw = nw_ref[...]
    shift = mod_ref[i_shift:i_shift + 1, :]
    scale = mod_ref[i_scale:i_scale + 1, :]

    def body(r, carry):
        sl = pl.ds(pl.multiple_of(r * rows, rows), rows)
        y = _rms(x_ref[sl, :]) * nw
        out_ref[sl, :] = (y * (1.0 + scale) + shift).astype(out_ref.dtype)
        return carry

    lax.fori_loop(0, x_ref.shape[0] // rows, body, 0)


def _inproj_kernel(x_ref, mod_ref, nw_ref, w_ref, wab_ref, wabt_ref, o_ref, ab_ref, abt_ref, xn_ref):
    @pl.when(pl.program_id(2) == 0)
    def _():
        _norm_mod_to(x_ref, nw_ref, mod_ref, 0, 1, xn_ref)
        xn = xn_ref[...]
        ab_ref[...] = jnp.dot(xn, wab_ref[...], preferred_element_type=F32)
        abt_ref[...] = lax.dot_general(wabt_ref[...], xn, (((1,), (1,)), ((), ())),
                                       preferred_element_type=F32)

    w = w_ref[...]
    for r in range(0, xn_ref.shape[0], ROW_SLAB):
        o_ref[r:r + ROW_SLAB, :] = jnp.dot(xn_ref[r:r + ROW_SLAB, :], w, preferred_element_type=F32)


def _inproj(x3, mod, mod_base, nw, w_main, w_ab, w_abt):
    nb, rows, d = x3.shape
    n = w_main.shape[1]
    tm = min(1024, rows)
    tn = 1024
    n_gate_rows = w_abt.shape[0]
    return pl.pallas_call(
        _inproj_kernel,
        out_shape=(jax.ShapeDtypeStruct((nb, rows, n), F32),
                   jax.ShapeDtypeStruct((nb, rows, 128), F32),
                   jax.ShapeDtypeStruct((nb, n_gate_rows, rows), F32)),
        grid=(nb, rows // tm, n // tn),
        in_specs=[pl.BlockSpec((None, tm, d), lambda b, i, j: (b, i, 0)),
                  pl.BlockSpec((None, N_MOD, d), lambda b, i, j: (mod_base + b, 0, 0)),
                  pl.BlockSpec((1, d), lambda b, i, j: (0, 0)),
                  pl.BlockSpec((d, tn), lambda b, i, j: (0, j)),
                  pl.BlockSpec((d, 128), lambda b, i, j: (0, 0)),
                  pl.BlockSpec((n_gate_rows, d), lambda b, i, j: (0, 0))],
        out_specs=(pl.BlockSpec((None, tm, tn), lambda b, i, j: (b, i, j)),
                   pl.BlockSpec((None, tm, 128), lambda b, i, j: (b, i, 0)),
                   pl.BlockSpec((None, n_gate_rows, tm), lambda b, i, j: (b, 0, i))),
        scratch_shapes=[pltpu.VMEM((tm, d), BF16)],
        compiler_params=_cparams(3),
        name="in_proj",
    )(x3, mod, nw, w_main, w_ab, w_abt)


def _delta_kernel(*refs, seq_x, seq_h, hg, ctx_out):
    n_in = 19
    (qx_ref, kx_ref, vx_ref, zx_ref, abx_ref, abtx_ref,
     qh_ref, kh_ref, vh_ref, zh_ref, abh_ref, abth_ref,
     cq_ref, ck_ref, cv_ref, alog_ref, dtb_ref, alogt_ref, dtbt_ref) = refs[:n_in]
    onorm_ref = refs[n_in]
    if ctx_out:
        ox_ref, oh_ref = refs[n_in + 1:n_in + 3]
        scratch = refs[n_in + 3:]
    else:
        ox_ref = refs[n_in + 1]
        oh_ref = None
        scratch = refs[n_in + 2:]
    (pad_s, q_s, k_s, v_s, o_s, gb_s, cum_s, cumt_s, u_s, wq_s, ktt_s, at_s, egl_s, st_s) = scratch

    c = CHUNK
    hd = HEAD_DIM
    width = hg * hd
    seq_t = seq_h + seq_x
    nc_h = seq_h // c
    nc_x = seq_x // c
    nc = nc_h + nc_x
    n_chain = hg * N_DIR

    def conv_part(src_ref, w_ref, dst_ref, row0, n_rows, mode):
        tile = 256 if n_rows % 256 == 0 else 128
        pad_s[0:8, :] = jnp.zeros((8, width), F32)
        pad_s[8:8 + n_rows, :] = src_ref[...]
        pad_s[8 + n_rows:16 + n_rows, :] = jnp.zeros((8, width), F32)
        taps = [w_ref[i:i + 1, :] for i in range(CONV_K)]

        def body(r, carry):
            start = pl.multiple_of(r * tile, tile)
            win = pad_s[pl.ds(start, tile + 16), :]
            acc = None
            for i in range(CONV_K):
                sh = (CONV_K // 2 - i) % (tile + 16)
                rolled = win if sh == 0 else pltpu.roll(win, sh, 0)
                term = rolled[8:8 + tile, :] * taps[i]
                acc = term if acc is None else acc + term
            y = _silu(acc)
            if mode != "v":
                cols = []
                for hl in range(hg):
                    yh = y[:, hl * hd:(hl + 1) * hd]
                    yh = yh * lax.rsqrt(jnp.sum(yh * yh, axis=-1, keepdims=True) + EPS)
                    cols.append(yh)
                y = cols[0] if hg == 1 else jnp.concatenate(cols, axis=1)
                if mode == "q":
                    y = y * (hd ** -0.5)
            dst_ref[pl.ds(pl.multiple_of(row0 + start, CHUNK), tile), :] = y
            return carry

        lax.fori_loop(0, n_rows // tile, body, 0)

    conv_part(qh_ref, cq_ref, q_s, 0, seq_h, "q")
    conv_part(qx_ref, cq_ref, q_s, seq_h, seq_x, "q")
    conv_part(kh_ref, ck_ref, k_s, 0, seq_h, "k")
    conv_part(kx_ref, ck_ref, k_s, seq_h, seq_x, "k")
    conv_part(vh_ref, cv_ref, v_s, 0, seq_h, "v")
    conv_part(vx_ref, cv_ref, v_s, seq_h, seq_x, "v")

    lane = lax.broadcasted_iota(jnp.int32, (c, 128), 1)
    slot_l = lane % GATE_SLOTS
    shift = (128 - pl.program_id(1) * (hg * GATE_SLOTS)) % 128
    alog_r = alog_ref[...]
    dtb_r = dtb_ref[...]

    def gates_cols(ab_ref, row0, n_rows):
        def body(r, carry):
            sl = pl.ds(pl.multiple_of(r * c, c), c)
            a = pltpu.roll(ab_ref[sl, :], shift, 1)
            g = -jnp.exp(alog_r) * _softplus(a + dtb_r)
            beta = 1.0 / (1.0 + jnp.exp(-a))
            gb_s[pl.ds(pl.multiple_of(row0 + r * c, c), c), :] = jnp.where(slot_l < N_DIR, g, beta)
            return carry

        lax.fori_loop(0, n_rows // c, body, 0)

    gates_cols(abh_ref, 0, seq_h)
    gates_cols(abx_ref, seq_h, seq_x)

    ri = lax.broadcasted_iota(jnp.int32, (c, c), 0)
    ci = lax.broadcasted_iota(jnp.int32, (c, c), 1)
    lower_incl = (ci <= ri)
    upper_incl = (ci >= ri)
    m_le = jnp.where(lower_incl, 1.0, 0.0).astype(BF16)
    m_ge = jnp.where(upper_incl, 1.0, 0.0).astype(BF16)

    def split3(x):
        hi = x.astype(BF16)
        r1 = x - hi.astype(F32)
        mid = r1.astype(BF16)
        lo = (r1 - mid.astype(F32)).astype(BF16)
        return hi, mid, lo

    alogt = alogt_ref[...]
    dtbt = dtbt_ref[...]
    n_gt = hg * GATE_SLOTS
    row_slot = lax.broadcasted_iota(jnp.int32, (n_gt, c), 0) % GATE_SLOTS

    def cums(abt_ref, chunk0, n_chunks):
        def body(r, carry):
            src = pl.ds(pl.multiple_of(r * c, c), c)
            dst = pl.ds(pl.multiple_of((chunk0 + r) * c, c), c)
            pieces = jnp.concatenate(split3(gb_s[dst, :]), axis=1)
            pre = jnp.dot(m_le, pieces, preferred_element_type=F32)
            suf = jnp.dot(m_ge, pieces, preferred_element_type=F32)
            pre = pre[:, 0:128] + pre[:, 128:256] + pre[:, 256:384]
            suf = suf[:, 0:128] + suf[:, 128:256] + suf[:, 256:384]
            cum_s[dst, :] = jnp.where(slot_l == 0, pre, suf)
            gt = -jnp.exp(alogt) * _softplus(abt_ref[:, src] + dtbt)
            pieces_t = jnp.concatenate(split3(gt), axis=0)
            pre_t = jnp.dot(pieces_t, m_ge, preferred_element_type=F32)
            suf_t = jnp.dot(pieces_t, m_le, preferred_element_type=F32)
            pre_t = pre_t[0:n_gt] + pre_t[n_gt:2 * n_gt] + pre_t[2 * n_gt:3 * n_gt]
            suf_t = suf_t[0:n_gt] + suf_t[n_gt:2 * n_gt] + suf_t[2 * n_gt:3 * n_gt]
            cumt_s[:, dst] = jnp.where(row_slot == 0, pre_t, suf_t)
            return carry

        lax.fori_loop(0, n_chunks, body, 0)

    cums(abth_ref, 0, nc_h)
    cums(abtx_ref, nc_h, nc_x)

    eye = jnp.where(ri == ci, 1.0, 0.0).astype(F32)
    masks = ((lower_incl, ci < ri), (upper_incl, ci > ri))

    def bdot(a, b):
        return jnp.dot(a.astype(BF16), b.astype(BF16), preferred_element_type=F32)

    same_blk = (ri // INV_BLOCK) == (ci // INV_BLOCK)

    def tri_inverse(nm, lower):
        nd = jnp.where(same_blk, nm, 0.0)
        m = eye + nd
        p = bdot(nd, nd)
        for _ in range(int(math.log2(INV_BLOCK)) - 2):
            r_ = bdot(p, jnp.concatenate([m, p], axis=1))
            m = m + r_[:, 0:c]
            p = r_[:, c:2 * c]
        m = m + bdot(p, m)
        s = INV_BLOCK
        while s < c:
            hit = [(a // s) % 2 == (1 if lower else 0) for a in range(0, c, s)]
            starts = list(range(0, c, s))

            def take(xm):
                return jnp.concatenate([xm[a:a + s] for a, t in zip(starts, hit) if t], axis=0)

            pair = ((ri // (2 * s)) == (ci // (2 * s))) & ((ri // s) != (ci // s))
            e_sel = take(jnp.where(pair, nm, 0.0))
            m_sel = take(m)
            x = bdot(e_sel, m).astype(BF16)
            zero = jnp.zeros((s, c), BF16)
            slabs, k_ = [], 0
            for t in hit:
                if t:
                    slabs.append(x[k_ * s:(k_ + 1) * s])
                    k_ += 1
                else:
                    slabs.append(zero)
            new_sel = m_sel + jnp.dot(m_sel.astype(BF16), jnp.concatenate(slabs, axis=0),
                                      preferred_element_type=F32)
            slabs, k_ = [], 0
            for a, t in zip(starts, hit):
                if t:
                    slabs.append(new_sel[k_ * s:(k_ + 1) * s])
                    k_ += 1
                else:
                    slabs.append(m[a:a + s])
            m = jnp.concatenate(slabs, axis=0)
            s *= 2
        return m

    def prep(ch, carry):
        rows = pl.ds(pl.multiple_of(ch * c, c), c)
        gbc = gb_s[rows, :]
        cumc = cum_s[rows, :]
        cumtc = cumt_s[:, rows]
        for hl in range(hg):
            cols = slice(hl * hd, (hl + 1) * hd)
            qc = q_s[rows, cols]
            kc = k_s[rows, cols]
            vc = v_s[rows, cols]
            qk = jnp.concatenate([qc, kc], axis=0).astype(BF16)
            aa = lax.dot_general(qk, kc.astype(BF16), (((1,), (1,)), ((), ())),
                                 preferred_element_type=F32)
            q_k = aa[0:c]
            k_k = aa[c:2 * c]
            for d in range(N_DIR):
                incl, strict = masks[d]
                base = hl * GATE_SLOTS
                beta_b = jnp.broadcast_to(gbc[:, base + N_DIR + d:base + N_DIR + d + 1], (c, hd))
                cum_b = jnp.broadcast_to(cumc[:, base + d:base + d + 1], (c, hd))
                cum_r = cumtc[base + d:base + d + 1, :]
                decay = jnp.where(incl, jnp.exp(jnp.where(incl, cum_b - cum_r, 0.0)), 0.0)
                nm = jnp.where(strict, -(beta_b * k_k * decay), 0.0)
                m = tri_inverse(nm, d == 0)
                e_c = jnp.exp(cum_b)
                rhs = jnp.concatenate([vc * beta_b, kc * (beta_b * e_c)], axis=1)
                uw = bdot(m, rhs)
                gl_b = cum_b[c - 1:c, :] if d == 0 else cum_b[0:1, :]
                kt = kc * jnp.exp(gl_b - cum_b)
                chain = hl * N_DIR + d
                slot = chain * nc + ch
                u_s[pl.ds(pl.multiple_of(slot * c, c), c), :] = uw[:, 0:hd]
                wq_s[pl.ds(pl.multiple_of(slot * 2 * c, c), c), :] = uw[:, hd:2 * hd].astype(BF16)
                wq_s[pl.ds(pl.multiple_of(slot * 2 * c + c, c), c), :] = (qc * e_c).astype(BF16)
                ktt_s[pl.ds(pl.multiple_of(slot * hd, hd), hd), :] = kt.T.astype(BF16)
                at_s[pl.ds(pl.multiple_of(slot * c, c), c), :] = (q_k * decay).astype(BF16)
                egl_s[pl.ds(pl.multiple_of(slot * 8, 8), 8), :] = jnp.broadcast_to(jnp.exp(gl_b), (8, hd))
        return carry

    lax.fori_loop(0, nc, prep, 0)

    st_s[...] = jnp.zeros(st_s.shape, F32)
    o_s[...] = jnp.zeros(o_s.shape, F32)

    def scan_step(ch_f, ch_b, with_q):
        for hl in range(hg):
            for d in range(N_DIR):
                ch = ch_f if d == 0 else ch_b
                chain = hl * N_DIR + d
                slot = chain * nc + ch
                st_rows = slice(chain * hd, (chain + 1) * hd)
                s_old = st_s[st_rows, :]
                s_b = s_old.astype(BF16)
                u = u_s[pl.ds(pl.multiple_of(slot * c, c), c), :]
                if with_q:
                    wq = wq_s[pl.ds(pl.multiple_of(slot * 2 * c, 2 * c), 2 * c), :]
                    r_ = jnp.dot(wq, s_b, preferred_element_type=F32)
                    v_new = u - r_[0:c]
                else:
                    w_ = wq_s[pl.ds(pl.multiple_of(slot * 2 * c, c), c), :]
                    v_new = u - jnp.dot(w_, s_b, preferred_element_type=F32)
                v_b = v_new.astype(BF16)
                ktt = ktt_s[pl.ds(pl.multiple_of(slot * hd, hd), hd), :]
                egl = egl_s[pl.ds(pl.multiple_of(slot * 8, 8), 1), :]
                st_s[st_rows, :] = s_old * egl + jnp.dot(ktt, v_b, preferred_element_type=F32)
                if with_q:
                    at = at_s[pl.ds(pl.multiple_of(slot * c, c), c), :]
                    o_c = r_[c:2 * c] + jnp.dot(at, v_b, preferred_element_type=F32)
                    orow = pl.ds(pl.multiple_of(ch * c, c), c)
                    ocol = slice(hl * hd, (hl + 1) * hd)
                    o_s[orow, ocol] = o_s[orow, ocol] + o_c

    def scan_h(it, carry):
        scan_step(it, nc_h - 1 - it, ctx_out)
        return carry

    def scan_x(it, carry):
        scan_step(nc_h + it, nc - 1 - it, True)
        return carry

    lax.fori_loop(0, nc_h, scan_h, 0)
    lax.fori_loop(0, nc_x, scan_x, 0)

    onw = onorm_ref[...]

    def finish(z_ref, out_ref, row0, n_rows):
        tile = 256 if n_rows % 256 == 0 else 128

        def body(r, carry):
            start = pl.multiple_of(r * tile, tile)
            src = pl.ds(pl.multiple_of(row0 + start, CHUNK), tile)
            dst = pl.ds(start, tile)
            outs = []
            for hl in range(hg):
                cols = slice(hl * hd, (hl + 1) * hd)
                o = o_s[src, cols]
                outs.append(_rms(o) * onw * _silu(z_ref[dst, cols]))
            y = outs[0] if hg == 1 else jnp.concatenate(outs, axis=1)
            out_ref[dst, :] = y.astype(out_ref.dtype)
            return carry

        lax.fori_loop(0, n_rows // tile, body, 0)

    finish(zx_ref, ox_ref, seq_h, seq_x)
    if ctx_out:
        finish(zh_ref, oh_ref, 0, seq_h)


def _delta(pxm, abx, abtx, phm, abh, abth, conv_w, alog_c, dtb_c, alog_t, dtb_t, onorm, n_heads, ctx_out):
    nb, seq_x, _ = pxm.shape
    seq_h = phm.shape[1]
    hg = HEADS_PER_STEP
    hd = HEAD_DIM
    width = hg * hd
    n_groups = n_heads // hg
    lin_w = n_heads * hd
    seq_t = seq_x + seq_h
    nc = seq_t // CHUNK
    n_chain = hg * N_DIR
    n_gt = hg * GATE_SLOTS

    def main_spec(rows, part):
        return pl.BlockSpec((None, rows, width), lambda b, g: (b, 0, part * n_groups + g))

    in_specs = (
        [main_spec(seq_x, p) for p in range(4)]
        + [pl.BlockSpec((None, seq_x, 128), lambda b, g: (b, 0, 0)),
           pl.BlockSpec((None, n_gt, seq_x), lambda b, g: (b, g, 0))]
        + [main_spec(seq_h, p) for p in range(4)]
        + [pl.BlockSpec((None, seq_h, 128), lambda b, g: (b, 0, 0)),
           pl.BlockSpec((None, n_gt, seq_h), lambda b, g: (b, g, 0))]
        + [pl.BlockSpec((CONV_K, width), lambda b, g, p=p: (0, p * n_groups + g)) for p in range(3)]
        + [pl.BlockSpec((None, 1, 128), lambda b, g: (g, 0, 0)),
           pl.BlockSpec((None, 1, 128), lambda b, g: (g, 0, 0)),
           pl.BlockSpec((n_gt, 128), lambda b, g: (g, 0)),
           pl.BlockSpec((n_gt, 128), lambda b, g: (g, 0)),
           pl.BlockSpec((1, hd), lambda b, g: (0, 0))]
    )
    out_shape = [jax.ShapeDtypeStruct((nb, seq_x, lin_w), BF16)]
    out_specs = [pl.BlockSpec((None, seq_x, width), lambda b, g: (b, 0, g))]
    if ctx_out:
        out_shape.append(jax.ShapeDtypeStruct((nb, seq_h, lin_w), BF16))
        out_specs.append(pl.BlockSpec((None, seq_h, width), lambda b, g: (b, 0, g)))
    scratch = [
        pltpu.VMEM((seq_x + 16, width), F32),
        pltpu.VMEM((seq_t, width), F32),
        pltpu.VMEM((seq_t, width), F32),
        pltpu.VMEM((seq_t, width), F32),
        pltpu.VMEM((seq_t, width), F32),
        pltpu.VMEM((seq_t, 128), F32),
        pltpu.VMEM((seq_t, 128), F32),
        pltpu.VMEM((n_gt, seq_t), F32),
        pltpu.VMEM((n_chain * nc * CHUNK, hd), F32),
        pltpu.VMEM((n_chain * nc * 2 * CHUNK, hd), BF16),
        pltpu.VMEM((n_chain * nc * hd, CHUNK), BF16),
        pltpu.VMEM((n_chain * nc * CHUNK, CHUNK), BF16),
        pltpu.VMEM((n_chain * nc * 8, hd), F32),
        pltpu.VMEM((n_chain * hd, hd), F32),
    ]
    args = ([pxm] * 4 + [abx, abtx] + [phm] * 4 + [abh, abth] + [conv_w] * 3
            + [alog_c, dtb_c, alog_t, dtb_t, onorm])
    res = pl.pallas_call(
        functools.partial(_delta_kernel, seq_x=seq_x, seq_h=seq_h, hg=hg, ctx_out=ctx_out),
        out_shape=tuple(out_shape),
        grid=(nb, n_groups),
        in_specs=in_specs,
        out_specs=tuple(out_specs),
        scratch_shapes=scratch,
        compiler_params=_cparams(2),
        name="gated_delta",
    )(*args)
    return res if ctx_out else (res[0], None)


def _pool_kernel(p_ref, w_ref, sc_ref, o_ref, y_s, *, on_grid, n_groups, gw):
    n = p_ref.shape[0]
    blk = 2 * GRID_W if on_grid else n
    period = GRID_W if on_grid else n
    ri = lax.broadcasted_iota(jnp.int32, (blk, blk), 0)
    ci = lax.broadcasted_iota(jnp.int32, (blk, blk), 1)
    pos = lax.broadcasted_iota(jnp.int32, (blk, gw), 0) % period

    for g in range(n_groups):
        size = POOL_SIZES[g]
        half = size // 2
        cols = slice(g * gw, (g + 1) * gw)
        if on_grid:
            n_rows = n // GRID_W
            for r in range(n_rows):
                lo = max(r - half, 0)
                hi = min(r + size - 1 - half, n_rows - 1)
                acc = p_ref[lo * GRID_W:(lo + 1) * GRID_W, cols]
                for rr in range(lo + 1, hi + 1):
                    acc = acc + p_ref[rr * GRID_W:(rr + 1) * GRID_W, cols]
                y_s[r * GRID_W:(r + 1) * GRID_W, :] = acc / float(hi - lo + 1)
            src = y_s
            src_cols = slice(0, gw)
        else:
            src = p_ref
            src_cols = cols
        band = ((ci >= ri - half) & (ci <= ri + size - 1 - half) & (ci // period == ri // period))
        band = jnp.where(band, 1.0, 0.0).astype(BF16)
        lo_c = jnp.maximum(pos - half, 0)
        hi_c = jnp.minimum(pos + size - 1 - half, period - 1)
        cnt = (hi_c - lo_c + 1).astype(F32)
        w_g = w_ref[g]
        scale = sc_ref[:, cols]

        def body(t, carry):
            rows = pl.ds(pl.multiple_of(t * blk, blk), blk)
            y = src[rows, src_cols]
            hi_p = y.astype(BF16)
            r1 = y - hi_p.astype(F32)
            mid_p = r1.astype(BF16)
            lo_p = (r1 - mid_p.astype(F32)).astype(BF16)
            tot = (jnp.dot(band, hi_p, preferred_element_type=F32)
                   + jnp.dot(band, mid_p, preferred_element_type=F32)
                   + jnp.dot(band, lo_p, preferred_element_type=F32))
            mean = tot / cnt
            dlt = (mean - p_ref[rows, cols]).astype(BF16)
            o_ref[rows, cols] = (jnp.dot(dlt, w_g, preferred_element_type=F32) * scale).astype(o_ref.dtype)
            return carry

        lax.fori_loop(0, n // blk, body, 0)


def _pool(pm, pool_w, pool_scale, part0, on_grid):
    nb, rows, _ = pm.shape
    n_groups, gw, _ = pool_w.shape
    pw = n_groups * gw
    return pl.pallas_call(
        functools.partial(_pool_kernel, on_grid=on_grid, n_groups=n_groups, gw=gw),
        out_shape=jax.ShapeDtypeStruct((nb, rows, pw), BF16),
        grid=(nb,),
        in_specs=[pl.BlockSpec((None, rows, pw), lambda b: (b, 0, part0)),
                  pl.BlockSpec((n_groups, gw, gw), lambda b: (0, 0, 0)),
                  pl.BlockSpec((1, pw), lambda b: (0, 0))],
        out_specs=pl.BlockSpec((None, rows, pw), lambda b: (b, 0, 0)),
        scratch_shapes=[pltpu.VMEM((rows, gw), F32)],
        compiler_params=_cparams(1),
        name="pool_mixer",
    )(pm, pool_w, pool_scale)


def _outproj_kernel(lin_ref, pool_ref, x_ref, mod_ref, wa_ref, wb_ref, o_ref):
    y = (jnp.dot(lin_ref[...], wa_ref[...], preferred_element_type=F32)
         + jnp.dot(pool_ref[...], wb_ref[...], preferred_element_type=F32))
    o_ref[...] = x_ref[...] + mod_ref[2:3, :] * y


def _outproj(lin, pool, x3, mod, mod_base, w_a, w_b):
    nb, rows, d = x3.shape
    tm = min(512, rows)
    ka = w_a.shape[0]
    kb = w_b.shape[0]
    return pl.pallas_call(
        _outproj_kernel,
        out_shape=jax.ShapeDtypeStruct((nb, rows, d), F32),
        grid=(nb, rows // tm),
        in_specs=[pl.BlockSpec((None, tm, ka), lambda b, i: (b, i, 0)),
                  pl.BlockSpec((None, tm, kb), lambda b, i: (b, i, 0)),
                  pl.BlockSpec((None, tm, d), lambda b, i: (b, i, 0)),
                  pl.BlockSpec((None, N_MOD, d), lambda b, i: (mod_base + b, 0, 0)),
                  pl.BlockSpec((ka, d), lambda b, i: (0, 0)),
                  pl.BlockSpec((kb, d), lambda b, i: (0, 0))],
        out_specs=pl.BlockSpec((None, tm, d), lambda b, i: (b, i, 0)),
        compiler_params=_cparams(2),
        name="out_proj",
    )(lin, pool, x3, mod, w_a, w_b)


def _mlp_kernel(x_ref, mod_ref, nw_ref, wu_ref, wd_ref, fw_ref, o_ref, xn_ref, *, final_norm):
    f = pl.program_id(2)

    @pl.when(f == 0)
    def _():
        _norm_mod_to(x_ref, nw_ref, mod_ref, 3, 4, xn_ref)

    @pl.when(f == 0)
    def _():
        o_ref[...] = jnp.zeros(o_ref.shape, F32)

    wu = wu_ref[...]
    wd = wd_ref[...]
    for r in range(0, xn_ref.shape[0], ROW_SLAB):
        h = jnp.dot(xn_ref[r:r + ROW_SLAB, :], wu, preferred_element_type=F32)
        h = jnp.square(jnp.maximum(h, 0.0)).astype(BF16)
        o_ref[r:r + ROW_SLAB, :] += jnp.dot(h, wd, preferred_element_type=F32)

    @pl.when(f == pl.num_programs(2) - 1)
    def _():
        gate = mod_ref[5:6, :]
        fw = fw_ref[...]
        rows = 128

        def body(r, carry):
            sl = pl.ds(pl.multiple_of(r * rows, rows), rows)
            y = x_ref[sl, :] + gate * o_ref[sl, :]
            if final_norm:
                y = _rms(y) * fw
            o_ref[sl, :] = y
            return carry

        lax.fori_loop(0, x_ref.shape[0] // rows, body, 0)


def _mlp(x3, mod, mod_base, nw, w_up, w_down, final_w, final_norm):
    nb, rows, d = x3.shape
    dff = w_up.shape[1]
    tm = min(1024, rows)
    tf = 512
    return pl.pallas_call(
        functools.partial(_mlp_kernel, final_norm=final_norm),
        out_shape=jax.ShapeDtypeStruct((nb, rows, d), F32),
        grid=(nb, rows // tm, dff // tf),
        in_specs=[pl.BlockSpec((None, tm, d), lambda b, i, f: (b, i, 0)),
                  pl.BlockSpec((None, N_MOD, d), lambda b, i, f: (mod_base + b, 0, 0)),
                  pl.BlockSpec((1, d), lambda b, i, f: (0, 0)),
                  pl.BlockSpec((d, tf), lambda b, i, f: (0, f)),
                  pl.BlockSpec((tf, d), lambda b, i, f: (f, 0)),
                  pl.BlockSpec((1, d), lambda b, i, f: (0, 0))],
        out_specs=pl.BlockSpec((None, tm, d), lambda b, i, f: (b, i, 0)),
        scratch_shapes=[pltpu.VMEM((tm, d), BF16)],
        compiler_params=_cparams(3),
        name="mlp",
    )(x3, mod, nw, w_up, w_down, final_w)


def _gate_layouts(w_in_l, a_log_l, dt_bias_l, lin_w, n_heads):
    d = w_in_l.shape[0]
    decay_start = 4 * lin_w
    ab = w_in_l[:, decay_start:decay_start + 2 * N_DIR * n_heads]
    ab = ab.reshape(d, 2 * N_DIR, n_heads)
    ab = jnp.transpose(ab, (0, 2, 1))
    ab = jnp.pad(ab, ((0, 0), (0, 0), (0, GATE_SLOTS - 2 * N_DIR)))
    w_ab = ab.reshape(d, n_heads * GATE_SLOTS)
    w_abt = w_ab.T.astype(BF16)
    w_ab = jnp.pad(w_ab, ((0, 0), (0, 128 - n_heads * GATE_SLOTS))).astype(BF16)

    def per_head(p):
        return jnp.pad(p.T.astype(F32), ((0, 0), (0, GATE_SLOTS - N_DIR)))

    hg = HEADS_PER_STEP
    n_groups = n_heads // hg
    alog = per_head(a_log_l)
    dtb = per_head(dt_bias_l)
    alog_c = jnp.pad(alog.reshape(n_groups, 1, hg * GATE_SLOTS), ((0, 0), (0, 0), (0, 128 - hg * GATE_SLOTS)))
    dtb_c = jnp.pad(dtb.reshape(n_groups, 1, hg * GATE_SLOTS), ((0, 0), (0, 0), (0, 128 - hg * GATE_SLOTS)))
    alog_t = jnp.broadcast_to(alog.reshape(n_heads * GATE_SLOTS, 1), (n_heads * GATE_SLOTS, 128))
    dtb_t = jnp.broadcast_to(dtb.reshape(n_heads * GATE_SLOTS, 1), (n_heads * GATE_SLOTS, 128))
    return w_ab, w_abt, alog_c, dtb_c, alog_t, dtb_t


def kernel(x, c, ctx, c_ctx, w_ada, b_ada, norm1_w, norm2_w, w_in, conv_w, a_log, dt_bias, onorm_w,
           pool_w, pool_scale, w_out, w_up, w_down, final_norm_w):
    nb, seq_x, d = x.shape
    seq_h = ctx.shape[1]
    depth = w_ada.shape[0]
    n_heads = a_log.shape[2]
    lin_w = n_heads * HEAD_DIM
    pool_width = pool_scale.shape[1]
    pool_start = 4 * lin_w + 2 * N_DIR * n_heads
    assert nb + 1 <= MOD_ROWS and seq_x % CHUNK == 0 and seq_h % CHUNK == 0
    assert n_heads % HEADS_PER_STEP == 0 and seq_x % GRID_W == 0

    c_all = jnp.concatenate([c, c_ctx[None, :], jnp.zeros((MOD_ROWS - nb - 1, d), F32)], axis=0)
    mod_all = _ada(c_all, w_ada, b_ada).reshape(depth, MOD_ROWS, N_MOD, d)

    h = ctx
    for l in range(depth):
        ctx_out = l < depth - 1
        mod = mod_all[l]
        w_main = jnp.concatenate([w_in[l, :, :4 * lin_w], w_in[l, :, pool_start:]], axis=1).astype(BF16)
        w_ab, w_abt, alog_c, dtb_c, alog_t, dtb_t = _gate_layouts(w_in[l], a_log[l], dt_bias[l], lin_w, n_heads)
        nw1 = norm1_w[l].reshape(1, d)
        nw2 = norm2_w[l].reshape(1, d)
        w_oa = w_out[l, :lin_w].astype(BF16)
        w_ob = w_out[l, lin_w:].astype(BF16)
        w_u = w_up[l].astype(BF16)
        w_d = w_down[l].astype(BF16)
        pw = pool_w[l].astype(BF16)
        ps = pool_scale[l].reshape(1, pool_width)
        fw = final_norm_w.reshape(1, d)

        pxm, abx, abtx = _inproj(x, mod, 0, nw1, w_main, w_ab, w_abt)
        h3 = h.reshape(1, nb * seq_h, d)
        phm, abh, abth = _inproj(h3, mod, nb, nw1, w_main, w_ab, w_abt)
        phm = phm.reshape(nb, seq_h, -1)
        abh = abh.reshape(nb, seq_h, 128)
        abth = jnp.transpose(abth.reshape(-1, nb, seq_h), (1, 0, 2))

        lin_x, lin_h = _delta(pxm, abx, abtx, phm, abh, abth, conv_w[l], alog_c, dtb_c, alog_t, dtb_t,
                              onorm_w[l].reshape(1, HEAD_DIM), n_heads, ctx_out)
        part0 = (4 * lin_w) // pool_width
        pool_x = _pool(pxm, pw, ps, part0, True)
        x = _outproj(lin_x, pool_x, x, mod, 0, w_oa, w_ob)
        x = _mlp(x, mod, 0, nw2, w_u, w_d, fw, l == depth - 1)
        if ctx_out:
            pool_h = _pool(phm, pw, ps, part0, False)
            h3 = _outproj(lin_h.reshape(1, nb * seq_h, lin_w), pool_h.reshape(1, nb * seq_h, pool_width),
                          h3, mod, nb, w_oa, w_ob)
            h = _mlp(h3, mod, nb, nw2, w_u, w_d, fw, False).reshape(nb, seq_h, d)
    return x
```

```python
import functools
import math

import jax
import jax.numpy as jnp
from jax import lax
from jax.experimental import pallas as pl
from jax.experimental.pallas import tpu as pltpu

F32 = jnp.float32
BF16 = jnp.bfloat16

EPS = 1e-6
N_MOD = 6
HEAD_DIM = 128
N_DIR = 2
CONV_K = 5
GRID_W = 64
POOL_SIZES = (2, 4, 8, 16)
CHUNK = 128
INV_BLOCK = 16
GATE_SLOTS = 8
HEADS_PER_STEP = 2
PREP_CHUNKS = 3
ROW_SLAB = 256
MOD_ROWS = 24
VMEM_LIMIT = 56 * 1024 * 1024


def _silu(x):
    return x * (1.0 / (1.0 + jnp.exp(-x)))


def _softplus(x):
    return jnp.maximum(x, 0.0) + jnp.log1p(jnp.exp(-jnp.abs(x)))


def _cparams(n_axes):
    return pltpu.CompilerParams(dimension_semantics=("arbitrary",) * n_axes,
                                vmem_limit_bytes=VMEM_LIMIT)


def _ada_kernel(c_ref, w_ref, b_ref, o_ref):
    sc = _silu(c_ref[...]).astype(BF16)
    o_ref[...] = jnp.dot(sc, w_ref[...].astype(BF16), preferred_element_type=F32) + b_ref[...]


def _ada(c_all, w_ada, b_ada):
    depth, d, n = w_ada.shape
    tn = 1024
    return pl.pallas_call(
        _ada_kernel,
        out_shape=jax.ShapeDtypeStruct((depth, MOD_ROWS, n), F32),
        grid=(depth, n // tn),
        in_specs=[pl.BlockSpec((MOD_ROWS, d), lambda l, j: (0, 0)),
                  pl.BlockSpec((None, d, tn), lambda l, j: (l, 0, j)),
                  pl.BlockSpec((None, 1, tn), lambda l, j: (l, 0, j))],
        out_specs=pl.BlockSpec((None, MOD_ROWS, tn), lambda l, j: (l, 0, j)),
        compiler_params=_cparams(2),
        name="ada_mod",
    )(c_all, w_ada, b_ada.reshape(depth, 1, n))


def _rms(x):
    return x * lax.rsqrt(jnp.mean(x * x, axis=-1, keepdims=True) + EPS)


def _norm_mod_to(x_ref, nw_ref, mod_ref, i_shift, i_scale, out_ref, rows=128):
    nw = nw_ref[...]
    shift = mod_ref[i_shift:i_shift + 1, :]
    scale = mod_ref[i_scale:i_scale + 1, :]

    def body(r, carry):
        sl = pl.ds(pl.multiple_of(r * rows, rows), rows)
        y = _rms(x_ref[sl, :]) * nw
        out_ref[sl, :] = (y * (1.0 + scale) + shift).astype(out_ref.dtype)
        return carry

    lax.fori_loop(0, x_ref.shape[0] // rows, body, 0)


def _inproj_kernel(x_ref, mod_ref, nw_ref, w_ref, wab_ref, wabt_ref, o_ref, ab_ref, abt_ref, xn_ref):
    @pl.when(pl.program_id(2) == 0)
    def _():
        _norm_mod_to(x_ref, nw_ref, mod_ref, 0, 1, xn_ref)
        xn = xn_ref[...]
        ab_ref[...] = jnp.dot(xn, wab_ref[...], preferred_element_type=F32)
        abt_ref[...] = lax.dot_general(wabt_ref[...], xn, (((1,), (1,)), ((), ())),
                                       preferred_element_type=F32)

    w = w_ref[...]
    for r in range(0, xn_ref.shape[0], ROW_SLAB):
        o_ref[r:r + ROW_SLAB, :] = jnp.dot(xn_ref[r:r + ROW_SLAB, :], w, preferred_element_type=F32)


def _inproj(x3, mod, mod_base, nw, w_main, w_ab, w_abt):
    nb, rows, d = x3.shape
    n = w_main.shape[1]
    tm = min(1024, rows)
    tn = 1024
    n_gate_rows = w_abt.shape[0]
    return pl.pallas_call(
        _inproj_kernel,
        out_shape=(jax.ShapeDtypeStruct((nb, rows, n), F32),
                   jax.ShapeDtypeStruct((nb, rows, 128), F32),
                   jax.ShapeDtypeStruct((nb, n_gate_rows, rows), F32)),
        grid=(nb, rows // tm, n // tn),
        in_specs=[pl.BlockSpec((None, tm, d), lambda b, i, j: (b, i, 0)),
                  pl.BlockSpec((None, N_MOD, d), lambda b, i, j: (mod_base + b, 0, 0)),
                  pl.BlockSpec((1, d), lambda b, i, j: (0, 0)),
                  pl.BlockSpec((d, tn), lambda b, i, j: (0, j)),
                  pl.BlockSpec((d, 128), lambda b, i, j: (0, 0)),
                  pl.BlockSpec((n_gate_rows, d), lambda b, i, j: (0, 0))],
        out_specs=(pl.BlockSpec((None, tm, tn), lambda b, i, j: (b, i, j)),
                   pl.BlockSpec((None, tm, 128), lambda b, i, j: (b, i, 0)),
                   pl.BlockSpec((None, n_gate_rows, tm), lambda b, i, j: (b, 0, i))),
        scratch_shapes=[pltpu.VMEM((tm, d), BF16)],
        compiler_params=_cparams(3),
        name="in_proj",
    )(x3, mod, nw, w_main, w_ab, w_abt)


def _delta_kernel(*refs, seq_x, seq_h, hg, ctx_out):
    n_in = 19
    (qx_ref, kx_ref, vx_ref, zx_ref, abx_ref, abtx_ref,
     qh_ref, kh_ref, vh_ref, zh_ref, abh_ref, abth_ref,
     cq_ref, ck_ref, cv_ref, alog_ref, dtb_ref, alogt_ref, dtbt_ref) = refs[:n_in]
    onorm_ref = refs[n_in]
    if ctx_out:
        ox_ref, oh_ref = refs[n_in + 1:n_in + 3]
        scratch = refs[n_in + 3:]
    else:
        ox_ref = refs[n_in + 1]
        oh_ref = None
        scratch = refs[n_in + 2:]
    (pad_s, q_s, k_s, v_s, o_s, gb_s, cum_s, cumt_s, u_s, wq_s, ktt_s, at_s, egl_s, st_s) = scratch

    c = CHUNK
    hd = HEAD_DIM
    width = hg * hd
    seq_t = seq_h + seq_x
    nc_h = seq_h // c
    nc_x = seq_x // c
    nc = nc_h + nc_x
    n_chain = hg * N_DIR

    def conv_part(src_ref, w_ref, dst_ref, row0, n_rows, mode):
        tile = 256 if n_rows % 256 == 0 else 128
        pad_s[0:8, :] = jnp.zeros((8, width), F32)
        pad_s[8:8 + n_rows, :] = src_ref[...]
        pad_s[8 + n_rows:16 + n_rows, :] = jnp.zeros((8, width), F32)
        taps = [w_ref[i:i + 1, :] for i in range(CONV_K)]

        def body(r, carry):
            start = pl.multiple_of(r * tile, tile)
            win = pad_s[pl.ds(start, tile + 16), :]
            acc = None
            for i in range(CONV_K):
                sh = (CONV_K // 2 - i) % (tile + 16)
                rolled = win if sh == 0 else pltpu.roll(win, sh, 0)
                term = rolled[8:8 + tile, :] * taps[i]
                acc = term if acc is None else acc + term
            y = _silu(acc)
            if mode != "v":
                cols = []
                for hl in range(hg):
                    yh = y[:, hl * hd:(hl + 1) * hd]
                    yh = yh * lax.rsqrt(jnp.sum(yh * yh, axis=-1, keepdims=True) + EPS)
                    cols.append(yh)
                y = cols[0] if hg == 1 else jnp.concatenate(cols, axis=1)
                if mode == "q":
                    y = y * (hd ** -0.5)
            dst_ref[pl.ds(pl.multiple_of(row0 + start, CHUNK), tile), :] = y
            return carry

        lax.fori_loop(0, n_rows // tile, body, 0)

    conv_part(qh_ref, cq_ref, q_s, 0, seq_h, "q")
    conv_part(qx_ref, cq_ref, q_s, seq_h, seq_x, "q")
    conv_part(kh_ref, ck_ref, k_s, 0, seq_h, "k")
    conv_part(kx_ref, ck_ref, k_s, seq_h, seq_x, "k")
    conv_part(vh_ref, cv_ref, v_s, 0, seq_h, "v")
    conv_part(vx_ref, cv_ref, v_s, seq_h, seq_x, "v")

    lane = lax.broadcasted_iota(jnp.int32, (c, 128), 1)
    slot_l = lane % GATE_SLOTS
    shift = (128 - pl.program_id(1) * (hg * GATE_SLOTS)) % 128
    alog_r = alog_ref[...]
    dtb_r = dtb_ref[...]

    def gates_cols(ab_ref, row0, n_rows):
        def body(r, carry):
            sl = pl.ds(pl.multiple_of(r * c, c), c)
            a = pltpu.roll(ab_ref[sl, :], shift, 1)
            g = -jnp.exp(alog_r) * _softplus(a + dtb_r)
            beta = 1.0 / (1.0 + jnp.exp(-a))
            gb_s[pl.ds(pl.multiple_of(row0 + r * c, c), c), :] = jnp.where(slot_l < N_DIR, g, beta)
            return carry

        lax.fori_loop(0, n_rows // c, body, 0, unroll=2)

    gates_cols(abh_ref, 0, seq_h)
    gates_cols(abx_ref, seq_h, seq_x)

    ri = lax.broadcasted_iota(jnp.int32, (c, c), 0)
    ci = lax.broadcasted_iota(jnp.int32, (c, c), 1)
    lower_incl = (ci <= ri)
    upper_incl = (ci >= ri)
    m_le = jnp.where(lower_incl, 1.0, 0.0).astype(BF16)
    m_ge = jnp.where(upper_incl, 1.0, 0.0).astype(BF16)

    def split3(x):
        hi = x.astype(BF16)
        r1 = x - hi.astype(F32)
        mid = r1.astype(BF16)
        lo = (r1 - mid.astype(F32)).astype(BF16)
        return hi, mid, lo

    alogt = alogt_ref[...]
    dtbt = dtbt_ref[...]
    n_gt = hg * GATE_SLOTS
    row_slot = lax.broadcasted_iota(jnp.int32, (n_gt, c), 0) % GATE_SLOTS

    def cums(abt_ref, chunk0, n_chunks):
        def body(r, carry):
            src = pl.ds(pl.multiple_of(r * c, c), c)
            dst = pl.ds(pl.multiple_of((chunk0 + r) * c, c), c)
            pieces = jnp.concatenate(split3(gb_s[dst, :]), axis=1)
            pre = jnp.dot(m_le, pieces, preferred_element_type=F32)
            suf = jnp.dot(m_ge, pieces, preferred_element_type=F32)
            pre = pre[:, 0:128] + pre[:, 128:256] + pre[:, 256:384]
            suf = suf[:, 0:128] + suf[:, 128:256] + suf[:, 256:384]
            cum_s[dst, :] = jnp.where(slot_l == 0, pre, suf)
            gt = -jnp.exp(alogt) * _softplus(abt_ref[:, src] + dtbt)
            pieces_t = jnp.concatenate(split3(gt), axis=0)
            pre_t = jnp.dot(pieces_t, m_ge, preferred_element_type=F32)
            suf_t = jnp.dot(pieces_t, m_le, preferred_element_type=F32)
            pre_t = pre_t[0:n_gt] + pre_t[n_gt:2 * n_gt] + pre_t[2 * n_gt:3 * n_gt]
            suf_t = suf_t[0:n_gt] + suf_t[n_gt:2 * n_gt] + suf_t[2 * n_gt:3 * n_gt]
            cumt_s[:, dst] = jnp.where(row_slot == 0, pre_t, suf_t)
            return carry

        lax.fori_loop(0, n_chunks, body, 0, unroll=2)

    cums(abth_ref, 0, nc_h)
    cums(abtx_ref, nc_h, nc_x)

    eye = jnp.where(ri == ci, 1.0, 0.0).astype(F32)
    masks = ((lower_incl, ci < ri), (upper_incl, ci > ri))

    def bdot(a, b):
        return jnp.dot(a.astype(BF16), b.astype(BF16), preferred_element_type=F32)

    same_blk = (ri // INV_BLOCK) == (ci // INV_BLOCK)

    def tri_inverse(nms, lowers):
        nds = [jnp.where(same_blk, nm, 0.0) for nm in nms]
        ms = [eye + nd for nd in nds]
        ps = [bdot(nd, nd) for nd in nds]
        for _ in range(int(math.log2(INV_BLOCK)) - 2):
            rs = [bdot(p, jnp.concatenate([m, p], axis=1)) for m, p in zip(ms, ps)]
            ms = [m + r_[:, 0:c] for m, r_ in zip(ms, rs)]
            ps = [r_[:, c:2 * c] for r_ in rs]
        rs = [bdot(p, m) for m, p in zip(ms, ps)]
        ms = [m + r_ for m, r_ in zip(ms, rs)]
        s = INV_BLOCK
        while s < c:
            starts = list(range(0, c, s))
            pair = ((ri // (2 * s)) == (ci // (2 * s))) & ((ri // s) != (ci // s))
            zero = jnp.zeros((s, c), BF16)
            hits = [[(a // s) % 2 == (1 if lower else 0) for a in starts] for lower in lowers]

            def take(xm, hit):
                return jnp.concatenate([xm[a:a + s] for a, t in zip(starts, hit) if t], axis=0)

            m_sels = [take(m, hit) for m, hit in zip(ms, hits)]
            xs = [bdot(take(jnp.where(pair, nm, 0.0), hit), m).astype(BF16)
                  for nm, m, hit in zip(nms, ms, hits)]
            ys = []
            for x, m_sel, hit in zip(xs, m_sels, hits):
                slabs, k_ = [], 0
                for t in hit:
                    slabs.append(x[k_ * s:(k_ + 1) * s] if t else zero)
                    k_ += 1 if t else 0
                ys.append(jnp.dot(m_sel.astype(BF16), jnp.concatenate(slabs, axis=0),
                                  preferred_element_type=F32))
            new_ms = []
            for m, m_sel, y, hit in zip(ms, m_sels, ys, hits):
                new_sel = m_sel + y
                slabs, k_ = [], 0
                for a, t in zip(starts, hit):
                    slabs.append(new_sel[k_ * s:(k_ + 1) * s] if t else m[a:a + s])
                    k_ += 1 if t else 0
                new_ms.append(jnp.concatenate(slabs, axis=0))
            ms = new_ms
            s *= 2
        return ms

    n_prep = PREP_CHUNKS if nc % PREP_CHUNKS == 0 else 1

    def prep(grp, carry):
        chs = [grp * n_prep + j for j in range(n_prep)]
        rows = [pl.ds(pl.multiple_of(ch * c, c), c) for ch in chs]
        gbc = [gb_s[r, :] for r in rows]
        cumc = [cum_s[r, :] for r in rows]
        cumtc = [cumt_s[:, r] for r in rows]
        qs, ks, vs, aas = {}, {}, {}, {}
        for j in range(n_prep):
            for hl in range(hg):
                cols = slice(hl * hd, (hl + 1) * hd)
                qs[j, hl] = q_s[rows[j], cols]
                ks[j, hl] = k_s[rows[j], cols]
                vs[j, hl] = v_s[rows[j], cols]
                qk = jnp.concatenate([qs[j, hl], ks[j, hl]], axis=0).astype(BF16)
                aas[j, hl] = lax.dot_general(qk, ks[j, hl].astype(BF16), (((1,), (1,)), ((), ())),
                                             preferred_element_type=F32)
        probs = [(j, hl, d) for j in range(n_prep) for hl in range(hg) for d in range(N_DIR)]
        nms, betas, cums, decays = [], [], [], []
        for j, hl, d in probs:
            incl, strict = masks[d]
            base = hl * GATE_SLOTS
            beta_b = jnp.broadcast_to(gbc[j][:, base + N_DIR + d:base + N_DIR + d + 1], (c, hd))
            cum_b = jnp.broadcast_to(cumc[j][:, base + d:base + d + 1], (c, hd))
            cum_r = cumtc[j][base + d:base + d + 1, :]
            decay = jnp.where(incl, jnp.exp(jnp.where(incl, cum_b - cum_r, 0.0)), 0.0)
            nms.append(jnp.where(strict, -(beta_b * aas[j, hl][c:2 * c] * decay), 0.0))
            betas.append(beta_b)
            cums.append(cum_b)
            decays.append(decay)
        ms = tri_inverse(nms, [d == 0 for _, _, d in probs])
        e_cs = [jnp.exp(cum_b) for cum_b in cums]
        uws = [bdot(m, jnp.concatenate([vs[j, hl] * beta_b, ks[j, hl] * (beta_b * e_c)], axis=1))
               for (j, hl, d), m, beta_b, e_c in zip(probs, ms, betas, e_cs)]
        for i, (j, hl, d) in enumerate(probs):
            cum_b, e_c, uw = cums[i], e_cs[i], uws[i]
            gl_b = cum_b[c - 1:c, :] if d == 0 else cum_b[0:1, :]
            kt = ks[j, hl] * jnp.exp(gl_b - cum_b)
            slot = (hl * N_DIR + d) * nc + chs[j]
            u_s[pl.ds(pl.multiple_of(slot * c, c), c), :] = uw[:, 0:hd]
            wq_s[pl.ds(pl.multiple_of(slot * 2 * c, c), c), :] = uw[:, hd:2 * hd].astype(BF16)
            wq_s[pl.ds(pl.multiple_of(slot * 2 * c + c, c), c), :] = (qs[j, hl] * e_c).astype(BF16)
            ktt_s[pl.ds(pl.multiple_of(slot * hd, hd), hd), :] = kt.T.astype(BF16)
            at_s[pl.ds(pl.multiple_of(slot * c, c), c), :] = (aas[j, hl][0:c] * decays[i]).astype(BF16)
            egl_s[pl.ds(pl.multiple_of(slot * 8, 8), 8), :] = jnp.broadcast_to(jnp.exp(gl_b), (8, hd))
        return carry

    lax.fori_loop(0, nc // n_prep, prep, 0)

    st_s[...] = jnp.zeros(st_s.shape, F32)
    o_s[...] = jnp.zeros(o_s.shape, F32)

    def scan_step(ch_f, ch_b, with_q):
        chains = [(hl, d) for hl in range(hg) for d in range(N_DIR)]
        chs = [ch_f if d == 0 else ch_b for _, d in chains]
        slots = [i * nc + ch for i, ch in enumerate(chs)]
        s_olds = [st_s[i * hd:(i + 1) * hd, :] for i in range(len(chains))]
        n_lhs = 2 * c if with_q else c
        rs = [jnp.dot(wq_s[pl.ds(pl.multiple_of(slot * 2 * c, 2 * c), n_lhs), :], s_old.astype(BF16),
                      preferred_element_type=F32) for slot, s_old in zip(slots, s_olds)]
        v_bs = [(u_s[pl.ds(pl.multiple_of(slot * c, c), c), :] - r_[0:c]).astype(BF16)
                for slot, r_ in zip(slots, rs)]
        kvs = [jnp.dot(ktt_s[pl.ds(pl.multiple_of(slot * hd, hd), hd), :], v_b, preferred_element_type=F32)
               for slot, v_b in zip(slots, v_bs)]
        if with_q:
            avs = [jnp.dot(at_s[pl.ds(pl.multiple_of(slot * c, c), c), :], v_b, preferred_element_type=F32)
                   for slot, v_b in zip(slots, v_bs)]
        for i, (hl, d) in enumerate(chains):
            egl = egl_s[pl.ds(pl.multiple_of(slots[i] * 8, 8), 1), :]
            st_s[i * hd:(i + 1) * hd, :] = s_olds[i] * egl + kvs[i]
            if with_q:
                orow = pl.ds(pl.multiple_of(chs[i] * c, c), c)
                ocol = slice(hl * hd, (hl + 1) * hd)
                o_s[orow, ocol] = o_s[orow, ocol] + (rs[i][c:2 * c] + avs[i])

    def scan_h(it, carry):
        scan_step(it, nc_h - 1 - it, ctx_out)
        return carry

    def scan_x(it, carry):
        scan_step(nc_h + it, nc - 1 - it, True)
        return carry

    lax.fori_loop(0, nc_h, scan_h, 0)
    lax.fori_loop(0, nc_x, scan_x, 0)

    onw = onorm_ref[...]

    def finish(z_ref, out_ref, row0, n_rows):
        tile = 256 if n_rows % 256 == 0 else 128

        def body(r, carry):
            start = pl.multiple_of(r * tile, tile)
            src = pl.ds(pl.multiple_of(row0 + start, CHUNK), tile)
            dst = pl.ds(start, tile)
            outs = []
            for hl in range(hg):
                cols = slice(hl * hd, (hl + 1) * hd)
                o = o_s[src, cols]
                outs.append(_rms(o) * onw * _silu(z_ref[dst, cols]))
            y = outs[0] if hg == 1 else jnp.concatenate(outs, axis=1)
            out_ref[dst, :] = y.astype(out_ref.dtype)
            return carry

        lax.fori_loop(0, n_rows // tile, body, 0)

    finish(zx_ref, ox_ref, seq_h, seq_x)
    if ctx_out:
        finish(zh_ref, oh_ref, 0, seq_h)


def _delta(pxm, abx, abtx, phm, abh, abth, conv_w, alog_c, dtb_c, alog_t, dtb_t, onorm, n_heads, ctx_out):
    nb, seq_x, _ = pxm.shape
    seq_h = phm.shape[1]
    hg = HEADS_PER_STEP
    hd = HEAD_DIM
    width = hg * hd
    n_groups = n_heads // hg
    lin_w = n_heads * hd
    seq_t = seq_x + seq_h
    nc = seq_t // CHUNK
    n_chain = hg * N_DIR
    n_gt = hg * GATE_SLOTS

    def main_spec(rows, part):
        return pl.BlockSpec((None, rows, width), lambda b, g: (b, 0, part * n_groups + g))

    in_specs = (
        [main_spec(seq_x, p) for p in range(4)]
        + [pl.BlockSpec((None, seq_x, 128), lambda b, g: (b, 0, 0)),
           pl.BlockSpec((None, n_gt, seq_x), lambda b, g: (b, g, 0))]
        + [main_spec(seq_h, p) for p in range(4)]
        + [pl.BlockSpec((None, seq_h, 128), lambda b, g: (b, 0, 0)),
           pl.BlockSpec((None, n_gt, seq_h), lambda b, g: (b, g, 0))]
        + [pl.BlockSpec((CONV_K, width), lambda b, g, p=p: (0, p * n_groups + g)) for p in range(3)]
        + [pl.BlockSpec((None, 1, 128), lambda b, g: (g, 0, 0)),
           pl.BlockSpec((None, 1, 128), lambda b, g: (g, 0, 0)),
           pl.BlockSpec((n_gt, 128), lambda b, g: (g, 0)),
           pl.BlockSpec((n_gt, 128), lambda b, g: (g, 0)),
           pl.BlockSpec((1, hd), lambda b, g: (0, 0))]
    )
    out_shape = [jax.ShapeDtypeStruct((nb, seq_x, lin_w), BF16)]
    out_specs = [pl.BlockSpec((None, seq_x, width), lambda b, g: (b, 0, g))]
    if ctx_out:
        out_shape.append(jax.ShapeDtypeStruct((nb, seq_h, lin_w), BF16))
        out_specs.append(pl.BlockSpec((None, seq_h, width), lambda b, g: (b, 0, g)))
    scratch = [
        pltpu.VMEM((seq_x + 16, width), F32),
        pltpu.VMEM((seq_t, width), F32),
        pltpu.VMEM((seq_t, width), F32),
        pltpu.VMEM((seq_t, width), F32),
        pltpu.VMEM((seq_t, width), F32),
        pltpu.VMEM((seq_t, 128), F32),
        pltpu.VMEM((seq_t, 128), F32),
        pltpu.VMEM((n_gt, seq_t), F32),
        pltpu.VMEM((n_chain * nc * CHUNK, hd), F32),
        pltpu.VMEM((n_chain * nc * 2 * CHUNK, hd), BF16),
        pltpu.VMEM((n_chain * nc * hd, CHUNK), BF16),
        pltpu.VMEM((n_chain * nc * CHUNK, CHUNK), BF16),
        pltpu.VMEM((n_chain * nc * 8, hd), F32),
        pltpu.VMEM((n_chain * hd, hd), F32),
    ]
    args = ([pxm] * 4 + [abx, abtx] + [phm] * 4 + [abh, abth] + [conv_w] * 3
            + [alog_c, dtb_c, alog_t, dtb_t, onorm])
    res = pl.pallas_call(
        functools.partial(_delta_kernel, seq_x=seq_x, seq_h=seq_h, hg=hg, ctx_out=ctx_out),
        out_shape=tuple(out_shape),
        grid=(nb, n_groups),
        in_specs=in_specs,
        out_specs=tuple(out_specs),
        scratch_shapes=scratch,
        compiler_params=_cparams(2),
        name="gated_delta",
    )(*args)
    return res if ctx_out else (res[0], None)


def _pool_kernel(p_ref, w_ref, sc_ref, o_ref, y_s, *, on_grid, n_groups, gw):
    n = p_ref.shape[0]
    blk = 2 * GRID_W if on_grid else n
    period = GRID_W if on_grid else n
    ri = lax.broadcasted_iota(jnp.int32, (blk, blk), 0)
    ci = lax.broadcasted_iota(jnp.int32, (blk, blk), 1)
    pos = lax.broadcasted_iota(jnp.int32, (blk, gw), 0) % period

    for g in range(n_groups):
        size = POOL_SIZES[g]
        half = size // 2
        cols = slice(g * gw, (g + 1) * gw)
        if on_grid:
            n_rows = n // GRID_W
            for r in range(n_rows):
                lo = max(r - half, 0)
                hi = min(r + size - 1 - half, n_rows - 1)
                acc = p_ref[lo * GRID_W:(lo + 1) * GRID_W, cols]
                for rr in range(lo + 1, hi + 1):
                    acc = acc + p_ref[rr * GRID_W:(rr + 1) * GRID_W, cols]
                y_s[r * GRID_W:(r + 1) * GRID_W, :] = acc / float(hi - lo + 1)
            src = y_s
            src_cols = slice(0, gw)
        else:
            src = p_ref
            src_cols = cols
        band = ((ci >= ri - half) & (ci <= ri + size - 1 - half) & (ci // period == ri // period))
        band = jnp.where(band, 1.0, 0.0).astype(BF16)
        lo_c = jnp.maximum(pos - half, 0)
        hi_c = jnp.minimum(pos + size - 1 - half, period - 1)
        cnt = (hi_c - lo_c + 1).astype(F32)
        w_g = w_ref[g]
        scale = sc_ref[:, cols]

        def body(t, carry):
            rows = pl.ds(pl.multiple_of(t * blk, blk), blk)
            y = src[rows, src_cols]
            hi_p = y.astype(BF16)
            r1 = y - hi_p.astype(F32)
            mid_p = r1.astype(BF16)
            lo_p = (r1 - mid_p.astype(F32)).astype(BF16)
            tot = (jnp.dot(band, hi_p, preferred_element_type=F32)
                   + jnp.dot(band, mid_p, preferred_element_type=F32)
                   + jnp.dot(band, lo_p, preferred_element_type=F32))
            mean = tot / cnt
            dlt = (mean - p_ref[rows, cols]).astype(BF16)
            o_ref[rows, cols] = (jnp.dot(dlt, w_g, preferred_element_type=F32) * scale).astype(o_ref.dtype)
            return carry

        lax.fori_loop(0, n // blk, body, 0)


def _pool(pm, pool_w, pool_scale, part0, on_grid):
    nb, rows, _ = pm.shape
    n_groups, gw, _ = pool_w.shape
    pw = n_groups * gw
    return pl.pallas_call(
        functools.partial(_pool_kernel, on_grid=on_grid, n_groups=n_groups, gw=gw),
        out_shape=jax.ShapeDtypeStruct((nb, rows, pw), BF16),
        grid=(nb,),
        in_specs=[pl.BlockSpec((None, rows, pw), lambda b: (b, 0, part0)),
                  pl.BlockSpec((n_groups, gw, gw), lambda b: (0, 0, 0)),
                  pl.BlockSpec((1, pw), lambda b: (0, 0))],
        out_specs=pl.BlockSpec((None, rows, pw), lambda b: (b, 0, 0)),
        scratch_shapes=[pltpu.VMEM((rows, gw), F32)],
        compiler_params=_cparams(1),
        name="pool_mixer",
    )(pm, pool_w, pool_scale)


def _outproj_kernel(lin_ref, pool_ref, x_ref, mod_ref, wa_ref, wb_ref, o_ref):
    y = (jnp.dot(lin_ref[...], wa_ref[...], preferred_element_type=F32)
         + jnp.dot(pool_ref[...], wb_ref[...], preferred_element_type=F32))
    o_ref[...] = x_ref[...] + mod_ref[2:3, :] * y


def _outproj(lin, pool, x3, mod, mod_base, w_a, w_b):
    nb, rows, d = x3.shape
    tm = min(512, rows)
    ka = w_a.shape[0]
    kb = w_b.shape[0]
    return pl.pallas_call(
        _outproj_kernel,
        out_shape=jax.ShapeDtypeStruct((nb, rows, d), F32),
        grid=(nb, rows // tm),
        in_specs=[pl.BlockSpec((None, tm, ka), lambda b, i: (b, i, 0)),
                  pl.BlockSpec((None, tm, kb), lambda b, i: (b, i, 0)),
                  pl.BlockSpec((None, tm, d), lambda b, i: (b, i, 0)),
                  pl.BlockSpec((None, N_MOD, d), lambda b, i: (mod_base + b, 0, 0)),
                  pl.BlockSpec((ka, d), lambda b, i: (0, 0)),
                  pl.BlockSpec((kb, d), lambda b, i: (0, 0))],
        out_specs=pl.BlockSpec((None, tm, d), lambda b, i: (b, i, 0)),
        compiler_params=_cparams(2),
        name="out_proj",
    )(lin, pool, x3, mod, w_a, w_b)


def _mlp_kernel(x_ref, mod_ref, nw_ref, wu_ref, wd_ref, fw_ref, o_ref, xn_ref, *, final_norm):
    f = pl.program_id(2)

    @pl.when(f == 0)
    def _():
        _norm_mod_to(x_ref, nw_ref, mod_ref, 3, 4, xn_ref)

    @pl.when(f == 0)
    def _():
        o_ref[...] = jnp.zeros(o_ref.shape, F32)

    wu = wu_ref[...]
    wd = wd_ref[...]
    for r in range(0, xn_ref.shape[0], ROW_SLAB):
        h = jnp.dot(xn_ref[r:r + ROW_SLAB, :], wu, preferred_element_type=F32)
        h = jnp.square(jnp.maximum(h, 0.0)).astype(BF16)
        o_ref[r:r + ROW_SLAB, :] += jnp.dot(h, wd, preferred_element_type=F32)

    @pl.when(f == pl.num_programs(2) - 1)
    def _():
        gate = mod_ref[5:6, :]
        fw = fw_ref[...]
        rows = 128

        def body(r, carry):
            sl = pl.ds(pl.multiple_of(r * rows, rows), rows)
            y = x_ref[sl, :] + gate * o_ref[sl, :]
            if final_norm:
                y = _rms(y) * fw
            o_ref[sl, :] = y
            return carry

        lax.fori_loop(0, x_ref.shape[0] // rows, body, 0)


def _mlp(x3, mod, mod_base, nw, w_up, w_down, final_w, final_norm):
    nb, rows, d = x3.shape
    dff = w_up.shape[1]
    tm = min(1024, rows)
    tf = 512
    return pl.pallas_call(
        functools.partial(_mlp_kernel, final_norm=final_norm),
        out_shape=jax.ShapeDtypeStruct((nb, rows, d), F32),
        grid=(nb, rows // tm, dff // tf),
        in_specs=[pl.BlockSpec((None, tm, d), lambda b, i, f: (b, i, 0)),
                  pl.BlockSpec((None, N_MOD, d), lambda b, i, f: (mod_base + b, 0, 0)),
                  pl.BlockSpec((1, d), lambda b, i, f: (0, 0)),
                  pl.BlockSpec((d, tf), lambda b, i, f: (0, f)),
                  pl.BlockSpec((tf, d), lambda b, i, f: (f, 0)),
                  pl.BlockSpec((1, d), lambda b, i, f: (0, 0))],
        out_specs=pl.BlockSpec((None, tm, d), lambda b, i, f: (b, i, 0)),
        scratch_shapes=[pltpu.VMEM((tm, d), BF16)],
        compiler_params=_cparams(3),
        name="mlp",
    )(x3, mod, nw, w_up, w_down, final_w)


def _gate_layouts(w_in_l, a_log_l, dt_bias_l, lin_w, n_heads):
    d = w_in_l.shape[0]
    decay_start = 4 * lin_w
    ab = w_in_l[:, decay_start:decay_start + 2 * N_DIR * n_heads]
    ab = ab.reshape(d, 2 * N_DIR, n_heads)
    ab = jnp.transpose(ab, (0, 2, 1))
    ab = jnp.pad(ab, ((0, 0), (0, 0), (0, GATE_SLOTS - 2 * N_DIR)))
    w_ab = ab.reshape(d, n_heads * GATE_SLOTS)
    w_abt = w_ab.T.astype(BF16)
    w_ab = jnp.pad(w_ab, ((0, 0), (0, 128 - n_heads * GATE_SLOTS))).astype(BF16)

    def per_head(p):
        return jnp.pad(p.T.astype(F32), ((0, 0), (0, GATE_SLOTS - N_DIR)))

    hg = HEADS_PER_STEP
    n_groups = n_heads // hg
    alog = per_head(a_log_l)
    dtb = per_head(dt_bias_l)
    alog_c = jnp.pad(alog.reshape(n_groups, 1, hg * GATE_SLOTS), ((0, 0), (0, 0), (0, 128 - hg * GATE_SLOTS)))
    dtb_c = jnp.pad(dtb.reshape(n_groups, 1, hg * GATE_SLOTS), ((0, 0), (0, 0), (0, 128 - hg * GATE_SLOTS)))
    alog_t = jnp.broadcast_to(alog.reshape(n_heads * GATE_SLOTS, 1), (n_heads * GATE_SLOTS, 128))
    dtb_t = jnp.broadcast_to(dtb.reshape(n_heads * GATE_SLOTS, 1), (n_heads * GATE_SLOTS, 128))
    return w_ab, w_abt, alog_c, dtb_c, alog_t, dtb_t


def kernel(x, c, ctx, c_ctx, w_ada, b_ada, norm1_w, norm2_w, w_in, conv_w, a_log, dt_bias, onorm_w,
           pool_w, pool_scale, w_out, w_up, w_down, final_norm_w):
    nb, seq_x, d = x.shape
    seq_h = ctx.shape[1]
    depth = w_ada.shape[0]
    n_heads = a_log.shape[2]
    lin_w = n_heads * HEAD_DIM
    pool_width = pool_scale.shape[1]
    pool_start = 4 * lin_w + 2 * N_DIR * n_heads
    assert nb + 1 <= MOD_ROWS and seq_x % CHUNK == 0 and seq_h % CHUNK == 0
    assert n_heads % HEADS_PER_STEP == 0 and seq_x % GRID_W == 0

    c_all = jnp.concatenate([c, c_ctx[None, :], jnp.zeros((MOD_ROWS - nb - 1, d), F32)], axis=0)
    mod_all = _ada(c_all, w_ada, b_ada).reshape(depth, MOD_ROWS, N_MOD, d)

    h = ctx
    for l in range(depth):
        ctx_out = l < depth - 1
        mod = mod_all[l]
        w_main = jnp.concatenate([w_in[l, :, :4 * lin_w], w_in[l, :, pool_start:]], axis=1).astype(BF16)
        w_ab, w_abt, alog_c, dtb_c, alog_t, dtb_t = _gate_layouts(w_in[l], a_log[l], dt_bias[l], lin_w, n_heads)
        nw1 = norm1_w[l].reshape(1, d)
        nw2 = norm2_w[l].reshape(1, d)
        w_oa = w_out[l, :lin_w].astype(BF16)
        w_ob = w_out[l, lin_w:].astype(BF16)
        w_u = w_up[l].astype(BF16)
        w_d = w_down[l].astype(BF16)
        pw = pool_w[l].astype(BF16)
        ps = pool_scale[l].reshape(1, pool_width)
        fw = final_norm_w.reshape(1, d)

        pxm, abx, abtx = _inproj(x, mod, 0, nw1, w_main, w_ab, w_abt)
        h3 = h.reshape(1, nb * seq_h, d)
        phm, abh, abth = _inproj(h3, mod, nb, nw1, w_main, w_ab, w_abt)
        phm = phm.reshape(nb, seq_h, -1)
        abh = abh.reshape(nb, seq_h, 128)
        abth = jnp.transpose(abth.reshape(-1, nb, seq_h), (1, 0, 2))

        lin_x, lin_h = _delta(pxm, abx, abtx, phm, abh, abth, conv_w[l], alog_c, dtb_c, alog_t, dtb_t,
                              onorm_w[l].reshape(1, HEAD_DIM), n_heads, ctx_out)
        part0 = (4 * lin_w) // pool_width
        pool_x = _pool(pxm, pw, ps, part0, True)
        x = _outproj(lin_x, pool_x, x, mod, 0, w_oa, w_ob)
        x = _mlp(x, mod, 0, nw2, w_u, w_d, fw, l == depth - 1)
        if ctx_out:
            pool_h = _pool(phm, pw, ps, part0, False)
            h3 = _outproj(lin_h.reshape(1, nb * seq_h, lin_w), pool_h.reshape(1, nb * seq_h, pool_width),
                          h3, mod, nb, w_oa, w_ob)
            h = _mlp(h3, mod, nb, nw2, w_u, w_d, fw, False).reshape(nb, seq_h, d)
    return x
```

```python
import functools
import math

import jax
import jax.numpy as jnp
from jax import lax
from jax.experimental import pallas as pl
from jax.experimental.pallas import tpu as pltpu

F32 = jnp.float32
BF16 = jnp.bfloat16

EPS = 1e-6
N_MOD = 6
HEAD_DIM = 128
N_DIR = 2
CONV_K = 5
GRID_W = 64
POOL_SIZES = (2, 4, 8, 16)
CHUNK = 128
INV_BLOCK = 16
GATE_SLOTS = 8
HEADS_PER_STEP = 2
POOL_BLOCKS = 4
PREP_CHUNKS = 3
ROW_SLAB = 256
MOD_ROWS = 24
VMEM_LIMIT = 56 * 1024 * 1024


def _silu(x):
    return x * (1.0 / (1.0 + jnp.exp(-x)))


def _softplus(x):
    return jnp.maximum(x, 0.0) + jnp.log1p(jnp.exp(-jnp.abs(x)))


def _cparams(n_axes):
    return pltpu.CompilerParams(dimension_semantics=("arbitrary",) * n_axes,
                                vmem_limit_bytes=VMEM_LIMIT)


def _ada_kernel(c_ref, w_ref, b_ref, o_ref):
    sc = _silu(c_ref[...]).astype(BF16)
    o_ref[...] = jnp.dot(sc, w_ref[...].astype(BF16), preferred_element_type=F32) + b_ref[...]


def _ada(c_all, w_ada, b_ada):
    depth, d, n = w_ada.shape
    tn = 1024
    return pl.pallas_call(
        _ada_kernel,
        out_shape=jax.ShapeDtypeStruct((depth, MOD_ROWS, n), F32),
        grid=(depth, n // tn),
        in_specs=[pl.BlockSpec((MOD_ROWS, d), lambda l, j: (0, 0)),
                  pl.BlockSpec((None, d, tn), lambda l, j: (l, 0, j)),
                  pl.BlockSpec((None, 1, tn), lambda l, j: (l, 0, j))],
        out_specs=pl.BlockSpec((None, MOD_ROWS, tn), lambda l, j: (l, 0, j)),
        compiler_params=_cparams(2),
        name="ada_mod",
    )(c_all, w_ada, b_ada.reshape(depth, 1, n))


def _rms(x):
    return x * lax.rsqrt(jnp.mean(x * x, axis=-1, keepdims=True) + EPS)


def _norm_mod_to(x_ref, nw_ref, mod_ref, i_shift, i_scale, out_ref, rows=128):
    nw = nw_ref[...]
    shift = mod_ref[i_shift:i_shift + 1, :]
    scale = mod_ref[i_scale:i_scale + 1, :]

    def body(r, carry):
        sl = pl.ds(pl.multiple_of(r * rows, rows), rows)
        y = _rms(x_ref[sl, :]) * nw
        out_ref[sl, :] = (y * (1.0 + scale) + shift).astype(out_ref.dtype)
        return carry

    lax.fori_loop(0, x_ref.shape[0] // rows, body, 0)


def _inproj_kernel(x_ref, mod_ref, nw_ref, w_ref, wab_ref, wabt_ref, o_ref, ab_ref, abt_ref, xn_ref):
    @pl.when(pl.program_id(2) == 0)
    def _():
        _norm_mod_to(x_ref, nw_ref, mod_ref, 0, 1, xn_ref)
        xn = xn_ref[...]
        ab_ref[...] = jnp.dot(xn, wab_ref[...], preferred_element_type=F32)
        abt_ref[...] = lax.dot_general(wabt_ref[...], xn, (((1,), (1,)), ((), ())),
                                       preferred_element_type=F32)

    w = w_ref[...]
    for r in range(0, xn_ref.shape[0], ROW_SLAB):
        o_ref[r:r + ROW_SLAB, :] = jnp.dot(xn_ref[r:r + ROW_SLAB, :], w, preferred_element_type=F32)


def _inproj(x3, mod, mod_base, nw, w_main, w_ab, w_abt):
    nb, rows, d = x3.shape
    n = w_main.shape[1]
    tm = min(1024, rows)
    tn = 1024
    n_gate_rows = w_abt.shape[0]
    return pl.pallas_call(
        _inproj_kernel,
        out_shape=(jax.ShapeDtypeStruct((nb, rows, n), F32),
                   jax.ShapeDtypeStruct((nb, rows, 128), F32),
                   jax.ShapeDtypeStruct((nb, n_gate_rows, rows), F32)),
        grid=(nb, rows // tm, n // tn),
        in_specs=[pl.BlockSpec((None, tm, d), lambda b, i, j: (b, i, 0)),
                  pl.BlockSpec((None, N_MOD, d), lambda b, i, j: (mod_base + b, 0, 0)),
                  pl.BlockSpec((1, d), lambda b, i, j: (0, 0)),
                  pl.BlockSpec((d, tn), lambda b, i, j: (0, j)),
                  pl.BlockSpec((d, 128), lambda b, i, j: (0, 0)),
                  pl.BlockSpec((n_gate_rows, d), lambda b, i, j: (0, 0))],
        out_specs=(pl.BlockSpec((None, tm, tn), lambda b, i, j: (b, i, j)),
                   pl.BlockSpec((None, tm, 128), lambda b, i, j: (b, i, 0)),
                   pl.BlockSpec((None, n_gate_rows, tm), lambda b, i, j: (b, 0, i))),
        scratch_shapes=[pltpu.VMEM((tm, d), BF16)],
        compiler_params=_cparams(3),
        name="in_proj",
    )(x3, mod, nw, w_main, w_ab, w_abt)


def _delta_kernel(*refs, seq_x, seq_h, hg, ctx_out):
    n_in = 19
    (qx_ref, kx_ref, vx_ref, zx_ref, abx_ref, abtx_ref,
     qh_ref, kh_ref, vh_ref, zh_ref, abh_ref, abth_ref,
     cq_ref, ck_ref, cv_ref, alog_ref, dtb_ref, alogt_ref, dtbt_ref) = refs[:n_in]
    onorm_ref = refs[n_in]
    if ctx_out:
        ox_ref, oh_ref = refs[n_in + 1:n_in + 3]
        scratch = refs[n_in + 3:]
    else:
        ox_ref = refs[n_in + 1]
        oh_ref = None
        scratch = refs[n_in + 2:]
    (pad_s, q_s, k_s, v_s, o_s, gb_s, cum_s, cumt_s, u_s, wq_s, ktt_s, at_s, egl_s, st_s) = scratch

    c = CHUNK
    hd = HEAD_DIM
    width = hg * hd
    seq_t = seq_h + seq_x
    nc_h = seq_h // c
    nc_x = seq_x // c
    nc = nc_h + nc_x
    n_chain = hg * N_DIR

    def conv_part(src_ref, w_ref, dst_ref, row0, n_rows, mode):
        tile = 256 if n_rows % 256 == 0 else 128
        pad_s[0:8, :] = jnp.zeros((8, width), F32)
        pad_s[8:8 + n_rows, :] = src_ref[...]
        pad_s[8 + n_rows:16 + n_rows, :] = jnp.zeros((8, width), F32)
        taps = [w_ref[i:i + 1, :] for i in range(CONV_K)]

        for r in range(n_rows // tile):
            start = r * tile
            acc = None
            for i in range(CONV_K):
                off = start + 8 - CONV_K // 2 + i
                term = pad_s[off:off + tile, :] * taps[i]
                acc = term if acc is None else acc + term
            y = _silu(acc)
            if mode != "v":
                cols = []
                for hl in range(hg):
                    yh = y[:, hl * hd:(hl + 1) * hd]
                    yh = yh * lax.rsqrt(jnp.sum(yh * yh, axis=-1, keepdims=True) + EPS)
                    cols.append(yh)
                y = cols[0] if hg == 1 else jnp.concatenate(cols, axis=1)
                if mode == "q":
                    y = y * (hd ** -0.5)
            dst_ref[row0 + start:row0 + start + tile, :] = y

    conv_part(qh_ref, cq_ref, q_s, 0, seq_h, "q")
    conv_part(qx_ref, cq_ref, q_s, seq_h, seq_x, "q")
    conv_part(kh_ref, ck_ref, k_s, 0, seq_h, "k")
    conv_part(kx_ref, ck_ref, k_s, seq_h, seq_x, "k")
    conv_part(vh_ref, cv_ref, v_s, 0, seq_h, "v")
    conv_part(vx_ref, cv_ref, v_s, seq_h, seq_x, "v")

    lane = lax.broadcasted_iota(jnp.int32, (c, 128), 1)
    slot_l = lane % GATE_SLOTS
    shift = (128 - pl.program_id(1) * (hg * GATE_SLOTS)) % 128
    alog_r = alog_ref[...]
    dtb_r = dtb_ref[...]

    def gates_cols(ab_ref, row0, n_rows):
        def body(r, carry):
            sl = pl.ds(pl.multiple_of(r * c, c), c)
            a = pltpu.roll(ab_ref[sl, :], shift, 1)
            g = -jnp.exp(alog_r) * _softplus(a + dtb_r)
            beta = 1.0 / (1.0 + jnp.exp(-a))
            gb_s[pl.ds(pl.multiple_of(row0 + r * c, c), c), :] = jnp.where(slot_l < N_DIR, g, beta)
            return carry

        lax.fori_loop(0, n_rows // c, body, 0, unroll=2)

    gates_cols(abh_ref, 0, seq_h)
    gates_cols(abx_ref, seq_h, seq_x)

    ri = lax.broadcasted_iota(jnp.int32, (c, c), 0)
    ci = lax.broadcasted_iota(jnp.int32, (c, c), 1)
    lower_incl = (ci <= ri)
    upper_incl = (ci >= ri)
    m_le = jnp.where(lower_incl, 1.0, 0.0).astype(BF16)
    m_ge = jnp.where(upper_incl, 1.0, 0.0).astype(BF16)

    def split3(x):
        hi = x.astype(BF16)
        r1 = x - hi.astype(F32)
        mid = r1.astype(BF16)
        lo = (r1 - mid.astype(F32)).astype(BF16)
        return hi, mid, lo

    alogt = alogt_ref[...]
    dtbt = dtbt_ref[...]
    n_gt = hg * GATE_SLOTS
    row_slot = lax.broadcasted_iota(jnp.int32, (n_gt, c), 0) % GATE_SLOTS

    def cums(abt_ref, chunk0, n_chunks):
        def body(r, carry):
            src = pl.ds(pl.multiple_of(r * c, c), c)
            dst = pl.ds(pl.multiple_of((chunk0 + r) * c, c), c)
            pieces = jnp.concatenate(split3(gb_s[dst, :]), axis=1)
            pre = jnp.dot(m_le, pieces, preferred_element_type=F32)
            suf = jnp.dot(m_ge, pieces, preferred_element_type=F32)
            pre = pre[:, 0:128] + pre[:, 128:256] + pre[:, 256:384]
            suf = suf[:, 0:128] + suf[:, 128:256] + suf[:, 256:384]
            cum_s[dst, :] = jnp.where(slot_l == 0, pre, suf)
            gt = -jnp.exp(alogt) * _softplus(abt_ref[:, src] + dtbt)
            pieces_t = jnp.concatenate(split3(gt), axis=0)
            pre_t = jnp.dot(pieces_t, m_ge, preferred_element_type=F32)
            suf_t = jnp.dot(pieces_t, m_le, preferred_element_type=F32)
            pre_t = pre_t[0:n_gt] + pre_t[n_gt:2 * n_gt] + pre_t[2 * n_gt:3 * n_gt]
            suf_t = suf_t[0:n_gt] + suf_t[n_gt:2 * n_gt] + suf_t[2 * n_gt:3 * n_gt]
            cumt_s[:, dst] = jnp.where(row_slot == 0, pre_t, suf_t)
            return carry

        lax.fori_loop(0, n_chunks, body, 0, unroll=2)

    cums(abth_ref, 0, nc_h)
    cums(abtx_ref, nc_h, nc_x)

    eye = jnp.where(ri == ci, 1.0, 0.0).astype(F32)
    masks = ((lower_incl, ci < ri), (upper_incl, ci > ri))

    def bdot(a, b):
        return jnp.dot(a.astype(BF16), b.astype(BF16), preferred_element_type=F32)

    same_blk = (ri // INV_BLOCK) == (ci // INV_BLOCK)

    def tri_inverse(nms, lowers):
        nds = [jnp.where(same_blk, nm, 0.0) for nm in nms]
        ms = [eye + nd for nd in nds]
        ps = [bdot(nd, nd) for nd in nds]
        for _ in range(int(math.log2(INV_BLOCK)) - 2):
            rs = [bdot(p, jnp.concatenate([m, p], axis=1)) for m, p in zip(ms, ps)]
            ms = [m + r_[:, 0:c] for m, r_ in zip(ms, rs)]
            ps = [r_[:, c:2 * c] for r_ in rs]
        rs = [bdot(p, m) for m, p in zip(ms, ps)]
        ms = [m + r_ for m, r_ in zip(ms, rs)]
        s = INV_BLOCK
        while s < c:
            starts = list(range(0, c, s))
            pair = ((ri // (2 * s)) == (ci // (2 * s))) & ((ri // s) != (ci // s))
            zero = jnp.zeros((s, c), BF16)
            hits = [[(a // s) % 2 == (1 if lower else 0) for a in starts] for lower in lowers]

            def take(xm, hit):
                return jnp.concatenate([xm[a:a + s] for a, t in zip(starts, hit) if t], axis=0)

            m_sels = [take(m, hit) for m, hit in zip(ms, hits)]
            xs = [bdot(take(jnp.where(pair, nm, 0.0), hit), m).astype(BF16)
                  for nm, m, hit in zip(nms, ms, hits)]
            ys = []
            for x, m_sel, hit in zip(xs, m_sels, hits):
                slabs, k_ = [], 0
                for t in hit:
                    slabs.append(x[k_ * s:(k_ + 1) * s] if t else zero)
                    k_ += 1 if t else 0
                ys.append(jnp.dot(m_sel.astype(BF16), jnp.concatenate(slabs, axis=0),
                                  preferred_element_type=F32))
            new_ms = []
            for m, m_sel, y, hit in zip(ms, m_sels, ys, hits):
                new_sel = m_sel + y
                slabs, k_ = [], 0
                for a, t in zip(starts, hit):
                    slabs.append(new_sel[k_ * s:(k_ + 1) * s] if t else m[a:a + s])
                    k_ += 1 if t else 0
                new_ms.append(jnp.concatenate(slabs, axis=0))
            ms = new_ms
            s *= 2
        return ms

    n_prep = PREP_CHUNKS if nc % PREP_CHUNKS == 0 else 1

    def prep(grp, carry):
        chs = [grp * n_prep + j for j in range(n_prep)]
        rows = [pl.ds(pl.multiple_of(ch * c, c), c) for ch in chs]
        gbc = [gb_s[r, :] for r in rows]
        cumc = [cum_s[r, :] for r in rows]
        cumtc = [cumt_s[:, r] for r in rows]
        qs, ks, vs, aas = {}, {}, {}, {}
        for j in range(n_prep):
            for hl in range(hg):
                cols = slice(hl * hd, (hl + 1) * hd)
                qs[j, hl] = q_s[rows[j], cols]
                ks[j, hl] = k_s[rows[j], cols]
                vs[j, hl] = v_s[rows[j], cols]
                qk = jnp.concatenate([qs[j, hl], ks[j, hl]], axis=0).astype(BF16)
                aas[j, hl] = lax.dot_general(qk, ks[j, hl].astype(BF16), (((1,), (1,)), ((), ())),
                                             preferred_element_type=F32)
        probs = [(j, hl, d) for j in range(n_prep) for hl in range(hg) for d in range(N_DIR)]
        nms, betas, cums, decays = [], [], [], []
        for j, hl, d in probs:
            incl, strict = masks[d]
            base = hl * GATE_SLOTS
            beta_b = jnp.broadcast_to(gbc[j][:, base + N_DIR + d:base + N_DIR + d + 1], (c, hd))
            cum_b = jnp.broadcast_to(cumc[j][:, base + d:base + d + 1], (c, hd))
            cum_r = cumtc[j][base + d:base + d + 1, :]
            decay = jnp.where(incl, jnp.exp(jnp.where(incl, cum_b - cum_r, 0.0)), 0.0)
            nms.append(jnp.where(strict, -(beta_b * aas[j, hl][c:2 * c] * decay), 0.0))
            betas.append(beta_b)
            cums.append(cum_b)
            decays.append(decay)
        ms = tri_inverse(nms, [d == 0 for _, _, d in probs])
        e_cs = [jnp.exp(cum_b) for cum_b in cums]
        uws = [bdot(m, jnp.concatenate([vs[j, hl] * beta_b, ks[j, hl] * (beta_b * e_c)], axis=1))
               for (j, hl, d), m, beta_b, e_c in zip(probs, ms, betas, e_cs)]
        for i, (j, hl, d) in enumerate(probs):
            cum_b, e_c, uw = cums[i], e_cs[i], uws[i]
            gl_b = cum_b[c - 1:c, :] if d == 0 else cum_b[0:1, :]
            kt = ks[j, hl] * jnp.exp(gl_b - cum_b)
            slot = (hl * N_DIR + d) * nc + chs[j]
            u_s[pl.ds(pl.multiple_of(slot * c, c), c), :] = uw[:, 0:hd]
            wq_s[pl.ds(pl.multiple_of(slot * 2 * c, c), c), :] = uw[:, hd:2 * hd].astype(BF16)
            wq_s[pl.ds(pl.multiple_of(slot * 2 * c + c, c), c), :] = (qs[j, hl] * e_c).astype(BF16)
            ktt_s[pl.ds(pl.multiple_of(slot * hd, hd), hd), :] = kt.T.astype(BF16)
            at_s[pl.ds(pl.multiple_of(slot * c, c), c), :] = (aas[j, hl][0:c] * decays[i]).astype(BF16)
            egl_s[pl.ds(pl.multiple_of(slot * 8, 8), 8), :] = jnp.broadcast_to(jnp.exp(gl_b), (8, hd))
        return carry

    lax.fori_loop(0, nc // n_prep, prep, 0)

    st_s[...] = jnp.zeros(st_s.shape, F32)
    o_s[...] = jnp.zeros(o_s.shape, F32)

    def scan_step(ch_f, ch_b, with_q):
        chains = [(hl, d) for hl in range(hg) for d in range(N_DIR)]
        chs = [ch_f if d == 0 else ch_b for _, d in chains]
        slots = [i * nc + ch for i, ch in enumerate(chs)]
        s_olds = [st_s[i * hd:(i + 1) * hd, :] for i in range(len(chains))]
        n_lhs = 2 * c if with_q else c
        rs = [jnp.dot(wq_s[pl.ds(pl.multiple_of(slot * 2 * c, 2 * c), n_lhs), :], s_old.astype(BF16),
                      preferred_element_type=F32) for slot, s_old in zip(slots, s_olds)]
        v_bs = [(u_s[pl.ds(pl.multiple_of(slot * c, c), c), :] - r_[0:c]).astype(BF16)
                for slot, r_ in zip(slots, rs)]
        kvs = [jnp.dot(ktt_s[pl.ds(pl.multiple_of(slot * hd, hd), hd), :], v_b, preferred_element_type=F32)
               for slot, v_b in zip(slots, v_bs)]
        if with_q:
            avs = [jnp.dot(at_s[pl.ds(pl.multiple_of(slot * c, c), c), :], v_b, preferred_element_type=F32)
                   for slot, v_b in zip(slots, v_bs)]
        for i, (hl, d) in enumerate(chains):
            egl = egl_s[pl.ds(pl.multiple_of(slots[i] * 8, 8), 1), :]
            st_s[i * hd:(i + 1) * hd, :] = s_olds[i] * egl + kvs[i]
            if with_q:
                orow = pl.ds(pl.multiple_of(chs[i] * c, c), c)
                ocol = slice(hl * hd, (hl + 1) * hd)
                o_s[orow, ocol] = o_s[orow, ocol] + (rs[i][c:2 * c] + avs[i])

    def scan_h(it, carry):
        scan_step(it, nc_h - 1 - it, ctx_out)
        return carry

    def scan_x(it, carry):
        scan_step(nc_h + it, nc - 1 - it, True)
        return carry

    lax.fori_loop(0, nc_h, scan_h, 0)
    lax.fori_loop(0, nc_x, scan_x, 0)

    onw = onorm_ref[...]

    def finish(z_ref, out_ref, row0, n_rows):
        tile = 256 if n_rows % 256 == 0 else 128

        def body(r, carry):
            start = pl.multiple_of(r * tile, tile)
            src = pl.ds(pl.multiple_of(row0 + start, CHUNK), tile)
            dst = pl.ds(start, tile)
            outs = []
            for hl in range(hg):
                cols = slice(hl * hd, (hl + 1) * hd)
                o = o_s[src, cols]
                outs.append(_rms(o) * onw * _silu(z_ref[dst, cols]))
            y = outs[0] if hg == 1 else jnp.concatenate(outs, axis=1)
            out_ref[dst, :] = y.astype(out_ref.dtype)
            return carry

        lax.fori_loop(0, n_rows // tile, body, 0)

    finish(zx_ref, ox_ref, seq_h, seq_x)
    if ctx_out:
        finish(zh_ref, oh_ref, 0, seq_h)


def _delta(pxm, abx, abtx, phm, abh, abth, conv_w, alog_c, dtb_c, alog_t, dtb_t, onorm, n_heads, ctx_out):
    nb, seq_x, _ = pxm.shape
    seq_h = phm.shape[1]
    hg = HEADS_PER_STEP
    hd = HEAD_DIM
    width = hg * hd
    n_groups = n_heads // hg
    lin_w = n_heads * hd
    seq_t = seq_x + seq_h
    nc = seq_t // CHUNK
    n_chain = hg * N_DIR
    n_gt = hg * GATE_SLOTS

    def main_spec(rows, part):
        return pl.BlockSpec((None, rows, width), lambda b, g: (b, 0, part * n_groups + g))

    in_specs = (
        [main_spec(seq_x, p) for p in range(4)]
        + [pl.BlockSpec((None, seq_x, 128), lambda b, g: (b, 0, 0)),
           pl.BlockSpec((None, n_gt, seq_x), lambda b, g: (b, g, 0))]
        + [main_spec(seq_h, p) for p in range(4)]
        + [pl.BlockSpec((None, seq_h, 128), lambda b, g: (b, 0, 0)),
           pl.BlockSpec((None, n_gt, seq_h), lambda b, g: (b, g, 0))]
        + [pl.BlockSpec((CONV_K, width), lambda b, g, p=p: (0, p * n_groups + g)) for p in range(3)]
        + [pl.BlockSpec((None, 1, 128), lambda b, g: (g, 0, 0)),
           pl.BlockSpec((None, 1, 128), lambda b, g: (g, 0, 0)),
           pl.BlockSpec((n_gt, 128), lambda b, g: (g, 0)),
           pl.BlockSpec((n_gt, 128), lambda b, g: (g, 0)),
           pl.BlockSpec((1, hd), lambda b, g: (0, 0))]
    )
    out_shape = [jax.ShapeDtypeStruct((nb, seq_x, lin_w), BF16)]
    out_specs = [pl.BlockSpec((None, seq_x, width), lambda b, g: (b, 0, g))]
    if ctx_out:
        out_shape.append(jax.ShapeDtypeStruct((nb, seq_h, lin_w), BF16))
        out_specs.append(pl.BlockSpec((None, seq_h, width), lambda b, g: (b, 0, g)))
    scratch = [
        pltpu.VMEM((seq_x + 16, width), F32),
        pltpu.VMEM((seq_t, width), F32),
        pltpu.VMEM((seq_t, width), F32),
        pltpu.VMEM((seq_t, width), F32),
        pltpu.VMEM((seq_t, width), F32),
        pltpu.VMEM((seq_t, 128), F32),
        pltpu.VMEM((seq_t, 128), F32),
        pltpu.VMEM((n_gt, seq_t), F32),
        pltpu.VMEM((n_chain * nc * CHUNK, hd), F32),
        pltpu.VMEM((n_chain * nc * 2 * CHUNK, hd), BF16),
        pltpu.VMEM((n_chain * nc * hd, CHUNK), BF16),
        pltpu.VMEM((n_chain * nc * CHUNK, CHUNK), BF16),
        pltpu.VMEM((n_chain * nc * 8, hd), F32),
        pltpu.VMEM((n_chain * hd, hd), F32),
    ]
    args = ([pxm] * 4 + [abx, abtx] + [phm] * 4 + [abh, abth] + [conv_w] * 3
            + [alog_c, dtb_c, alog_t, dtb_t, onorm])
    res = pl.pallas_call(
        functools.partial(_delta_kernel, seq_x=seq_x, seq_h=seq_h, hg=hg, ctx_out=ctx_out),
        out_shape=tuple(out_shape),
        grid=(nb, n_groups),
        in_specs=in_specs,
        out_specs=tuple(out_specs),
        scratch_shapes=scratch,
        compiler_params=_cparams(2),
        name="gated_delta",
    )(*args)
    return res if ctx_out else (res[0], None)


def _pool_kernel(p_ref, w_ref, sc_ref, o_ref, y_s, *, on_grid, n_groups, gw):
    n = p_ref.shape[0]
    blk = 2 * GRID_W if on_grid else n
    period = GRID_W if on_grid else n
    ri = lax.broadcasted_iota(jnp.int32, (blk, blk), 0)
    ci = lax.broadcasted_iota(jnp.int32, (blk, blk), 1)
    pos = lax.broadcasted_iota(jnp.int32, (blk, gw), 0) % period

    for g in range(n_groups):
        size = POOL_SIZES[g]
        half = size // 2
        cols = slice(g * gw, (g + 1) * gw)
        if on_grid:
            n_rows = n // GRID_W
            for r in range(n_rows):
                lo = max(r - half, 0)
                hi = min(r + size - 1 - half, n_rows - 1)
                acc = p_ref[lo * GRID_W:(lo + 1) * GRID_W, cols]
                for rr in range(lo + 1, hi + 1):
                    acc = acc + p_ref[rr * GRID_W:(rr + 1) * GRID_W, cols]
                y_s[r * GRID_W:(r + 1) * GRID_W, :] = acc / float(hi - lo + 1)
            src = y_s
            src_cols = slice(0, gw)
        else:
            src = p_ref
            src_cols = cols
        band = ((ci >= ri - half) & (ci <= ri + size - 1 - half) & (ci // period == ri // period))
        band = jnp.where(band, 1.0, 0.0).astype(BF16)
        lo_c = jnp.maximum(pos - half, 0)
        hi_c = jnp.minimum(pos + size - 1 - half, period - 1)
        cnt = (hi_c - lo_c + 1).astype(F32)
        w_g = w_ref[g]
        scale = sc_ref[:, cols]

        n_blk = n // blk
        group = POOL_BLOCKS if n_blk % POOL_BLOCKS == 0 else 1

        def body(t, carry):
            rows = [pl.ds(pl.multiple_of((t * group + j) * blk, blk), blk) for j in range(group)]
            pieces = []
            for r in rows:
                y = src[r, src_cols]
                hi_p = y.astype(BF16)
                r1 = y - hi_p.astype(F32)
                mid_p = r1.astype(BF16)
                lo_p = (r1 - mid_p.astype(F32)).astype(BF16)
                pieces.append(jnp.concatenate([hi_p, mid_p, lo_p], axis=1))
            tots = [jnp.dot(band, pc, preferred_element_type=F32) for pc in pieces]
            dlts = []
            for r, tot in zip(rows, tots):
                mean = (tot[:, 0:gw] + tot[:, gw:2 * gw] + tot[:, 2 * gw:3 * gw]) / cnt
                dlts.append((mean - p_ref[r, cols]).astype(BF16))
            outs = [jnp.dot(dlt, w_g, preferred_element_type=F32) for dlt in dlts]
            for r, out in zip(rows, outs):
                o_ref[r, cols] = (out * scale).astype(o_ref.dtype)
            return carry

        lax.fori_loop(0, n_blk // group, body, 0)


def _pool(pm, pool_w, pool_scale, part0, on_grid):
    nb, rows, _ = pm.shape
    n_groups, gw, _ = pool_w.shape
    pw = n_groups * gw
    return pl.pallas_call(
        functools.partial(_pool_kernel, on_grid=on_grid, n_groups=n_groups, gw=gw),
        out_shape=jax.ShapeDtypeStruct((nb, rows, pw), BF16),
        grid=(nb,),
        in_specs=[pl.BlockSpec((None, rows, pw), lambda b: (b, 0, part0)),
                  pl.BlockSpec((n_groups, gw, gw), lambda b: (0, 0, 0)),
                  pl.BlockSpec((1, pw), lambda b: (0, 0))],
        out_specs=pl.BlockSpec((None, rows, pw), lambda b: (b, 0, 0)),
        scratch_shapes=[pltpu.VMEM((rows, gw), F32)],
        compiler_params=_cparams(1),
        name="pool_mixer",
    )(pm, pool_w, pool_scale)


def _outproj_kernel(lin_ref, pool_ref, x_ref, mod_ref, wa_ref, wb_ref, o_ref):
    y = (jnp.dot(lin_ref[...], wa_ref[...], preferred_element_type=F32)
         + jnp.dot(pool_ref[...], wb_ref[...], preferred_element_type=F32))
    o_ref[...] = x_ref[...] + mod_ref[2:3, :] * y


def _outproj(lin, pool, x3, mod, mod_base, w_a, w_b):
    nb, rows, d = x3.shape
    tm = min(512, rows)
    ka = w_a.shape[0]
    kb = w_b.shape[0]
    return pl.pallas_call(
        _outproj_kernel,
        out_shape=jax.ShapeDtypeStruct((nb, rows, d), F32),
        grid=(nb, rows // tm),
        in_specs=[pl.BlockSpec((None, tm, ka), lambda b, i: (b, i, 0)),
                  pl.BlockSpec((None, tm, kb), lambda b, i: (b, i, 0)),
                  pl.BlockSpec((None, tm, d), lambda b, i: (b, i, 0)),
                  pl.BlockSpec((None, N_MOD, d), lambda b, i: (mod_base + b, 0, 0)),
                  pl.BlockSpec((ka, d), lambda b, i: (0, 0)),
                  pl.BlockSpec((kb, d), lambda b, i: (0, 0))],
        out_specs=pl.BlockSpec((None, tm, d), lambda b, i: (b, i, 0)),
        compiler_params=_cparams(2),
        name="out_proj",
    )(lin, pool, x3, mod, w_a, w_b)


def _mlp_kernel(x_ref, mod_ref, nw_ref, wu_ref, wd_ref, fw_ref, o_ref, xn_ref, *, final_norm):
    f = pl.program_id(2)

    @pl.when(f == 0)
    def _():
        _norm_mod_to(x_ref, nw_ref, mod_ref, 3, 4, xn_ref)

    @pl.when(f == 0)
    def _():
        o_ref[...] = jnp.zeros(o_ref.shape, F32)

    wu = wu_ref[...]
    wd = wd_ref[...]
    for r in range(0, xn_ref.shape[0], ROW_SLAB):
        h = jnp.dot(xn_ref[r:r + ROW_SLAB, :], wu, preferred_element_type=F32)
        h = jnp.square(jnp.maximum(h, 0.0)).astype(BF16)
        o_ref[r:r + ROW_SLAB, :] += jnp.dot(h, wd, preferred_element_type=F32)

    @pl.when(f == pl.num_programs(2) - 1)
    def _():
        gate = mod_ref[5:6, :]
        fw = fw_ref[...]
        rows = 128

        def body(r, carry):
            sl = pl.ds(pl.multiple_of(r * rows, rows), rows)
            y = x_ref[sl, :] + gate * o_ref[sl, :]
            if final_norm:
                y = _rms(y) * fw
            o_ref[sl, :] = y
            return carry

        lax.fori_loop(0, x_ref.shape[0] // rows, body, 0)


def _mlp(x3, mod, mod_base, nw, w_up, w_down, final_w, final_norm):
    nb, rows, d = x3.shape
    dff = w_up.shape[1]
    tm = min(1024, rows)
    tf = 512
    return pl.pallas_call(
        functools.partial(_mlp_kernel, final_norm=final_norm),
        out_shape=jax.ShapeDtypeStruct((nb, rows, d), F32),
        grid=(nb, rows // tm, dff // tf),
        in_specs=[pl.BlockSpec((None, tm, d), lambda b, i, f: (b, i, 0)),
                  pl.BlockSpec((None, N_MOD, d), lambda b, i, f: (mod_base + b, 0, 0)),
                  pl.BlockSpec((1, d), lambda b, i, f: (0, 0)),
                  pl.BlockSpec((d, tf), lambda b, i, f: (0, f)),
                  pl.BlockSpec((tf, d), lambda b, i, f: (f, 0)),
                  pl.BlockSpec((1, d), lambda b, i, f: (0, 0))],
        out_specs=pl.BlockSpec((None, tm, d), lambda b, i, f: (b, i, 0)),
        scratch_shapes=[pltpu.VMEM((tm, d), BF16)],
        compiler_params=_cparams(3),
        name="mlp",
    )(x3, mod, nw, w_up, w_down, final_w)


def _gate_layouts(w_in_l, a_log_l, dt_bias_l, lin_w, n_heads):
    d = w_in_l.shape[0]
    decay_start = 4 * lin_w
    ab = w_in_l[:, decay_start:decay_start + 2 * N_DIR * n_heads]
    ab = ab.reshape(d, 2 * N_DIR, n_heads)
    ab = jnp.transpose(ab, (0, 2, 1))
    ab = jnp.pad(ab, ((0, 0), (0, 0), (0, GATE_SLOTS - 2 * N_DIR)))
    w_ab = ab.reshape(d, n_heads * GATE_SLOTS)
    w_abt = w_ab.T.astype(BF16)
    w_ab = jnp.pad(w_ab, ((0, 0), (0, 128 - n_heads * GATE_SLOTS))).astype(BF16)

    def per_head(p):
        return jnp.pad(p.T.astype(F32), ((0, 0), (0, GATE_SLOTS - N_DIR)))

    hg = HEADS_PER_STEP
    n_groups = n_heads // hg
    alog = per_head(a_log_l)
    dtb = per_head(dt_bias_l)
    alog_c = jnp.pad(alog.reshape(n_groups, 1, hg * GATE_SLOTS), ((0, 0), (0, 0), (0, 128 - hg * GATE_SLOTS)))
    dtb_c = jnp.pad(dtb.reshape(n_groups, 1, hg * GATE_SLOTS), ((0, 0), (0, 0), (0, 128 - hg * GATE_SLOTS)))
    alog_t = jnp.broadcast_to(alog.reshape(n_heads * GATE_SLOTS, 1), (n_heads * GATE_SLOTS, 128))
    dtb_t = jnp.broadcast_to(dtb.reshape(n_heads * GATE_SLOTS, 1), (n_heads * GATE_SLOTS, 128))
    return w_ab, w_abt, alog_c, dtb_c, alog_t, dtb_t


def kernel(x, c, ctx, c_ctx, w_ada, b_ada, norm1_w, norm2_w, w_in, conv_w, a_log, dt_bias, onorm_w,
           pool_w, pool_scale, w_out, w_up, w_down, final_norm_w):
    nb, seq_x, d = x.shape
    seq_h = ctx.shape[1]
    depth = w_ada.shape[0]
    n_heads = a_log.shape[2]
    lin_w = n_heads * HEAD_DIM
    pool_width = pool_scale.shape[1]
    pool_start = 4 * lin_w + 2 * N_DIR * n_heads
    assert nb + 1 <= MOD_ROWS and seq_x % CHUNK == 0 and seq_h % CHUNK == 0
    assert n_heads % HEADS_PER_STEP == 0 and seq_x % GRID_W == 0

    c_all = jnp.concatenate([c, c_ctx[None, :], jnp.zeros((MOD_ROWS - nb - 1, d), F32)], axis=0)
    mod_all = _ada(c_all, w_ada, b_ada).reshape(depth, MOD_ROWS, N_MOD, d)

    h = ctx
    for l in range(depth):
        ctx_out = l < depth - 1
        mod = mod_all[l]
        w_main = jnp.concatenate([w_in[l, :, :4 * lin_w], w_in[l, :, pool_start:]], axis=1).astype(BF16)
        w_ab, w_abt, alog_c, dtb_c, alog_t, dtb_t = _gate_layouts(w_in[l], a_log[l], dt_bias[l], lin_w, n_heads)
        nw1 = norm1_w[l].reshape(1, d)
        nw2 = norm2_w[l].reshape(1, d)
        w_oa = w_out[l, :lin_w].astype(BF16)
        w_ob = w_out[l, lin_w:].astype(BF16)
        w_u = w_up[l].astype(BF16)
        w_d = w_down[l].astype(BF16)
        pw = pool_w[l].astype(BF16)
        ps = pool_scale[l].reshape(1, pool_width)
        fw = final_norm_w.reshape(1, d)

        pxm, abx, abtx = _inproj(x, mod, 0, nw1, w_main, w_ab, w_abt)
        h3 = h.reshape(1, nb * seq_h, d)
        phm, abh, abth = _inproj(h3, mod, nb, nw1, w_main, w_ab, w_abt)
        phm = phm.reshape(nb, seq_h, -1)
        abh = abh.reshape(nb, seq_h, 128)
        abth = jnp.transpose(abth.reshape(-1, nb, seq_h), (1, 0, 2))

        lin_x, lin_h = _delta(pxm, abx, abtx, phm, abh, abth, conv_w[l], alog_c, dtb_c, alog_t, dtb_t,
                              onorm_w[l].reshape(1, HEAD_DIM), n_heads, ctx_out)
        part0 = (4 * lin_w) // pool_width
        pool_x = _pool(pxm, pw, ps, part0, True)
        x = _outproj(lin_x, pool_x, x, mod, 0, w_oa, w_ob)
        x = _mlp(x, mod, 0, nw2, w_u, w_d, fw, l == depth - 1)
        if ctx_out:
            pool_h = _pool(phm, pw, ps, part0, False)
            h3 = _outproj(lin_h.reshape(1, nb * seq_h, lin_w), pool_h.reshape(1, nb * seq_h, pool_width),
                          h3, mod, nb, w_oa, w_ob)
            h = _mlp(h3, mod, nb, nw2, w_u, w_d, fw, False).reshape(nb, seq_h, d)
    return x
```

```python
import functools
import math

import jax
import jax.numpy as jnp
from jax import lax
from jax.experimental import pallas as pl
from jax.experimental.pallas import tpu as pltpu

F32 = jnp.float32
BF16 = jnp.bfloat16

EPS = 1e-6
N_MOD = 6
HEAD_DIM = 128
N_DIR = 2
CONV_K = 5
GRID_W = 64
POOL_SIZES = (2, 4, 8, 16)
CHUNK = 128
INV_BLOCK = 16
GATE_SLOTS = 8
HEADS_PER_STEP = 2
POOL_BLOCKS = 4
PREP_CHUNKS = 3
ROW_SLAB = 256
MOD_ROWS = 24
VMEM_LIMIT = 56 * 1024 * 1024


def _silu(x):
    return x * (1.0 / (1.0 + jnp.exp(-x)))


def _softplus(x):
    return jnp.maximum(x, 0.0) + jnp.log1p(jnp.exp(-jnp.abs(x)))


def _cparams(n_axes):
    return pltpu.CompilerParams(dimension_semantics=("arbitrary",) * n_axes,
                                vmem_limit_bytes=VMEM_LIMIT)


def _ada_kernel(c_ref, w_ref, b_ref, o_ref):
    sc = _silu(c_ref[...]).astype(BF16)
    o_ref[...] = jnp.dot(sc, w_ref[...].astype(BF16), preferred_element_type=F32) + b_ref[...]


def _ada(c_all, w_ada, b_ada):
    depth, d, n = w_ada.shape
    tn = 1024
    return pl.pallas_call(
        _ada_kernel,
        out_shape=jax.ShapeDtypeStruct((depth, MOD_ROWS, n), F32),
        grid=(depth, n // tn),
        in_specs=[pl.BlockSpec((MOD_ROWS, d), lambda l, j: (0, 0)),
                  pl.BlockSpec((None, d, tn), lambda l, j: (l, 0, j)),
                  pl.BlockSpec((None, 1, tn), lambda l, j: (l, 0, j))],
        out_specs=pl.BlockSpec((None, MOD_ROWS, tn), lambda l, j: (l, 0, j)),
        compiler_params=_cparams(2),
        name="ada_mod",
    )(c_all, w_ada, b_ada.reshape(depth, 1, n))


def _rms(x):
    return x * lax.rsqrt(jnp.mean(x * x, axis=-1, keepdims=True) + EPS)


def _norm_mod_to(x_ref, nw_ref, mod_ref, i_shift, i_scale, out_ref, rows=128):
    nw = nw_ref[...]
    shift = mod_ref[i_shift:i_shift + 1, :]
    scale = mod_ref[i_scale:i_scale + 1, :]

    def body(r, carry):
        sl = pl.ds(pl.multiple_of(r * rows, rows), rows)
        y = _rms(x_ref[sl, :]) * nw
        out_ref[sl, :] = (y * (1.0 + scale) + shift).astype(out_ref.dtype)
        return carry

    lax.fori_loop(0, x_ref.shape[0] // rows, body, 0)


def _inproj_kernel(x_ref, mod_ref, nw_ref, w_ref, wab_ref, wabt_ref, o_ref, ab_ref, abt_ref, xn_ref):
    @pl.when(pl.program_id(2) == 0)
    def _():
        _norm_mod_to(x_ref, nw_ref, mod_ref, 0, 1, xn_ref)
        xn = xn_ref[...]
        ab_ref[...] = jnp.dot(xn, wab_ref[...], preferred_element_type=F32)
        abt_ref[...] = lax.dot_general(wabt_ref[...], xn, (((1,), (1,)), ((), ())),
                                       preferred_element_type=F32)

    w = w_ref[...]
    for r in range(0, xn_ref.shape[0], ROW_SLAB):
        o_ref[r:r + ROW_SLAB, :] = jnp.dot(xn_ref[r:r + ROW_SLAB, :], w, preferred_element_type=F32)


def _inproj(x3, mod, mod_base, nw, w_main, w_ab, w_abt):
    nb, rows, d = x3.shape
    n = w_main.shape[1]
    tm = min(1024, rows)
    tn = 1024
    n_gate_rows = w_abt.shape[0]
    return pl.pallas_call(
        _inproj_kernel,
        out_shape=(jax.ShapeDtypeStruct((nb, rows, n), F32),
                   jax.ShapeDtypeStruct((nb, rows, 128), F32),
                   jax.ShapeDtypeStruct((nb, n_gate_rows, rows), F32)),
        grid=(nb, rows // tm, n // tn),
        in_specs=[pl.BlockSpec((None, tm, d), lambda b, i, j: (b, i, 0)),
                  pl.BlockSpec((None, N_MOD, d), lambda b, i, j: (mod_base + b, 0, 0)),
                  pl.BlockSpec((1, d), lambda b, i, j: (0, 0)),
                  pl.BlockSpec((d, tn), lambda b, i, j: (0, j)),
                  pl.BlockSpec((d, 128), lambda b, i, j: (0, 0)),
                  pl.BlockSpec((n_gate_rows, d), lambda b, i, j: (0, 0))],
        out_specs=(pl.BlockSpec((None, tm, tn), lambda b, i, j: (b, i, j)),
                   pl.BlockSpec((None, tm, 128), lambda b, i, j: (b, i, 0)),
                   pl.BlockSpec((None, n_gate_rows, tm), lambda b, i, j: (b, 0, i))),
        scratch_shapes=[pltpu.VMEM((tm, d), BF16)],
        compiler_params=_cparams(3),
        name="in_proj",
    )(x3, mod, nw, w_main, w_ab, w_abt)


def _split3(x):
    hi = x.astype(BF16)
    r1 = x - hi.astype(F32)
    mid = r1.astype(BF16)
    lo = (r1 - mid.astype(F32)).astype(BF16)
    return hi, mid, lo


def _gate_kernel(abx_ref, abtx_ref, abh_ref, abth_ref, alog_ref, dtb_ref, alogt_ref, dtbt_ref,
                 gb_ref, cum_ref, cumt_ref, *, seq_x, seq_h):
    c = CHUNK
    n_gt = abtx_ref.shape[0]
    slot_l = lax.broadcasted_iota(jnp.int32, (c, 128), 1) % GATE_SLOTS
    row_slot = lax.broadcasted_iota(jnp.int32, (n_gt, c), 0) % GATE_SLOTS
    ri = lax.broadcasted_iota(jnp.int32, (c, c), 0)
    ci = lax.broadcasted_iota(jnp.int32, (c, c), 1)
    m_le = jnp.where(ci <= ri, 1.0, 0.0).astype(BF16)
    m_ge = jnp.where(ci >= ri, 1.0, 0.0).astype(BF16)
    alog_r = alog_ref[...]
    dtb_r = dtb_ref[...]
    alogt = alogt_ref[...]
    dtbt = dtbt_ref[...]

    def part(ab_ref, abt_ref, row0, n_rows):
        def body(r, carry):
            src = pl.ds(pl.multiple_of(r * c, c), c)
            dst = pl.ds(pl.multiple_of(row0 + r * c, c), c)
            a = ab_ref[src, :]
            g = -jnp.exp(alog_r) * _softplus(a + dtb_r)
            gb = jnp.where(slot_l < N_DIR, g, 1.0 / (1.0 + jnp.exp(-a)))
            gb_ref[dst, :] = gb
            pieces = jnp.concatenate(_split3(gb), axis=1)
            pre = jnp.dot(m_le, pieces, preferred_element_type=F32)
            suf = jnp.dot(m_ge, pieces, preferred_element_type=F32)
            pre = pre[:, 0:128] + pre[:, 128:256] + pre[:, 256:384]
            suf = suf[:, 0:128] + suf[:, 128:256] + suf[:, 256:384]
            cum_ref[dst, :] = jnp.where(slot_l == 0, pre, suf)
            gt = -jnp.exp(alogt) * _softplus(abt_ref[:, src] + dtbt)
            pieces_t = jnp.concatenate(_split3(gt), axis=0)
            pre_t = jnp.dot(pieces_t, m_ge, preferred_element_type=F32)
            suf_t = jnp.dot(pieces_t, m_le, preferred_element_type=F32)
            pre_t = pre_t[0:n_gt] + pre_t[n_gt:2 * n_gt] + pre_t[2 * n_gt:3 * n_gt]
            suf_t = suf_t[0:n_gt] + suf_t[n_gt:2 * n_gt] + suf_t[2 * n_gt:3 * n_gt]
            cumt_ref[:, dst] = jnp.where(row_slot == 0, pre_t, suf_t)
            return carry

        lax.fori_loop(0, n_rows // c, body, 0, unroll=2)

    part(abh_ref, abth_ref, 0, seq_h)
    part(abx_ref, abtx_ref, seq_h, seq_x)


def _gates(abx, abtx, abh, abth, alog_c, dtb_c, alog_t, dtb_t):
    nb, seq_x, _ = abx.shape
    seq_h = abh.shape[1]
    n_gt = abtx.shape[1]
    seq_t = seq_x + seq_h
    return pl.pallas_call(
        functools.partial(_gate_kernel, seq_x=seq_x, seq_h=seq_h),
        out_shape=(jax.ShapeDtypeStruct((nb, seq_t, 128), F32),
                   jax.ShapeDtypeStruct((nb, seq_t, 128), F32),
                   jax.ShapeDtypeStruct((nb, n_gt, seq_t), F32)),
        grid=(nb,),
        in_specs=[pl.BlockSpec((None, seq_x, 128), lambda b: (b, 0, 0)),
                  pl.BlockSpec((None, n_gt, seq_x), lambda b: (b, 0, 0)),
                  pl.BlockSpec((None, seq_h, 128), lambda b: (b, 0, 0)),
                  pl.BlockSpec((None, n_gt, seq_h), lambda b: (b, 0, 0)),
                  pl.BlockSpec((1, 128), lambda b: (0, 0)),
                  pl.BlockSpec((1, 128), lambda b: (0, 0)),
                  pl.BlockSpec((n_gt, 128), lambda b: (0, 0)),
                  pl.BlockSpec((n_gt, 128), lambda b: (0, 0))],
        out_specs=(pl.BlockSpec((None, seq_t, 128), lambda b: (b, 0, 0)),
                   pl.BlockSpec((None, seq_t, 128), lambda b: (b, 0, 0)),
                   pl.BlockSpec((None, n_gt, seq_t), lambda b: (b, 0, 0))),
        compiler_params=_cparams(1),
        name="gates",
    )(abx, abtx, abh, abth, alog_c, dtb_c, alog_t, dtb_t)


def _delta_kernel(*refs, seq_x, seq_h, hg, ctx_out):
    n_in = 14
    (qx_ref, kx_ref, vx_ref, zx_ref, qh_ref, kh_ref, vh_ref, zh_ref,
     cq_ref, ck_ref, cv_ref, gb_ref, cum_ref, cumt_ref) = refs[:n_in]
    onorm_ref = refs[n_in]
    if ctx_out:
        ox_ref, oh_ref = refs[n_in + 1:n_in + 3]
        scratch = refs[n_in + 3:]
    else:
        ox_ref = refs[n_in + 1]
        oh_ref = None
        scratch = refs[n_in + 2:]
    (pad_s, q_s, k_s, v_s, o_s, u_s, wq_s, ktt_s, at_s, egl_s, st_s) = scratch

    c = CHUNK
    hd = HEAD_DIM
    width = hg * hd
    seq_t = seq_h + seq_x
    nc_h = seq_h // c
    nc_x = seq_x // c
    nc = nc_h + nc_x
    n_chain = hg * N_DIR

    def conv_part(src_ref, w_ref, dst_ref, row0, n_rows, mode):
        tile = 256 if n_rows % 256 == 0 else 128
        pad_s[0:8, :] = jnp.zeros((8, width), F32)
        pad_s[8:8 + n_rows, :] = src_ref[...]
        pad_s[8 + n_rows:16 + n_rows, :] = jnp.zeros((8, width), F32)
        taps = [w_ref[i:i + 1, :] for i in range(CONV_K)]

        for r in range(n_rows // tile):
            start = r * tile
            acc = None
            for i in range(CONV_K):
                off = start + 8 - CONV_K // 2 + i
                term = pad_s[off:off + tile, :] * taps[i]
                acc = term if acc is None else acc + term
            y = _silu(acc)
            if mode != "v":
                cols = []
                for hl in range(hg):
                    yh = y[:, hl * hd:(hl + 1) * hd]
                    yh = yh * lax.rsqrt(jnp.sum(yh * yh, axis=-1, keepdims=True) + EPS)
                    cols.append(yh)
                y = cols[0] if hg == 1 else jnp.concatenate(cols, axis=1)
                if mode == "q":
                    y = y * (hd ** -0.5)
            dst_ref[row0 + start:row0 + start + tile, :] = y

    conv_part(qh_ref, cq_ref, q_s, 0, seq_h, "q")
    conv_part(qx_ref, cq_ref, q_s, seq_h, seq_x, "q")
    conv_part(kh_ref, ck_ref, k_s, 0, seq_h, "k")
    conv_part(kx_ref, ck_ref, k_s, seq_h, seq_x, "k")
    conv_part(vh_ref, cv_ref, v_s, 0, seq_h, "v")
    conv_part(vx_ref, cv_ref, v_s, seq_h, seq_x, "v")

    shift = (128 - pl.program_id(1) * (hg * GATE_SLOTS)) % 128
    ri = lax.broadcasted_iota(jnp.int32, (c, c), 0)
    ci = lax.broadcasted_iota(jnp.int32, (c, c), 1)
    lower_incl = (ci <= ri)
    upper_incl = (ci >= ri)

    eye = jnp.where(ri == ci, 1.0, 0.0).astype(F32)
    masks = ((lower_incl, ci < ri), (upper_incl, ci > ri))

    def bdot(a, b):
        return jnp.dot(a.astype(BF16), b.astype(BF16), preferred_element_type=F32)

    same_blk = (ri // INV_BLOCK) == (ci // INV_BLOCK)

    def tri_inverse(nms, lowers):
        nds = [jnp.where(same_blk, nm, 0.0) for nm in nms]
        ms = [eye + nd for nd in nds]
        ps = [bdot(nd, nd) for nd in nds]
        for _ in range(int(math.log2(INV_BLOCK)) - 2):
            rs = [bdot(p, jnp.concatenate([m, p], axis=1)) for m, p in zip(ms, ps)]
            ms = [m + r_[:, 0:c] for m, r_ in zip(ms, rs)]
            ps = [r_[:, c:2 * c] for r_ in rs]
        rs = [bdot(p, m) for m, p in zip(ms, ps)]
        ms = [m + r_ for m, r_ in zip(ms, rs)]
        s = INV_BLOCK
        while s < c:
            starts = list(range(0, c, s))
            pair = ((ri // (2 * s)) == (ci // (2 * s))) & ((ri // s) != (ci // s))
            zero = jnp.zeros((s, c), BF16)
            hits = [[(a // s) % 2 == (1 if lower else 0) for a in starts] for lower in lowers]

            def take(xm, hit):
                return jnp.concatenate([xm[a:a + s] for a, t in zip(starts, hit) if t], axis=0)

            m_sels = [take(m, hit) for m, hit in zip(ms, hits)]
            xs = [bdot(take(jnp.where(pair, nm, 0.0), hit), m).astype(BF16)
                  for nm, m, hit in zip(nms, ms, hits)]
            ys = []
            for x, m_sel, hit in zip(xs, m_sels, hits):
                slabs, k_ = [], 0
                for t in hit:
                    slabs.append(x[k_ * s:(k_ + 1) * s] if t else zero)
                    k_ += 1 if t else 0
                ys.append(jnp.dot(m_sel.astype(BF16), jnp.concatenate(slabs, axis=0),
                                  preferred_element_type=F32))
            new_ms = []
            for m, m_sel, y, hit in zip(ms, m_sels, ys, hits):
                new_sel = m_sel + y
                slabs, k_ = [], 0
                for a, t in zip(starts, hit):
                    slabs.append(new_sel[k_ * s:(k_ + 1) * s] if t else m[a:a + s])
                    k_ += 1 if t else 0
                new_ms.append(jnp.concatenate(slabs, axis=0))
            ms = new_ms
            s *= 2
        return ms

    n_prep = PREP_CHUNKS if nc % PREP_CHUNKS == 0 else 1

    def prep(grp, carry):
        chs = [grp * n_prep + j for j in range(n_prep)]
        rows = [pl.ds(pl.multiple_of(ch * c, c), c) for ch in chs]
        gbc = [pltpu.roll(gb_ref[r, :], shift, 1) for r in rows]
        cumc = [pltpu.roll(cum_ref[r, :], shift, 1) for r in rows]
        cumtc = [cumt_ref[:, r] for r in rows]
        qs, ks, vs, aas = {}, {}, {}, {}
        for j in range(n_prep):
            for hl in range(hg):
                cols = slice(hl * hd, (hl + 1) * hd)
                qs[j, hl] = q_s[rows[j], cols]
                ks[j, hl] = k_s[rows[j], cols]
                vs[j, hl] = v_s[rows[j], cols]
                qk = jnp.concatenate([qs[j, hl], ks[j, hl]], axis=0).astype(BF16)
                aas[j, hl] = lax.dot_general(qk, ks[j, hl].astype(BF16), (((1,), (1,)), ((), ())),
                                             preferred_element_type=F32)
        probs = [(j, hl, d) for j in range(n_prep) for hl in range(hg) for d in range(N_DIR)]
        nms, betas, cums, decays = [], [], [], []
        for j, hl, d in probs:
            incl, strict = masks[d]
            base = hl * GATE_SLOTS
            beta_b = jnp.broadcast_to(gbc[j][:, base + N_DIR + d:base + N_DIR + d + 1], (c, hd))
            cum_b = jnp.broadcast_to(cumc[j][:, base + d:base + d + 1], (c, hd))
            cum_r = cumtc[j][base + d:base + d + 1, :]
            decay = jnp.where(incl, jnp.exp(jnp.where(incl, cum_b - cum_r, 0.0)), 0.0)
            nms.append(jnp.where(strict, -(beta_b * aas[j, hl][c:2 * c] * decay), 0.0))
            betas.append(beta_b)
            cums.append(cum_b)
            decays.append(decay)
        ms = tri_inverse(nms, [d == 0 for _, _, d in probs])
        e_cs = [jnp.exp(cum_b) for cum_b in cums]
        uws = [bdot(m, jnp.concatenate([vs[j, hl] * beta_b, ks[j, hl] * (beta_b * e_c)], axis=1))
               for (j, hl, d), m, beta_b, e_c in zip(probs, ms, betas, e_cs)]
        for i, (j, hl, d) in enumerate(probs):
            cum_b, e_c, uw = cums[i], e_cs[i], uws[i]
            gl_b = cum_b[c - 1:c, :] if d == 0 else cum_b[0:1, :]
            kt = ks[j, hl] * jnp.exp(gl_b - cum_b)
            slot = (hl * N_DIR + d) * nc + chs[j]
            u_s[pl.ds(pl.multiple_of(slot * c, c), c), :] = uw[:, 0:hd]
            wq_s[pl.ds(pl.multiple_of(slot * 2 * c, c), c), :] = uw[:, hd:2 * hd].astype(BF16)
            wq_s[pl.ds(pl.multiple_of(slot * 2 * c + c, c), c), :] = (qs[j, hl] * e_c).astype(BF16)
            ktt_s[pl.ds(pl.multiple_of(slot * hd, hd), hd), :] = kt.T.astype(BF16)
            at_s[pl.ds(pl.multiple_of(slot * c, c), c), :] = (aas[j, hl][0:c] * decays[i]).astype(BF16)
            egl_s[pl.ds(pl.multiple_of(slot * 8, 8), 8), :] = jnp.broadcast_to(jnp.exp(gl_b), (8, hd))
        return carry

    lax.fori_loop(0, nc // n_prep, prep, 0)

    st_s[...] = jnp.zeros(st_s.shape, F32)
    o_s[...] = jnp.zeros(o_s.shape, F32)

    def scan_step(ch_f, ch_b, with_q):
        chains = [(hl, d) for hl in range(hg) for d in range(N_DIR)]
        chs = [ch_f if d == 0 else ch_b for _, d in chains]
        slots = [i * nc + ch for i, ch in enumerate(chs)]
        s_olds = [st_s[i * hd:(i + 1) * hd, :] for i in range(len(chains))]
        n_lhs = 2 * c if with_q else c
        rs = [jnp.dot(wq_s[pl.ds(pl.multiple_of(slot * 2 * c, 2 * c), n_lhs), :], s_old.astype(BF16),
                      preferred_element_type=F32) for slot, s_old in zip(slots, s_olds)]
        v_bs = [(u_s[pl.ds(pl.multiple_of(slot * c, c), c), :] - r_[0:c]).astype(BF16)
                for slot, r_ in zip(slots, rs)]
        kvs = [jnp.dot(ktt_s[pl.ds(pl.multiple_of(slot * hd, hd), hd), :], v_b, preferred_element_type=F32)
               for slot, v_b in zip(slots, v_bs)]
        if with_q:
            avs = [jnp.dot(at_s[pl.ds(pl.multiple_of(slot * c, c), c), :], v_b, preferred_element_type=F32)
                   for slot, v_b in zip(slots, v_bs)]
        for i, (hl, d) in enumerate(chains):
            egl = egl_s[pl.ds(pl.multiple_of(slots[i] * 8, 8), 1), :]
            st_s[i * hd:(i + 1) * hd, :] = s_olds[i] * egl + kvs[i]
            if with_q:
                orow = pl.ds(pl.multiple_of(chs[i] * c, c), c)
                ocol = slice(hl * hd, (hl + 1) * hd)
                o_s[orow, ocol] = o_s[orow, ocol] + (rs[i][c:2 * c] + avs[i])

    def scan_h(it, carry):
        scan_step(it, nc_h - 1 - it, ctx_out)
        return carry

    def scan_x(it, carry):
        scan_step(nc_h + it, nc - 1 - it, True)
        return carry

    lax.fori_loop(0, nc_h, scan_h, 0)
    lax.fori_loop(0, nc_x, scan_x, 0)

    onw = onorm_ref[...]

    def finish(z_ref, out_ref, row0, n_rows):
        tile = 256 if n_rows % 256 == 0 else 128

        def body(r, carry):
            start = pl.multiple_of(r * tile, tile)
            src = pl.ds(pl.multiple_of(row0 + start, CHUNK), tile)
            dst = pl.ds(start, tile)
            outs = []
            for hl in range(hg):
                cols = slice(hl * hd, (hl + 1) * hd)
                o = o_s[src, cols]
                outs.append(_rms(o) * onw * _silu(z_ref[dst, cols]))
            y = outs[0] if hg == 1 else jnp.concatenate(outs, axis=1)
            out_ref[dst, :] = y.astype(out_ref.dtype)
            return carry

        lax.fori_loop(0, n_rows // tile, body, 0)

    finish(zx_ref, ox_ref, seq_h, seq_x)
    if ctx_out:
        finish(zh_ref, oh_ref, 0, seq_h)


def _delta(pxm, phm, gb, cum, cumt, conv_w, onorm, n_heads, ctx_out):
    nb, seq_x, _ = pxm.shape
    seq_h = phm.shape[1]
    hg = HEADS_PER_STEP
    hd = HEAD_DIM
    width = hg * hd
    n_groups = n_heads // hg
    lin_w = n_heads * hd
    seq_t = seq_x + seq_h
    nc = seq_t // CHUNK
    n_chain = hg * N_DIR
    n_gt = hg * GATE_SLOTS

    def main_spec(rows, part):
        return pl.BlockSpec((None, rows, width), lambda b, g: (b, 0, part * n_groups + g))

    in_specs = (
        [main_spec(seq_x, p) for p in range(4)]
        + [main_spec(seq_h, p) for p in range(4)]
        + [pl.BlockSpec((CONV_K, width), lambda b, g, p=p: (0, p * n_groups + g)) for p in range(3)]
        + [pl.BlockSpec((None, seq_t, 128), lambda b, g: (b, 0, 0)),
           pl.BlockSpec((None, seq_t, 128), lambda b, g: (b, 0, 0)),
           pl.BlockSpec((None, n_gt, seq_t), lambda b, g: (b, g, 0)),
           pl.BlockSpec((1, hd), lambda b, g: (0, 0))]
    )
    out_shape = [jax.ShapeDtypeStruct((nb, seq_x, lin_w), BF16)]
    out_specs = [pl.BlockSpec((None, seq_x, width), lambda b, g: (b, 0, g))]
    if ctx_out:
        out_shape.append(jax.ShapeDtypeStruct((nb, seq_h, lin_w), BF16))
        out_specs.append(pl.BlockSpec((None, seq_h, width), lambda b, g: (b, 0, g)))
    scratch = [
        pltpu.VMEM((seq_x + 16, width), F32),
        pltpu.VMEM((seq_t, width), F32),
        pltpu.VMEM((seq_t, width), F32),
        pltpu.VMEM((seq_t, width), F32),
        pltpu.VMEM((seq_t, width), F32),
        pltpu.VMEM((n_chain * nc * CHUNK, hd), F32),
        pltpu.VMEM((n_chain * nc * 2 * CHUNK, hd), BF16),
        pltpu.VMEM((n_chain * nc * hd, CHUNK), BF16),
        pltpu.VMEM((n_chain * nc * CHUNK, CHUNK), BF16),
        pltpu.VMEM((n_chain * nc * 8, hd), F32),
        pltpu.VMEM((n_chain * hd, hd), F32),
    ]
    args = [pxm] * 4 + [phm] * 4 + [conv_w] * 3 + [gb, cum, cumt, onorm]
    res = pl.pallas_call(
        functools.partial(_delta_kernel, seq_x=seq_x, seq_h=seq_h, hg=hg, ctx_out=ctx_out),
        out_shape=tuple(out_shape),
        grid=(nb, n_groups),
        in_specs=in_specs,
        out_specs=tuple(out_specs),
        scratch_shapes=scratch,
        compiler_params=_cparams(2),
        name="gated_delta",
    )(*args)
    return res if ctx_out else (res[0], None)


def _pool_kernel(p_ref, w_ref, sc_ref, o_ref, y_s, *, on_grid, n_groups, gw):
    n = p_ref.shape[0]
    blk = 2 * GRID_W if on_grid else n
    period = GRID_W if on_grid else n
    ri = lax.broadcasted_iota(jnp.int32, (blk, blk), 0)
    ci = lax.broadcasted_iota(jnp.int32, (blk, blk), 1)
    pos = lax.broadcasted_iota(jnp.int32, (blk, gw), 0) % period

    for g in range(n_groups):
        size = POOL_SIZES[g]
        half = size // 2
        cols = slice(g * gw, (g + 1) * gw)
        if on_grid:
            n_rows = n // GRID_W
            for r in range(n_rows):
                lo = max(r - half, 0)
                hi = min(r + size - 1 - half, n_rows - 1)
                acc = p_ref[lo * GRID_W:(lo + 1) * GRID_W, cols]
                for rr in range(lo + 1, hi + 1):
                    acc = acc + p_ref[rr * GRID_W:(rr + 1) * GRID_W, cols]
                y_s[r * GRID_W:(r + 1) * GRID_W, :] = acc / float(hi - lo + 1)
            src = y_s
            src_cols = slice(0, gw)
        else:
            src = p_ref
            src_cols = cols
        band = ((ci >= ri - half) & (ci <= ri + size - 1 - half) & (ci // period == ri // period))
        band = jnp.where(band, 1.0, 0.0).astype(BF16)
        lo_c = jnp.maximum(pos - half, 0)
        hi_c = jnp.minimum(pos + size - 1 - half, period - 1)
        cnt = (hi_c - lo_c + 1).astype(F32)
        w_g = w_ref[g]
        scale = sc_ref[:, cols]

        n_blk = n // blk
        group = POOL_BLOCKS if n_blk % POOL_BLOCKS == 0 else 1

        def body(t, carry):
            rows = [pl.ds(pl.multiple_of((t * group + j) * blk, blk), blk) for j in range(group)]
            pieces = []
            for r in rows:
                y = src[r, src_cols]
                hi_p = y.astype(BF16)
                r1 = y - hi_p.astype(F32)
                mid_p = r1.astype(BF16)
                lo_p = (r1 - mid_p.astype(F32)).astype(BF16)
                pieces.append(jnp.concatenate([hi_p, mid_p, lo_p], axis=1))
            tots = [jnp.dot(band, pc, preferred_element_type=F32) for pc in pieces]
            dlts = []
            for r, tot in zip(rows, tots):
                mean = (tot[:, 0:gw] + tot[:, gw:2 * gw] + tot[:, 2 * gw:3 * gw]) / cnt
                dlts.append((mean - p_ref[r, cols]).astype(BF16))
            outs = [jnp.dot(dlt, w_g, preferred_element_type=F32) for dlt in dlts]
            for r, out in zip(rows, outs):
                o_ref[r, cols] = (out * scale).astype(o_ref.dtype)
            return carry

        lax.fori_loop(0, n_blk // group, body, 0)


def _pool(pm, pool_w, pool_scale, part0, on_grid):
    nb, rows, _ = pm.shape
    n_groups, gw, _ = pool_w.shape
    pw = n_groups * gw
    return pl.pallas_call(
        functools.partial(_pool_kernel, on_grid=on_grid, n_groups=n_groups, gw=gw),
        out_shape=jax.ShapeDtypeStruct((nb, rows, pw), BF16),
        grid=(nb,),
        in_specs=[pl.BlockSpec((None, rows, pw), lambda b: (b, 0, part0)),
                  pl.BlockSpec((n_groups, gw, gw), lambda b: (0, 0, 0)),
                  pl.BlockSpec((1, pw), lambda b: (0, 0))],
        out_specs=pl.BlockSpec((None, rows, pw), lambda b: (b, 0, 0)),
        scratch_shapes=[pltpu.VMEM((rows, gw), F32)],
        compiler_params=_cparams(1),
        name="pool_mixer",
    )(pm, pool_w, pool_scale)


def _outproj_kernel(lin_ref, pool_ref, x_ref, mod_ref, wa_ref, wb_ref, o_ref):
    y = (jnp.dot(lin_ref[...], wa_ref[...], preferred_element_type=F32)
         + jnp.dot(pool_ref[...], wb_ref[...], preferred_element_type=F32))
    o_ref[...] = x_ref[...] + mod_ref[2:3, :] * y


def _outproj(lin, pool, x3, mod, mod_base, w_a, w_b):
    nb, rows, d = x3.shape
    tm = min(512, rows)
    ka = w_a.shape[0]
    kb = w_b.shape[0]
    return pl.pallas_call(
        _outproj_kernel,
        out_shape=jax.ShapeDtypeStruct((nb, rows, d), F32),
        grid=(nb, rows // tm),
        in_specs=[pl.BlockSpec((None, tm, ka), lambda b, i: (b, i, 0)),
                  pl.BlockSpec((None, tm, kb), lambda b, i: (b, i, 0)),
                  pl.BlockSpec((None, tm, d), lambda b, i: (b, i, 0)),
                  pl.BlockSpec((None, N_MOD, d), lambda b, i: (mod_base + b, 0, 0)),
                  pl.BlockSpec((ka, d), lambda b, i: (0, 0)),
                  pl.BlockSpec((kb, d), lambda b, i: (0, 0))],
        out_specs=pl.BlockSpec((None, tm, d), lambda b, i: (b, i, 0)),
        compiler_params=_cparams(2),
        name="out_proj",
    )(lin, pool, x3, mod, w_a, w_b)


def _mlp_kernel(x_ref, mod_ref, nw_ref, wu_ref, wd_ref, fw_ref, o_ref, xn_ref, *, final_norm):
    f = pl.program_id(2)

    @pl.when(f == 0)
    def _():
        _norm_mod_to(x_ref, nw_ref, mod_ref, 3, 4, xn_ref)

    @pl.when(f == 0)
    def _():
        o_ref[...] = jnp.zeros(o_ref.shape, F32)

    wu = wu_ref[...]
    wd = wd_ref[...]
    slabs = range(0, xn_ref.shape[0], ROW_SLAB)
    hs = [jnp.dot(xn_ref[r:r + ROW_SLAB, :], wu, preferred_element_type=F32) for r in slabs]
    hs = [jnp.square(jnp.maximum(h, 0.0)).astype(BF16) for h in hs]
    for r, h in zip(slabs, hs):
        o_ref[r:r + ROW_SLAB, :] += jnp.dot(h, wd, preferred_element_type=F32)

    @pl.when(f == pl.num_programs(2) - 1)
    def _():
        gate = mod_ref[5:6, :]
        fw = fw_ref[...]
        rows = 128

        def body(r, carry):
            sl = pl.ds(pl.multiple_of(r * rows, rows), rows)
            y = x_ref[sl, :] + gate * o_ref[sl, :]
            if final_norm:
                y = _rms(y) * fw
            o_ref[sl, :] = y
            return carry

        lax.fori_loop(0, x_ref.shape[0] // rows, body, 0)


def _mlp(x3, mod, mod_base, nw, w_up, w_down, final_w, final_norm):
    nb, rows, d = x3.shape
    dff = w_up.shape[1]
    tm = min(1024, rows)
    tf = 512
    return pl.pallas_call(
        functools.partial(_mlp_kernel, final_norm=final_norm),
        out_shape=jax.ShapeDtypeStruct((nb, rows, d), F32),
        grid=(nb, rows // tm, dff // tf),
        in_specs=[pl.BlockSpec((None, tm, d), lambda b, i, f: (b, i, 0)),
                  pl.BlockSpec((None, N_MOD, d), lambda b, i, f: (mod_base + b, 0, 0)),
                  pl.BlockSpec((1, d), lambda b, i, f: (0, 0)),
                  pl.BlockSpec((d, tf), lambda b, i, f: (0, f)),
                  pl.BlockSpec((tf, d), lambda b, i, f: (f, 0)),
                  pl.BlockSpec((1, d), lambda b, i, f: (0, 0))],
        out_specs=pl.BlockSpec((None, tm, d), lambda b, i, f: (b, i, 0)),
        scratch_shapes=[pltpu.VMEM((tm, d), BF16)],
        compiler_params=_cparams(3),
        name="mlp",
    )(x3, mod, nw, w_up, w_down, final_w)


def _gate_layouts(w_in_l, a_log_l, dt_bias_l, lin_w, n_heads):
    d = w_in_l.shape[0]
    decay_start = 4 * lin_w
    ab = w_in_l[:, decay_start:decay_start + 2 * N_DIR * n_heads]
    ab = ab.reshape(d, 2 * N_DIR, n_heads)
    ab = jnp.transpose(ab, (0, 2, 1))
    ab = jnp.pad(ab, ((0, 0), (0, 0), (0, GATE_SLOTS - 2 * N_DIR)))
    w_ab = ab.reshape(d, n_heads * GATE_SLOTS)
    w_abt = w_ab.T.astype(BF16)
    w_ab = jnp.pad(w_ab, ((0, 0), (0, 128 - n_heads * GATE_SLOTS))).astype(BF16)

    def per_head(p):
        return jnp.pad(p.T.astype(F32), ((0, 0), (0, GATE_SLOTS - N_DIR)))

    alog = per_head(a_log_l)
    dtb = per_head(dt_bias_l)
    alog_c = jnp.pad(alog.reshape(1, n_heads * GATE_SLOTS), ((0, 0), (0, 128 - n_heads * GATE_SLOTS)))
    dtb_c = jnp.pad(dtb.reshape(1, n_heads * GATE_SLOTS), ((0, 0), (0, 128 - n_heads * GATE_SLOTS)))
    alog_t = jnp.broadcast_to(alog.reshape(n_heads * GATE_SLOTS, 1), (n_heads * GATE_SLOTS, 128))
    dtb_t = jnp.broadcast_to(dtb.reshape(n_heads * GATE_SLOTS, 1), (n_heads * GATE_SLOTS, 128))
    return w_ab, w_abt, alog_c, dtb_c, alog_t, dtb_t


def kernel(x, c, ctx, c_ctx, w_ada, b_ada, norm1_w, norm2_w, w_in, conv_w, a_log, dt_bias, onorm_w,
           pool_w, pool_scale, w_out, w_up, w_down, final_norm_w):
    nb, seq_x, d = x.shape
    seq_h = ctx.shape[1]
    depth = w_ada.shape[0]
    n_heads = a_log.shape[2]
    lin_w = n_heads * HEAD_DIM
    pool_width = pool_scale.shape[1]
    pool_start = 4 * lin_w + 2 * N_DIR * n_heads
    assert nb + 1 <= MOD_ROWS and seq_x % CHUNK == 0 and seq_h % CHUNK == 0
    assert n_heads % HEADS_PER_STEP == 0 and seq_x % GRID_W == 0

    c_all = jnp.concatenate([c, c_ctx[None, :], jnp.zeros((MOD_ROWS - nb - 1, d), F32)], axis=0)
    mod_all = _ada(c_all, w_ada, b_ada).reshape(depth, MOD_ROWS, N_MOD, d)

    h = ctx
    for l in range(depth):
        ctx_out = l < depth - 1
        mod = mod_all[l]
        w_main = jnp.concatenate([w_in[l, :, :4 * lin_w], w_in[l, :, pool_start:]], axis=1).astype(BF16)
        w_ab, w_abt, alog_c, dtb_c, alog_t, dtb_t = _gate_layouts(w_in[l], a_log[l], dt_bias[l], lin_w, n_heads)
        nw1 = norm1_w[l].reshape(1, d)
        nw2 = norm2_w[l].reshape(1, d)
        w_oa = w_out[l, :lin_w].astype(BF16)
        w_ob = w_out[l, lin_w:].astype(BF16)
        w_u = w_up[l].astype(BF16)
        w_d = w_down[l].astype(BF16)
        pw = pool_w[l].astype(BF16)
        ps = pool_scale[l].reshape(1, pool_width)
        fw = final_norm_w.reshape(1, d)

        pxm, abx, abtx = _inproj(x, mod, 0, nw1, w_main, w_ab, w_abt)
        h3 = h.reshape(1, nb * seq_h, d)
        phm, abh, abth = _inproj(h3, mod, nb, nw1, w_main, w_ab, w_abt)
        phm = phm.reshape(nb, seq_h, -1)
        abh = abh.reshape(nb, seq_h, 128)
        abth = jnp.transpose(abth.reshape(-1, nb, seq_h), (1, 0, 2))

        gb, cum, cumt = _gates(abx, abtx, abh, abth, alog_c, dtb_c, alog_t, dtb_t)
        lin_x, lin_h = _delta(pxm, phm, gb, cum, cumt, conv_w[l], onorm_w[l].reshape(1, HEAD_DIM),
                              n_heads, ctx_out)
        part0 = (4 * lin_w) // pool_width
        pool_x = _pool(pxm, pw, ps, part0, True)
        x = _outproj(lin_x, pool_x, x, mod, 0, w_oa, w_ob)
        x = _mlp(x, mod, 0, nw2, w_u, w_d, fw, l == depth - 1)
        if ctx_out:
            pool_h = _pool(phm, pw, ps, part0, False)
            h3 = _outproj(lin_h.reshape(1, nb * seq_h, lin_w), pool_h.reshape(1, nb * seq_h, pool_width),
                          h3, mod, nb, w_oa, w_ob)
            h = _mlp(h3, mod, nb, nw2, w_u, w_d, fw, False).reshape(nb, seq_h, d)
    return x
```

```python
import functools
import math

import jax
import jax.numpy as jnp
from jax import lax
from jax.experimental import pallas as pl
from jax.experimental.pallas import tpu as pltpu

F32 = jnp.float32
BF16 = jnp.bfloat16

EPS = 1e-6
N_MOD = 6
HEAD_DIM = 128
N_DIR = 2
CONV_K = 5
GRID_W = 64
POOL_SIZES = (2, 4, 8, 16)
CHUNK = 128
INV_BLOCK = 16
GATE_SLOTS = 8
HEADS_PER_STEP = 2
POOL_BLOCKS = 4
PREP_CHUNKS = 3
ROW_SLAB = 256
MOD_ROWS = 24
VMEM_LIMIT = 56 * 1024 * 1024


def _silu(x):
    return x * (1.0 / (1.0 + jnp.exp(-x)))


def _softplus(x):
    return jnp.maximum(x, 0.0) + jnp.log1p(jnp.exp(-jnp.abs(x)))


def _cparams(n_axes):
    return pltpu.CompilerParams(dimension_semantics=("arbitrary",) * n_axes,
                                vmem_limit_bytes=VMEM_LIMIT)


def _ada_kernel(c_ref, w_ref, b_ref, o_ref):
    sc = _silu(c_ref[...]).astype(BF16)
    o_ref[...] = jnp.dot(sc, w_ref[...].astype(BF16), preferred_element_type=F32) + b_ref[...]


def _ada(c_all, w_ada, b_ada):
    depth, d, n = w_ada.shape
    tn = 1024
    return pl.pallas_call(
        _ada_kernel,
        out_shape=jax.ShapeDtypeStruct((depth, MOD_ROWS, n), F32),
        grid=(depth, n // tn),
        in_specs=[pl.BlockSpec((MOD_ROWS, d), lambda l, j: (0, 0)),
                  pl.BlockSpec((None, d, tn), lambda l, j: (l, 0, j)),
                  pl.BlockSpec((None, 1, tn), lambda l, j: (l, 0, j))],
        out_specs=pl.BlockSpec((None, MOD_ROWS, tn), lambda l, j: (l, 0, j)),
        compiler_params=_cparams(2),
        name="ada_mod",
    )(c_all, w_ada, b_ada.reshape(depth, 1, n))


def _rms(x):
    return x * lax.rsqrt(jnp.mean(x * x, axis=-1, keepdims=True) + EPS)


def _norm_mod_to(x_ref, nw_ref, mod_ref, i_shift, i_scale, out_ref, rows=128):
    nw = nw_ref[...]
    shift = mod_ref[i_shift:i_shift + 1, :]
    scale = mod_ref[i_scale:i_scale + 1, :]

    def body(r, carry):
        sl = pl.ds(pl.multiple_of(r * rows, rows), rows)
        y = _rms(x_ref[sl, :]) * nw
        out_ref[sl, :] = (y * (1.0 + scale) + shift).astype(out_ref.dtype)
        return carry

    lax.fori_loop(0, x_ref.shape[0] // rows, body, 0)


def _inproj_kernel(x_ref, mod_ref, nw_ref, w_ref, wab_ref, o_ref, ab_ref, abt_ref, xn_a, xn_b):
    s = pl.program_id(0)
    j = pl.program_id(1)
    tm = x_ref.shape[0]
    n_gate_rows = abt_ref.shape[0]

    @pl.when((s == 0) & (j == 0))
    def _():
        xn_b[...] = jnp.zeros(xn_b.shape, xn_b.dtype)

    def step(xn_fill, xn_use):
        @pl.when(j == 0)
        def _():
            ab = jnp.dot(xn_use[...], wab_ref[...], preferred_element_type=F32)
            ab_ref[...] = ab
            abt_ref[...] = ab.T[0:n_gate_rows, :]

        nw = nw_ref[...]
        shift = mod_ref[0:1, :]
        scale = mod_ref[1:2, :]
        n_parts = pl.num_programs(1) - 1
        part_rows = tm // 4
        row0 = jnp.minimum(j, n_parts - 1) * part_rows
        for r in range(0, part_rows, 128):
            sl = pl.ds(pl.multiple_of(row0 + r, 128), 128)
            y = _rms(x_ref[sl, :]) * nw
            xn_fill[sl, :] = (y * (1.0 + scale) + shift).astype(BF16)

        w = w_ref[...]
        for r in range(0, tm, ROW_SLAB):
            o_ref[r:r + ROW_SLAB, :] = jnp.dot(xn_use[r:r + ROW_SLAB, :], w,
                                               preferred_element_type=F32).astype(o_ref.dtype)

    @pl.when(s % 2 == 0)
    def _():
        step(xn_a, xn_b)

    @pl.when(s % 2 == 1)
    def _():
        step(xn_b, xn_a)


def _inproj(x3, mod, mod_base, nw, w_main, w_ab, n_gate_rows):
    nb, rows, d = x3.shape
    n = w_main.shape[1]
    tm = min(1024, rows)
    tn = 1024
    tpb = rows // tm
    n_tiles = nb * tpb
    assert n // tn == 5 and tm % 512 == 0

    def tile_in(s):
        t = jnp.minimum(s, n_tiles - 1)
        return t // tpb, t % tpb

    def tile_out(s):
        t = jnp.maximum(s - 1, 0)
        return t // tpb, t % tpb

    return pl.pallas_call(
        _inproj_kernel,
        out_shape=(jax.ShapeDtypeStruct((nb, rows, n), BF16),
                   jax.ShapeDtypeStruct((nb, rows, 128), F32),
                   jax.ShapeDtypeStruct((nb, n_gate_rows, rows), F32)),
        grid=(n_tiles + 1, n // tn),
        in_specs=[pl.BlockSpec((None, tm, d), lambda s, j: (*tile_in(s), 0)),
                  pl.BlockSpec((None, N_MOD, d), lambda s, j: (mod_base + tile_in(s)[0], 0, 0)),
                  pl.BlockSpec((1, d), lambda s, j: (0, 0)),
                  pl.BlockSpec((d, tn), lambda s, j: (0, j)),
                  pl.BlockSpec((d, 128), lambda s, j: (0, 0))],
        out_specs=(pl.BlockSpec((None, tm, tn), lambda s, j: (*tile_out(s), jnp.where(s == 0, 0, j))),
                   pl.BlockSpec((None, tm, 128), lambda s, j: (*tile_out(s), 0)),
                   pl.BlockSpec((None, n_gate_rows, tm), lambda s, j: (tile_out(s)[0], 0, tile_out(s)[1]))),
        scratch_shapes=[pltpu.VMEM((tm, d), BF16), pltpu.VMEM((tm, d), BF16)],
        compiler_params=_cparams(2),
        name="in_proj",
    )(x3, mod, nw, w_main, w_ab)


def _split3(x):
    hi = x.astype(BF16)
    r1 = x - hi.astype(F32)
    mid = r1.astype(BF16)
    lo = (r1 - mid.astype(F32)).astype(BF16)
    return hi, mid, lo


def _gate_kernel(abx_ref, abtx_ref, abh_ref, abth_ref, alog_ref, dtb_ref, alogt_ref, dtbt_ref,
                 gb_ref, cum_ref, cumt_ref, *, seq_x, seq_h):
    c = CHUNK
    n_gt = abtx_ref.shape[0]
    slot_l = lax.broadcasted_iota(jnp.int32, (c, 128), 1) % GATE_SLOTS
    row_slot = lax.broadcasted_iota(jnp.int32, (n_gt, c), 0) % GATE_SLOTS
    ri = lax.broadcasted_iota(jnp.int32, (c, c), 0)
    ci = lax.broadcasted_iota(jnp.int32, (c, c), 1)
    m_le = jnp.where(ci <= ri, 1.0, 0.0).astype(BF16)
    m_ge = jnp.where(ci >= ri, 1.0, 0.0).astype(BF16)
    alog_r = alog_ref[...]
    dtb_r = dtb_ref[...]
    alogt = alogt_ref[...]
    dtbt = dtbt_ref[...]

    def part(ab_ref, abt_ref, row0, n_rows):
        def body(r, carry):
            src = pl.ds(pl.multiple_of(r * c, c), c)
            dst = pl.ds(pl.multiple_of(row0 + r * c, c), c)
            a = ab_ref[src, :]
            g = -jnp.exp(alog_r) * _softplus(a + dtb_r)
            gb = jnp.where(slot_l < N_DIR, g, 1.0 / (1.0 + jnp.exp(-a)))
            gb_ref[dst, :] = gb
            pieces = jnp.concatenate(_split3(gb), axis=1)
            pre = jnp.dot(m_le, pieces, preferred_element_type=F32)
            suf = jnp.dot(m_ge, pieces, preferred_element_type=F32)
            pre = pre[:, 0:128] + pre[:, 128:256] + pre[:, 256:384]
            suf = suf[:, 0:128] + suf[:, 128:256] + suf[:, 256:384]
            cum_ref[dst, :] = jnp.where(slot_l == 0, pre, suf)
            gt = -jnp.exp(alogt) * _softplus(abt_ref[:, src] + dtbt)
            pieces_t = jnp.concatenate(_split3(gt), axis=0)
            pre_t = jnp.dot(pieces_t, m_ge, preferred_element_type=F32)
            suf_t = jnp.dot(pieces_t, m_le, preferred_element_type=F32)
            pre_t = pre_t[0:n_gt] + pre_t[n_gt:2 * n_gt] + pre_t[2 * n_gt:3 * n_gt]
            suf_t = suf_t[0:n_gt] + suf_t[n_gt:2 * n_gt] + suf_t[2 * n_gt:3 * n_gt]
            cumt_ref[:, dst] = jnp.where(row_slot == 0, pre_t, suf_t)
            return carry

        lax.fori_loop(0, n_rows // c, body, 0, unroll=2)

    part(abh_ref, abth_ref, 0, seq_h)
    part(abx_ref, abtx_ref, seq_h, seq_x)


def _gates(abx, abtx, abh, abth, alog_c, dtb_c, alog_t, dtb_t):
    nb, seq_x, _ = abx.shape
    seq_h = abh.shape[1]
    n_gt = abtx.shape[1]
    seq_t = seq_x + seq_h
    return pl.pallas_call(
        functools.partial(_gate_kernel, seq_x=seq_x, seq_h=seq_h),
        out_shape=(jax.ShapeDtypeStruct((nb, seq_t, 128), F32),
                   jax.ShapeDtypeStruct((nb, seq_t, 128), F32),
                   jax.ShapeDtypeStruct((nb, n_gt, seq_t), F32)),
        grid=(nb,),
        in_specs=[pl.BlockSpec((None, seq_x, 128), lambda b: (b, 0, 0)),
                  pl.BlockSpec((None, n_gt, seq_x), lambda b: (b, 0, 0)),
                  pl.BlockSpec((None, seq_h, 128), lambda b: (b, 0, 0)),
                  pl.BlockSpec((None, n_gt, seq_h), lambda b: (b, 0, 0)),
                  pl.BlockSpec((1, 128), lambda b: (0, 0)),
                  pl.BlockSpec((1, 128), lambda b: (0, 0)),
                  pl.BlockSpec((n_gt, 128), lambda b: (0, 0)),
                  pl.BlockSpec((n_gt, 128), lambda b: (0, 0))],
        out_specs=(pl.BlockSpec((None, seq_t, 128), lambda b: (b, 0, 0)),
                   pl.BlockSpec((None, seq_t, 128), lambda b: (b, 0, 0)),
                   pl.BlockSpec((None, n_gt, seq_t), lambda b: (b, 0, 0))),
        compiler_params=_cparams(1),
        name="gates",
    )(abx, abtx, abh, abth, alog_c, dtb_c, alog_t, dtb_t)


def _delta_kernel(*refs, seq_x, seq_h, hg, ctx_out):
    n_in = 14
    (qx_ref, kx_ref, vx_ref, zx_ref, qh_ref, kh_ref, vh_ref, zh_ref,
     cq_ref, ck_ref, cv_ref, gb_ref, cum_ref, cumt_ref) = refs[:n_in]
    onorm_ref = refs[n_in]
    if ctx_out:
        ox_ref, oh_ref = refs[n_in + 1:n_in + 3]
        scratch = refs[n_in + 3:]
    else:
        ox_ref = refs[n_in + 1]
        oh_ref = None
        scratch = refs[n_in + 2:]
    (pad_s, q_s, k_s, v_s, o_s, u_s, wq_s, ktt_s, at_s, egl_s, st_s) = scratch

    c = CHUNK
    hd = HEAD_DIM
    width = hg * hd
    seq_t = seq_h + seq_x
    nc_h = seq_h // c
    nc_x = seq_x // c
    nc = nc_h + nc_x
    n_chain = hg * N_DIR

    def conv_part(src_ref, w_ref, dst_ref, row0, n_rows, mode):
        tile = 256 if n_rows % 256 == 0 else 128
        pad_s[0:8, :] = jnp.zeros((8, width), F32)
        pad_s[8:8 + n_rows, :] = src_ref[...].astype(F32)
        pad_s[8 + n_rows:16 + n_rows, :] = jnp.zeros((8, width), F32)
        taps = [w_ref[i:i + 1, :] for i in range(CONV_K)]

        for r in range(n_rows // tile):
            start = r * tile
            acc = None
            for i in range(CONV_K):
                off = start + 8 - CONV_K // 2 + i
                term = pad_s[off:off + tile, :] * taps[i]
                acc = term if acc is None else acc + term
            y = _silu(acc)
            if mode != "v":
                cols = []
                for hl in range(hg):
                    yh = y[:, hl * hd:(hl + 1) * hd]
                    yh = yh * lax.rsqrt(jnp.sum(yh * yh, axis=-1, keepdims=True) + EPS)
                    cols.append(yh)
                y = cols[0] if hg == 1 else jnp.concatenate(cols, axis=1)
                if mode == "q":
                    y = y * (hd ** -0.5)
            dst_ref[row0 + start:row0 + start + tile, :] = y

    conv_part(qh_ref, cq_ref, q_s, 0, seq_h, "q")
    conv_part(qx_ref, cq_ref, q_s, seq_h, seq_x, "q")
    conv_part(kh_ref, ck_ref, k_s, 0, seq_h, "k")
    conv_part(kx_ref, ck_ref, k_s, seq_h, seq_x, "k")
    conv_part(vh_ref, cv_ref, v_s, 0, seq_h, "v")
    conv_part(vx_ref, cv_ref, v_s, seq_h, seq_x, "v")

    shift = (128 - pl.program_id(1) * (hg * GATE_SLOTS)) % 128
    ri = lax.broadcasted_iota(jnp.int32, (c, c), 0)
    ci = lax.broadcasted_iota(jnp.int32, (c, c), 1)
    lower_incl = (ci <= ri)
    upper_incl = (ci >= ri)

    eye = jnp.where(ri == ci, 1.0, 0.0).astype(F32)
    masks = ((lower_incl, ci < ri), (upper_incl, ci > ri))

    def bdot(a, b):
        return jnp.dot(a.astype(BF16), b.astype(BF16), preferred_element_type=F32)

    same_blk = (ri // INV_BLOCK) == (ci // INV_BLOCK)

    def tri_inverse(nms, lowers):
        nds = [jnp.where(same_blk, nm, 0.0) for nm in nms]
        ms = [eye + nd for nd in nds]
        ps = [bdot(nd, nd) for nd in nds]
        for _ in range(int(math.log2(INV_BLOCK)) - 2):
            rs = [bdot(p, jnp.concatenate([m, p], axis=1)) for m, p in zip(ms, ps)]
            ms = [m + r_[:, 0:c] for m, r_ in zip(ms, rs)]
            ps = [r_[:, c:2 * c] for r_ in rs]
        rs = [bdot(p, m) for m, p in zip(ms, ps)]
        ms = [m + r_ for m, r_ in zip(ms, rs)]
        s = INV_BLOCK
        while s < c:
            starts = list(range(0, c, s))
            pair = ((ri // (2 * s)) == (ci // (2 * s))) & ((ri // s) != (ci // s))
            zero = jnp.zeros((s, c), BF16)
            hits = [[(a // s) % 2 == (1 if lower else 0) for a in starts] for lower in lowers]

            def take(xm, hit):
                return jnp.concatenate([xm[a:a + s] for a, t in zip(starts, hit) if t], axis=0)

            m_sels = [take(m, hit) for m, hit in zip(ms, hits)]
            xs = [bdot(take(jnp.where(pair, nm, 0.0), hit), m).astype(BF16)
                  for nm, m, hit in zip(nms, ms, hits)]
            ys = []
            for x, m_sel, hit in zip(xs, m_sels, hits):
                slabs, k_ = [], 0
                for t in hit:
                    slabs.append(x[k_ * s:(k_ + 1) * s] if t else zero)
                    k_ += 1 if t else 0
                ys.append(jnp.dot(m_sel.astype(BF16), jnp.concatenate(slabs, axis=0),
                                  preferred_element_type=F32))
            new_ms = []
            for m, m_sel, y, hit in zip(ms, m_sels, ys, hits):
                new_sel = m_sel + y
                slabs, k_ = [], 0
                for a, t in zip(starts, hit):
                    slabs.append(new_sel[k_ * s:(k_ + 1) * s] if t else m[a:a + s])
                    k_ += 1 if t else 0
                new_ms.append(jnp.concatenate(slabs, axis=0))
            ms = new_ms
            s *= 2
        return ms

    n_prep = PREP_CHUNKS if nc % PREP_CHUNKS == 0 else 1

    def prep(grp, carry):
        chs = [grp * n_prep + j for j in range(n_prep)]
        rows = [pl.ds(pl.multiple_of(ch * c, c), c) for ch in chs]
        gbc = [pltpu.roll(gb_ref[r, :], shift, 1) for r in rows]
        cumc = [pltpu.roll(cum_ref[r, :], shift, 1) for r in rows]
        cumtc = [cumt_ref[:, r] for r in rows]
        qs, ks, vs, aas = {}, {}, {}, {}
        for j in range(n_prep):
            for hl in range(hg):
                cols = slice(hl * hd, (hl + 1) * hd)
                qs[j, hl] = q_s[rows[j], cols]
                ks[j, hl] = k_s[rows[j], cols]
                vs[j, hl] = v_s[rows[j], cols]
                qk = jnp.concatenate([qs[j, hl], ks[j, hl]], axis=0).astype(BF16)
                aas[j, hl] = lax.dot_general(qk, ks[j, hl].astype(BF16), (((1,), (1,)), ((), ())),
                                             preferred_element_type=F32)
        probs = [(j, hl, d) for j in range(n_prep) for hl in range(hg) for d in range(N_DIR)]
        nms, betas, cums, decays = [], [], [], []
        for j, hl, d in probs:
            incl, strict = masks[d]
            base = hl * GATE_SLOTS
            beta_b = jnp.broadcast_to(gbc[j][:, base + N_DIR + d:base + N_DIR + d + 1], (c, hd))
            cum_b = jnp.broadcast_to(cumc[j][:, base + d:base + d + 1], (c, hd))
            cum_r = cumtc[j][base + d:base + d + 1, :]
            decay = jnp.where(incl, jnp.exp(jnp.where(incl, cum_b - cum_r, 0.0)), 0.0)
            nms.append(jnp.where(strict, -(beta_b * aas[j, hl][c:2 * c] * decay), 0.0))
            betas.append(beta_b)
            cums.append(cum_b)
            decays.append(decay)
        ms = tri_inverse(nms, [d == 0 for _, _, d in probs])
        e_cs = [jnp.exp(cum_b) for cum_b in cums]
        uws = [bdot(m, jnp.concatenate([vs[j, hl] * beta_b, ks[j, hl] * (beta_b * e_c)], axis=1))
               for (j, hl, d), m, beta_b, e_c in zip(probs, ms, betas, e_cs)]
        for i, (j, hl, d) in enumerate(probs):
            cum_b, e_c, uw = cums[i], e_cs[i], uws[i]
            gl_b = cum_b[c - 1:c, :] if d == 0 else cum_b[0:1, :]
            kt = ks[j, hl] * jnp.exp(gl_b - cum_b)
            slot = (hl * N_DIR + d) * nc + chs[j]
            u_s[pl.ds(pl.multiple_of(slot * c, c), c), :] = uw[:, 0:hd]
            wq_s[pl.ds(pl.multiple_of(slot * 2 * c, c), c), :] = uw[:, hd:2 * hd].astype(BF16)
            wq_s[pl.ds(pl.multiple_of(slot * 2 * c + c, c), c), :] = (qs[j, hl] * e_c).astype(BF16)
            ktt_s[pl.ds(pl.multiple_of(slot * hd, hd), hd), :] = kt.T.astype(BF16)
            at_s[pl.ds(pl.multiple_of(slot * c, c), c), :] = (aas[j, hl][0:c] * decays[i]).astype(BF16)
            egl_s[pl.ds(pl.multiple_of(slot * 8, 8), 8), :] = jnp.broadcast_to(jnp.exp(gl_b), (8, hd))
        return carry

    lax.fori_loop(0, nc // n_prep, prep, 0)

    st_s[...] = jnp.zeros(st_s.shape, F32)
    o_s[...] = jnp.zeros(o_s.shape, F32)

    def scan_step(ch_f, ch_b, with_q):
        chains = [(hl, d) for hl in range(hg) for d in range(N_DIR)]
        chs = [ch_f if d == 0 else ch_b for _, d in chains]
        slots = [i * nc + ch for i, ch in enumerate(chs)]
        s_olds = [st_s[i * hd:(i + 1) * hd, :] for i in range(len(chains))]
        n_lhs = 2 * c if with_q else c
        rs = [jnp.dot(wq_s[pl.ds(pl.multiple_of(slot * 2 * c, 2 * c), n_lhs), :], s_old.astype(BF16),
                      preferred_element_type=F32) for slot, s_old in zip(slots, s_olds)]
        v_bs = [(u_s[pl.ds(pl.multiple_of(slot * c, c), c), :] - r_[0:c]).astype(BF16)
                for slot, r_ in zip(slots, rs)]
        kvs = [jnp.dot(ktt_s[pl.ds(pl.multiple_of(slot * hd, hd), hd), :], v_b, preferred_element_type=F32)
               for slot, v_b in zip(slots, v_bs)]
        if with_q:
            avs = [jnp.dot(at_s[pl.ds(pl.multiple_of(slot * c, c), c), :], v_b, preferred_element_type=F32)
                   for slot, v_b in zip(slots, v_bs)]
        for i, (hl, d) in enumerate(chains):
            egl = egl_s[pl.ds(pl.multiple_of(slots[i] * 8, 8), 1), :]
            st_s[i * hd:(i + 1) * hd, :] = s_olds[i] * egl + kvs[i]
            if with_q:
                orow = pl.ds(pl.multiple_of(chs[i] * c, c), c)
                ocol = slice(hl * hd, (hl + 1) * hd)
                o_s[orow, ocol] = o_s[orow, ocol] + (rs[i][c:2 * c] + avs[i])

    def scan_h(it, carry):
        scan_step(it, nc_h - 1 - it, ctx_out)
        return carry

    def scan_x(it, carry):
        scan_step(nc_h + it, nc - 1 - it, True)
        return carry

    lax.fori_loop(0, nc_h, scan_h, 0)
    lax.fori_loop(0, nc_x, scan_x, 0)

    onw = onorm_ref[...]

    def finish(z_ref, out_ref, row0, n_rows):
        tile = 256 if n_rows % 256 == 0 else 128

        def body(r, carry):
            start = pl.multiple_of(r * tile, tile)
            src = pl.ds(pl.multiple_of(row0 + start, CHUNK), tile)
            dst = pl.ds(start, tile)
            outs = []
            for hl in range(hg):
                cols = slice(hl * hd, (hl + 1) * hd)
                o = o_s[src, cols]
                outs.append(_rms(o) * onw * _silu(z_ref[dst, cols].astype(F32)))
            y = outs[0] if hg == 1 else jnp.concatenate(outs, axis=1)
            out_ref[dst, :] = y.astype(out_ref.dtype)
            return carry

        lax.fori_loop(0, n_rows // tile, body, 0)

    finish(zx_ref, ox_ref, seq_h, seq_x)
    if ctx_out:
        finish(zh_ref, oh_ref, 0, seq_h)


def _delta(pxm, phm, gb, cum, cumt, conv_w, onorm, n_heads, ctx_out):
    nb, seq_x, _ = pxm.shape
    seq_h = phm.shape[1]
    hg = HEADS_PER_STEP
    hd = HEAD_DIM
    width = hg * hd
    n_groups = n_heads // hg
    lin_w = n_heads * hd
    seq_t = seq_x + seq_h
    nc = seq_t // CHUNK
    n_chain = hg * N_DIR
    n_gt = hg * GATE_SLOTS

    def main_spec(rows, part):
        return pl.BlockSpec((None, rows, width), lambda b, g: (b, 0, part * n_groups + g))

    in_specs = (
        [main_spec(seq_x, p) for p in range(4)]
        + [main_spec(seq_h, p) for p in range(4)]
        + [pl.BlockSpec((CONV_K, width), lambda b, g, p=p: (0, p * n_groups + g)) for p in range(3)]
        + [pl.BlockSpec((None, seq_t, 128), lambda b, g: (b, 0, 0)),
           pl.BlockSpec((None, seq_t, 128), lambda b, g: (b, 0, 0)),
           pl.BlockSpec((None, n_gt, seq_t), lambda b, g: (b, g, 0)),
           pl.BlockSpec((1, hd), lambda b, g: (0, 0))]
    )
    out_shape = [jax.ShapeDtypeStruct((nb, seq_x, lin_w), BF16)]
    out_specs = [pl.BlockSpec((None, seq_x, width), lambda b, g: (b, 0, g))]
    if ctx_out:
        out_shape.append(jax.ShapeDtypeStruct((nb, seq_h, lin_w), BF16))
        out_specs.append(pl.BlockSpec((None, seq_h, width), lambda b, g: (b, 0, g)))
    scratch = [
        pltpu.VMEM((seq_x + 16, width), F32),
        pltpu.VMEM((seq_t, width), F32),
        pltpu.VMEM((seq_t, width), F32),
        pltpu.VMEM((seq_t, width), F32),
        pltpu.VMEM((seq_t, width), F32),
        pltpu.VMEM((n_chain * nc * CHUNK, hd), F32),
        pltpu.VMEM((n_chain * nc * 2 * CHUNK, hd), BF16),
        pltpu.VMEM((n_chain * nc * hd, CHUNK), BF16),
        pltpu.VMEM((n_chain * nc * CHUNK, CHUNK), BF16),
        pltpu.VMEM((n_chain * nc * 8, hd), F32),
        pltpu.VMEM((n_chain * hd, hd), F32),
    ]
    args = [pxm] * 4 + [phm] * 4 + [conv_w] * 3 + [gb, cum, cumt, onorm]
    res = pl.pallas_call(
        functools.partial(_delta_kernel, seq_x=seq_x, seq_h=seq_h, hg=hg, ctx_out=ctx_out),
        out_shape=tuple(out_shape),
        grid=(nb, n_groups),
        in_specs=in_specs,
        out_specs=tuple(out_specs),
        scratch_shapes=scratch,
        compiler_params=_cparams(2),
        name="gated_delta",
    )(*args)
    return res if ctx_out else (res[0], None)


def _pool_kernel(p_ref, w_ref, sc_ref, o_ref, y_s, *, on_grid, n_groups, gw):
    n = p_ref.shape[0]
    blk = 2 * GRID_W if on_grid else n
    period = GRID_W if on_grid else n
    ri = lax.broadcasted_iota(jnp.int32, (blk, blk), 0)
    ci = lax.broadcasted_iota(jnp.int32, (blk, blk), 1)
    pos = lax.broadcasted_iota(jnp.int32, (blk, gw), 0) % period

    for g in range(n_groups):
        size = POOL_SIZES[g]
        half = size // 2
        cols = slice(g * gw, (g + 1) * gw)
        if on_grid:
            n_rows = n // GRID_W
            for r in range(n_rows):
                lo = max(r - half, 0)
                hi = min(r + size - 1 - half, n_rows - 1)
                acc = p_ref[lo * GRID_W:(lo + 1) * GRID_W, cols].astype(F32)
                for rr in range(lo + 1, hi + 1):
                    acc = acc + p_ref[rr * GRID_W:(rr + 1) * GRID_W, cols].astype(F32)
                y_s[r * GRID_W:(r + 1) * GRID_W, :] = acc / float(hi - lo + 1)
            src = y_s
            src_cols = slice(0, gw)
        else:
            src = p_ref
            src_cols = cols
        band = ((ci >= ri - half) & (ci <= ri + size - 1 - half) & (ci // period == ri // period))
        band = jnp.where(band, 1.0, 0.0).astype(BF16)
        lo_c = jnp.maximum(pos - half, 0)
        hi_c = jnp.minimum(pos + size - 1 - half, period - 1)
        cnt = (hi_c - lo_c + 1).astype(F32)
        w_g = w_ref[g]
        scale = sc_ref[:, cols]

        n_blk = n // blk
        group = POOL_BLOCKS if n_blk % POOL_BLOCKS == 0 else 1

        def body(t, carry):
            rows = [pl.ds(pl.multiple_of((t * group + j) * blk, blk), blk) for j in range(group)]
            pieces = []
            for r in rows:
                y = src[r, src_cols].astype(F32)
                hi_p = y.astype(BF16)
                r1 = y - hi_p.astype(F32)
                mid_p = r1.astype(BF16)
                lo_p = (r1 - mid_p.astype(F32)).astype(BF16)
                pieces.append(jnp.concatenate([hi_p, mid_p, lo_p], axis=1))
            tots = [jnp.dot(band, pc, preferred_element_type=F32) for pc in pieces]
            dlts = []
            for r, tot in zip(rows, tots):
                mean = (tot[:, 0:gw] + tot[:, gw:2 * gw] + tot[:, 2 * gw:3 * gw]) / cnt
                dlts.append((mean - p_ref[r, cols].astype(F32)).astype(BF16))
            outs = [jnp.dot(dlt, w_g, preferred_element_type=F32) for dlt in dlts]
            for r, out in zip(rows, outs):
                o_ref[r, cols] = (out * scale).astype(o_ref.dtype)
            return carry

        lax.fori_loop(0, n_blk // group, body, 0)


def _pool(pm, pool_w, pool_scale, part0, on_grid):
    nb, rows, _ = pm.shape
    n_groups, gw, _ = pool_w.shape
    pw = n_groups * gw
    return pl.pallas_call(
        functools.partial(_pool_kernel, on_grid=on_grid, n_groups=n_groups, gw=gw),
        out_shape=jax.ShapeDtypeStruct((nb, rows, pw), BF16),
        grid=(nb,),
        in_specs=[pl.BlockSpec((None, rows, pw), lambda b: (b, 0, part0)),
                  pl.BlockSpec((n_groups, gw, gw), lambda b: (0, 0, 0)),
                  pl.BlockSpec((1, pw), lambda b: (0, 0))],
        out_specs=pl.BlockSpec((None, rows, pw), lambda b: (b, 0, 0)),
        scratch_shapes=[pltpu.VMEM((rows, gw), F32)],
        compiler_params=_cparams(1),
        name="pool_mixer",
    )(pm, pool_w, pool_scale)


def _outproj_kernel(lin_ref, pool_ref, x_ref, mod_ref, wa_ref, wb_ref, o_ref):
    y = (jnp.dot(lin_ref[...], wa_ref[...], preferred_element_type=F32)
         + jnp.dot(pool_ref[...], wb_ref[...], preferred_element_type=F32))
    o_ref[...] = x_ref[...] + mod_ref[2:3, :] * y


def _outproj(lin, pool, x3, mod, mod_base, w_a, w_b):
    nb, rows, d = x3.shape
    tm = min(512, rows)
    ka = w_a.shape[0]
    kb = w_b.shape[0]
    return pl.pallas_call(
        _outproj_kernel,
        out_shape=jax.ShapeDtypeStruct((nb, rows, d), F32),
        grid=(nb, rows // tm),
        in_specs=[pl.BlockSpec((None, tm, ka), lambda b, i: (b, i, 0)),
                  pl.BlockSpec((None, tm, kb), lambda b, i: (b, i, 0)),
                  pl.BlockSpec((None, tm, d), lambda b, i: (b, i, 0)),
                  pl.BlockSpec((None, N_MOD, d), lambda b, i: (mod_base + b, 0, 0)),
                  pl.BlockSpec((ka, d), lambda b, i: (0, 0)),
                  pl.BlockSpec((kb, d), lambda b, i: (0, 0))],
        out_specs=pl.BlockSpec((None, tm, d), lambda b, i: (b, i, 0)),
        compiler_params=_cparams(2),
        name="out_proj",
    )(lin, pool, x3, mod, w_a, w_b)


def _mlp_kernel(x_ref, mod_ref, nw_ref, wu_ref, wd_ref, fw_ref, o_ref, xn_ref, *, final_norm):
    f = pl.program_id(2)

    @pl.when(f == 0)
    def _():
        _norm_mod_to(x_ref, nw_ref, mod_ref, 3, 4, xn_ref)

    @pl.when(f == 0)
    def _():
        o_ref[...] = jnp.zeros(o_ref.shape, F32)

    wu = wu_ref[...]
    wd = wd_ref[...]
    slabs = range(0, xn_ref.shape[0], ROW_SLAB)
    hs = [jnp.dot(xn_ref[r:r + ROW_SLAB, :], wu, preferred_element_type=F32) for r in slabs]
    hs = [jnp.square(jnp.maximum(h, 0.0)).astype(BF16) for h in hs]
    for r, h in zip(slabs, hs):
        o_ref[r:r + ROW_SLAB, :] += jnp.dot(h, wd, preferred_element_type=F32)

    @pl.when(f == pl.num_programs(2) - 1)
    def _():
        gate = mod_ref[5:6, :]
        fw = fw_ref[...]
        rows = 128

        def body(r, carry):
            sl = pl.ds(pl.multiple_of(r * rows, rows), rows)
            y = x_ref[sl, :] + gate * o_ref[sl, :]
            if final_norm:
                y = _rms(y) * fw
            o_ref[sl, :] = y
            return carry

        lax.fori_loop(0, x_ref.shape[0] // rows, body, 0)


def _mlp(x3, mod, mod_base, nw, w_up, w_down, final_w, final_norm):
    nb, rows, d = x3.shape
    dff = w_up.shape[1]
    tm = min(1024, rows)
    tf = 512
    return pl.pallas_call(
        functools.partial(_mlp_kernel, final_norm=final_norm),
        out_shape=jax.ShapeDtypeStruct((nb, rows, d), F32),
        grid=(nb, rows // tm, dff // tf),
        in_specs=[pl.BlockSpec((None, tm, d), lambda b, i, f: (b, i, 0)),
                  pl.BlockSpec((None, N_MOD, d), lambda b, i, f: (mod_base + b, 0, 0)),
                  pl.BlockSpec((1, d), lambda b, i, f: (0, 0)),
                  pl.BlockSpec((d, tf), lambda b, i, f: (0, f)),
                  pl.BlockSpec((tf, d), lambda b, i, f: (f, 0)),
                  pl.BlockSpec((1, d), lambda b, i, f: (0, 0))],
        out_specs=pl.BlockSpec((None, tm, d), lambda b, i, f: (b, i, 0)),
        scratch_shapes=[pltpu.VMEM((tm, d), BF16)],
        compiler_params=_cparams(3),
        name="mlp",
    )(x3, mod, nw, w_up, w_down, final_w)


def _gate_layouts(w_in_l, a_log_l, dt_bias_l, lin_w, n_heads):
    d = w_in_l.shape[0]
    decay_start = 4 * lin_w
    ab = w_in_l[:, decay_start:decay_start + 2 * N_DIR * n_heads]
    ab = ab.reshape(d, 2 * N_DIR, n_heads)
    ab = jnp.transpose(ab, (0, 2, 1))
    ab = jnp.pad(ab, ((0, 0), (0, 0), (0, GATE_SLOTS - 2 * N_DIR)))
    w_ab = ab.reshape(d, n_heads * GATE_SLOTS)
    w_ab = jnp.pad(w_ab, ((0, 0), (0, 128 - n_heads * GATE_SLOTS))).astype(BF16)

    def per_head(p):
        return jnp.pad(p.T.astype(F32), ((0, 0), (0, GATE_SLOTS - N_DIR)))

    alog = per_head(a_log_l)
    dtb = per_head(dt_bias_l)
    alog_c = jnp.pad(alog.reshape(1, n_heads * GATE_SLOTS), ((0, 0), (0, 128 - n_heads * GATE_SLOTS)))
    dtb_c = jnp.pad(dtb.reshape(1, n_heads * GATE_SLOTS), ((0, 0), (0, 128 - n_heads * GATE_SLOTS)))
    alog_t = jnp.broadcast_to(alog.reshape(n_heads * GATE_SLOTS, 1), (n_heads * GATE_SLOTS, 128))
    dtb_t = jnp.broadcast_to(dtb.reshape(n_heads * GATE_SLOTS, 1), (n_heads * GATE_SLOTS, 128))
    return w_ab, alog_c, dtb_c, alog_t, dtb_t


def kernel(x, c, ctx, c_ctx, w_ada, b_ada, norm1_w, norm2_w, w_in, conv_w, a_log, dt_bias, onorm_w,
           pool_w, pool_scale, w_out, w_up, w_down, final_norm_w):
    nb, seq_x, d = x.shape
    seq_h = ctx.shape[1]
    depth = w_ada.shape[0]
    n_heads = a_log.shape[2]
    lin_w = n_heads * HEAD_DIM
    pool_width = pool_scale.shape[1]
    pool_start = 4 * lin_w + 2 * N_DIR * n_heads
    assert nb + 1 <= MOD_ROWS and seq_x % CHUNK == 0 and seq_h % CHUNK == 0
    assert n_heads % HEADS_PER_STEP == 0 and seq_x % GRID_W == 0

    c_all = jnp.concatenate([c, c_ctx[None, :], jnp.zeros((MOD_ROWS - nb - 1, d), F32)], axis=0)
    mod_all = _ada(c_all, w_ada, b_ada).reshape(depth, MOD_ROWS, N_MOD, d)

    h = ctx
    for l in range(depth):
        ctx_out = l < depth - 1
        mod = mod_all[l]
        w_main = jnp.concatenate([w_in[l, :, :4 * lin_w], w_in[l, :, pool_start:]], axis=1).astype(BF16)
        w_ab, alog_c, dtb_c, alog_t, dtb_t = _gate_layouts(w_in[l], a_log[l], dt_bias[l], lin_w, n_heads)
        n_gate_rows = n_heads * GATE_SLOTS
        nw1 = norm1_w[l].reshape(1, d)
        nw2 = norm2_w[l].reshape(1, d)
        w_oa = w_out[l, :lin_w].astype(BF16)
        w_ob = w_out[l, lin_w:].astype(BF16)
        w_u = w_up[l].astype(BF16)
        w_d = w_down[l].astype(BF16)
        pw = pool_w[l].astype(BF16)
        ps = pool_scale[l].reshape(1, pool_width)
        fw = final_norm_w.reshape(1, d)

        pxm, abx, abtx = _inproj(x, mod, 0, nw1, w_main, w_ab, n_gate_rows)
        h3 = h.reshape(1, nb * seq_h, d)
        phm, abh, abth = _inproj(h3, mod, nb, nw1, w_main, w_ab, n_gate_rows)
        phm = phm.reshape(nb, seq_h, -1)
        abh = abh.reshape(nb, seq_h, 128)
        abth = jnp.transpose(abth.reshape(-1, nb, seq_h), (1, 0, 2))

        gb, cum, cumt = _gates(abx, abtx, abh, abth, alog_c, dtb_c, alog_t, dtb_t)
        lin_x, lin_h = _delta(pxm, phm, gb, cum, cumt, conv_w[l], onorm_w[l].reshape(1, HEAD_DIM),
                              n_heads, ctx_out)
        part0 = (4 * lin_w) // pool_width
        pool_x = _pool(pxm, pw, ps, part0, True)
        x = _outproj(lin_x, pool_x, x, mod, 0, w_oa, w_ob)
        x = _mlp(x, mod, 0, nw2, w_u, w_d, fw, l == depth - 1)
        if ctx_out:
            pool_h = _pool(phm, pw, ps, part0, False)
            h3 = _outproj(lin_h.reshape(1, nb * seq_h, lin_w), pool_h.reshape(1, nb * seq_h, pool_width),
                          h3, mod, nb, w_oa, w_ob)
            h = _mlp(h3, mod, nb, nw2, w_u, w_d, fw, False).reshape(nb, seq_h, d)
    return x
```

```python
import functools
import math

import jax
import jax.numpy as jnp
from jax import lax
from jax.experimental import pallas as pl
from jax.experimental.pallas import tpu as pltpu

F32 = jnp.float32
BF16 = jnp.bfloat16

EPS = 1e-6
N_MOD = 6
HEAD_DIM = 128
N_DIR = 2
CONV_K = 5
GRID_W = 64
POOL_SIZES = (2, 4, 8, 16)
CHUNK = 128
INV_BLOCK = 16
GATE_SLOTS = 8
HEADS_PER_STEP = 2
CONV_PHASE = 4
POOL_BLOCKS = 4
PREP_CHUNKS = 3
ROW_SLAB = 256
MOD_ROWS = 24
VMEM_LIMIT = 56 * 1024 * 1024


def _silu(x):
    return x * (1.0 / (1.0 + jnp.exp(-x)))


def _softplus(x):
    return jnp.maximum(x, 0.0) + jnp.log1p(jnp.exp(-jnp.abs(x)))


def _cparams(n_axes):
    return pltpu.CompilerParams(dimension_semantics=("arbitrary",) * n_axes,
                                vmem_limit_bytes=VMEM_LIMIT)


def _ada_kernel(c_ref, w_ref, b_ref, o_ref):
    sc = _silu(c_ref[...]).astype(BF16)
    o_ref[...] = jnp.dot(sc, w_ref[...].astype(BF16), preferred_element_type=F32) + b_ref[...]


def _ada(c_all, w_ada, b_ada):
    depth, d, n = w_ada.shape
    tn = 1024
    return pl.pallas_call(
        _ada_kernel,
        out_shape=jax.ShapeDtypeStruct((depth, MOD_ROWS, n), F32),
        grid=(depth, n // tn),
        in_specs=[pl.BlockSpec((MOD_ROWS, d), lambda l, j: (0, 0)),
                  pl.BlockSpec((None, d, tn), lambda l, j: (l, 0, j)),
                  pl.BlockSpec((None, 1, tn), lambda l, j: (l, 0, j))],
        out_specs=pl.BlockSpec((None, MOD_ROWS, tn), lambda l, j: (l, 0, j)),
        compiler_params=_cparams(2),
        name="ada_mod",
    )(c_all, w_ada, b_ada.reshape(depth, 1, n))


def _rms(x):
    return x * lax.rsqrt(jnp.mean(x * x, axis=-1, keepdims=True) + EPS)


def _norm_mod_to(x_ref, nw_ref, mod_ref, i_shift, i_scale, out_ref, rows=128):
    nw = nw_ref[...]
    shift = mod_ref[i_shift:i_shift + 1, :]
    scale = mod_ref[i_scale:i_scale + 1, :]

    def body(r, carry):
        sl = pl.ds(pl.multiple_of(r * rows, rows), rows)
        y = _rms(x_ref[sl, :]) * nw
        out_ref[sl, :] = (y * (1.0 + scale) + shift).astype(out_ref.dtype)
        return carry

    lax.fori_loop(0, x_ref.shape[0] // rows, body, 0)


def _inproj_kernel(x_ref, mod_ref, nw_ref, w_ref, wab_ref, o_ref, ab_ref, abt_ref, xn_a, xn_b):
    s = pl.program_id(0)
    j = pl.program_id(1)
    tm = x_ref.shape[0]
    n_gate_rows = abt_ref.shape[0]

    @pl.when((s == 0) & (j == 0))
    def _():
        xn_b[...] = jnp.zeros(xn_b.shape, xn_b.dtype)

    def step(xn_fill, xn_use):
        @pl.when(j == 0)
        def _():
            ab = jnp.dot(xn_use[...], wab_ref[...], preferred_element_type=F32)
            ab_ref[...] = ab
            abt_ref[...] = ab.T[0:n_gate_rows, :]

        nw = nw_ref[...]
        shift = mod_ref[0:1, :]
        scale = mod_ref[1:2, :]
        n_parts = pl.num_programs(1) - 1
        part_rows = tm // 4
        row0 = jnp.minimum(j, n_parts - 1) * part_rows
        for r in range(0, part_rows, 128):
            sl = pl.ds(pl.multiple_of(row0 + r, 128), 128)
            y = _rms(x_ref[sl, :]) * nw
            xn_fill[sl, :] = (y * (1.0 + scale) + shift).astype(BF16)

        w = w_ref[...]
        for r in range(0, tm, ROW_SLAB):
            o_ref[r:r + ROW_SLAB, :] = jnp.dot(xn_use[r:r + ROW_SLAB, :], w,
                                               preferred_element_type=F32).astype(o_ref.dtype)

    @pl.when(s % 2 == 0)
    def _():
        step(xn_a, xn_b)

    @pl.when(s % 2 == 1)
    def _():
        step(xn_b, xn_a)


def _inproj(x3, mod, mod_base, nw, w_main, w_ab, n_gate_rows):
    nb, rows, d = x3.shape
    n = w_main.shape[1]
    tm = min(1024, rows)
    tn = 1024
    tpb = rows // tm
    n_tiles = nb * tpb
    assert n // tn == 5 and tm % 512 == 0

    def tile_in(s):
        t = jnp.minimum(s, n_tiles - 1)
        return t // tpb, t % tpb

    def tile_out(s):
        t = jnp.maximum(s - 1, 0)
        return t // tpb, t % tpb

    return pl.pallas_call(
        _inproj_kernel,
        out_shape=(jax.ShapeDtypeStruct((nb, rows, n), BF16),
                   jax.ShapeDtypeStruct((nb, rows, 128), F32),
                   jax.ShapeDtypeStruct((nb, n_gate_rows, rows), F32)),
        grid=(n_tiles + 1, n // tn),
        in_specs=[pl.BlockSpec((None, tm, d), lambda s, j: (*tile_in(s), 0)),
                  pl.BlockSpec((None, N_MOD, d), lambda s, j: (mod_base + tile_in(s)[0], 0, 0)),
                  pl.BlockSpec((1, d), lambda s, j: (0, 0)),
                  pl.BlockSpec((d, tn), lambda s, j: (0, j)),
                  pl.BlockSpec((d, 128), lambda s, j: (0, 0))],
        out_specs=(pl.BlockSpec((None, tm, tn), lambda s, j: (*tile_out(s), jnp.where(s == 0, 0, j))),
                   pl.BlockSpec((None, tm, 128), lambda s, j: (*tile_out(s), 0)),
                   pl.BlockSpec((None, n_gate_rows, tm), lambda s, j: (tile_out(s)[0], 0, tile_out(s)[1]))),
        scratch_shapes=[pltpu.VMEM((tm, d), BF16), pltpu.VMEM((tm, d), BF16)],
        compiler_params=_cparams(2),
        name="in_proj",
    )(x3, mod, nw, w_main, w_ab)


def _split3(x):
    hi = x.astype(BF16)
    r1 = x - hi.astype(F32)
    mid = r1.astype(BF16)
    lo = (r1 - mid.astype(F32)).astype(BF16)
    return hi, mid, lo


def _gate_kernel(abx_ref, abtx_ref, abh_ref, abth_ref, alog_ref, dtb_ref, alogt_ref, dtbt_ref,
                 gb_ref, cum_ref, cumt_ref, *, seq_x, seq_h):
    c = CHUNK
    n_gt = abtx_ref.shape[0]
    slot_l = lax.broadcasted_iota(jnp.int32, (c, 128), 1) % GATE_SLOTS
    row_slot = lax.broadcasted_iota(jnp.int32, (n_gt, c), 0) % GATE_SLOTS
    ri = lax.broadcasted_iota(jnp.int32, (c, c), 0)
    ci = lax.broadcasted_iota(jnp.int32, (c, c), 1)
    m_le = jnp.where(ci <= ri, 1.0, 0.0).astype(BF16)
    m_ge = jnp.where(ci >= ri, 1.0, 0.0).astype(BF16)
    alog_r = alog_ref[...]
    dtb_r = dtb_ref[...]
    alogt = alogt_ref[...]
    dtbt = dtbt_ref[...]

    def part(ab_ref, abt_ref, row0, n_rows):
        def body(r, carry):
            src = pl.ds(pl.multiple_of(r * c, c), c)
            dst = pl.ds(pl.multiple_of(row0 + r * c, c), c)
            a = ab_ref[src, :]
            g = -jnp.exp(alog_r) * _softplus(a + dtb_r)
            gb = jnp.where(slot_l < N_DIR, g, 1.0 / (1.0 + jnp.exp(-a)))
            gb_ref[dst, :] = gb
            pieces = jnp.concatenate(_split3(gb), axis=1)
            pre = jnp.dot(m_le, pieces, preferred_element_type=F32)
            suf = jnp.dot(m_ge, pieces, preferred_element_type=F32)
            pre = pre[:, 0:128] + pre[:, 128:256] + pre[:, 256:384]
            suf = suf[:, 0:128] + suf[:, 128:256] + suf[:, 256:384]
            cum_ref[dst, :] = jnp.where(slot_l == 0, pre, suf)
            gt = -jnp.exp(alogt) * _softplus(abt_ref[:, src] + dtbt)
            pieces_t = jnp.concatenate(_split3(gt), axis=0)
            pre_t = jnp.dot(pieces_t, m_ge, preferred_element_type=F32)
            suf_t = jnp.dot(pieces_t, m_le, preferred_element_type=F32)
            pre_t = pre_t[0:n_gt] + pre_t[n_gt:2 * n_gt] + pre_t[2 * n_gt:3 * n_gt]
            suf_t = suf_t[0:n_gt] + suf_t[n_gt:2 * n_gt] + suf_t[2 * n_gt:3 * n_gt]
            cumt_ref[:, dst] = jnp.where(row_slot == 0, pre_t, suf_t)
            return carry

        lax.fori_loop(0, n_rows // c, body, 0, unroll=2)

    part(abh_ref, abth_ref, 0, seq_h)
    part(abx_ref, abtx_ref, seq_h, seq_x)


def _gates(abx, abtx, abh, abth, alog_c, dtb_c, alog_t, dtb_t):
    nb, seq_x, _ = abx.shape
    seq_h = abh.shape[1]
    n_gt = abtx.shape[1]
    seq_t = seq_x + seq_h
    return pl.pallas_call(
        functools.partial(_gate_kernel, seq_x=seq_x, seq_h=seq_h),
        out_shape=(jax.ShapeDtypeStruct((nb, seq_t, 128), F32),
                   jax.ShapeDtypeStruct((nb, seq_t, 128), F32),
                   jax.ShapeDtypeStruct((nb, n_gt, seq_t), F32)),
        grid=(nb,),
        in_specs=[pl.BlockSpec((None, seq_x, 128), lambda b: (b, 0, 0)),
                  pl.BlockSpec((None, n_gt, seq_x), lambda b: (b, 0, 0)),
                  pl.BlockSpec((None, seq_h, 128), lambda b: (b, 0, 0)),
                  pl.BlockSpec((None, n_gt, seq_h), lambda b: (b, 0, 0)),
                  pl.BlockSpec((1, 128), lambda b: (0, 0)),
                  pl.BlockSpec((1, 128), lambda b: (0, 0)),
                  pl.BlockSpec((n_gt, 128), lambda b: (0, 0)),
                  pl.BlockSpec((n_gt, 128), lambda b: (0, 0))],
        out_specs=(pl.BlockSpec((None, seq_t, 128), lambda b: (b, 0, 0)),
                   pl.BlockSpec((None, seq_t, 128), lambda b: (b, 0, 0)),
                   pl.BlockSpec((None, n_gt, seq_t), lambda b: (b, 0, 0))),
        compiler_params=_cparams(1),
        name="gates",
    )(abx, abtx, abh, abth, alog_c, dtb_c, alog_t, dtb_t)


def _delta_kernel(*refs, seq_x, seq_h, hg, ctx_out):
    n_in = 14
    (qx_ref, kx_ref, vx_ref, zx_ref, qh_ref, kh_ref, vh_ref, zh_ref,
     cq_ref, ck_ref, cv_ref, gb_ref, cum_ref, cumt_ref) = refs[:n_in]
    onorm_ref = refs[n_in]
    if ctx_out:
        ox_ref, oh_ref = refs[n_in + 1:n_in + 3]
        scratch = refs[n_in + 3:]
    else:
        ox_ref = refs[n_in + 1]
        oh_ref = None
        scratch = refs[n_in + 2:]
    (padq_s, padk_s, padv_s, q_s, k_s, v_s, o_s, u_s, wq_s, ktt_s, at_s, egl_s, st_s) = scratch

    c = CHUNK
    hd = HEAD_DIM
    width = hg * hd
    seq_t = seq_h + seq_x
    nc_h = seq_h // c
    nc_x = seq_x // c
    nc = nc_h + nc_x
    n_chain = hg * N_DIR

    zero8 = jnp.zeros((8, hd), F32)
    for pad, h_ref, x_ref in ((padq_s, qh_ref, qx_ref), (padk_s, kh_ref, kx_ref), (padv_s, vh_ref, vx_ref)):
        for hl in range(hg):
            cols = slice(hl * hd, (hl + 1) * hd)
            pad[hl, 0:8, :] = zero8
            pad[hl, 8:8 + seq_h, :] = h_ref[:, cols].astype(F32)
            pad[hl, 8 + seq_h:16 + seq_h, :] = zero8
            pad[hl, 16 + seq_h:16 + seq_t, :] = x_ref[:, cols].astype(F32)
            pad[hl, 16 + seq_t:24 + seq_t, :] = zero8
    conv_taps = [[[w_ref[i:i + 1, hl * hd:(hl + 1) * hd] for i in range(CONV_K)] for hl in range(hg)]
                 for w_ref in (cq_ref, ck_ref, cv_ref)]

    def conv_chunk(ch):
        base = ch * c + jnp.where(ch >= nc_h, 16, 8)
        n_ph = c // CONV_PHASE
        for pad, taps, dst, mode in ((padq_s, conv_taps[0], q_s, "q"), (padk_s, conv_taps[1], k_s, "k"),
                                     (padv_s, conv_taps[2], v_s, "v")):
            for hl in range(hg):
                shifted = {off: pad[hl, pl.ds(base + off, n_ph, stride=CONV_PHASE), :]
                           for off in range(-(CONV_K // 2), CONV_PHASE + CONV_K // 2)}
                for ph in range(CONV_PHASE):
                    acc = None
                    for i in range(CONV_K):
                        term = shifted[ph + i - CONV_K // 2] * taps[hl][i]
                        acc = term if acc is None else acc + term
                    y = _silu(acc)
                    if mode != "v":
                        y = y * lax.rsqrt(jnp.sum(y * y, axis=-1, keepdims=True) + EPS)
                        if mode == "q":
                            y = y * (hd ** -0.5)
                    dst[hl, pl.ds(ch * c + ph, n_ph, stride=CONV_PHASE), :] = y

    shift = (128 - pl.program_id(1) * (hg * GATE_SLOTS)) % 128
    ri = lax.broadcasted_iota(jnp.int32, (c, c), 0)
    ci = lax.broadcasted_iota(jnp.int32, (c, c), 1)
    lower_incl = (ci <= ri)
    upper_incl = (ci >= ri)

    eye = jnp.where(ri == ci, 1.0, 0.0).astype(F32)
    masks = ((lower_incl, ci < ri), (upper_incl, ci > ri))

    def bdot(a, b):
        return jnp.dot(a.astype(BF16), b.astype(BF16), preferred_element_type=F32)

    same_blk = (ri // INV_BLOCK) == (ci // INV_BLOCK)

    def tri_inverse(nms, lowers):
        nds = [jnp.where(same_blk, nm, 0.0) for nm in nms]
        ms = [eye + nd for nd in nds]
        ps = [bdot(nd, nd) for nd in nds]
        for _ in range(int(math.log2(INV_BLOCK)) - 2):
            rs = [bdot(p, jnp.concatenate([m, p], axis=1)) for m, p in zip(ms, ps)]
            ms = [m + r_[:, 0:c] for m, r_ in zip(ms, rs)]
            ps = [r_[:, c:2 * c] for r_ in rs]
        rs = [bdot(p, m) for m, p in zip(ms, ps)]
        ms = [m + r_ for m, r_ in zip(ms, rs)]
        s = INV_BLOCK
        while s < c:
            starts = list(range(0, c, s))
            pair = ((ri // (2 * s)) == (ci // (2 * s))) & ((ri // s) != (ci // s))
            zero = jnp.zeros((s, c), BF16)
            hits = [[(a // s) % 2 == (1 if lower else 0) for a in starts] for lower in lowers]

            def take(xm, hit):
                return jnp.concatenate([xm[a:a + s] for a, t in zip(starts, hit) if t], axis=0)

            m_sels = [take(m, hit) for m, hit in zip(ms, hits)]
            xs = [bdot(take(jnp.where(pair, nm, 0.0), hit), m).astype(BF16)
                  for nm, m, hit in zip(nms, ms, hits)]
            ys = []
            for x, m_sel, hit in zip(xs, m_sels, hits):
                slabs, k_ = [], 0
                for t in hit:
                    slabs.append(x[k_ * s:(k_ + 1) * s] if t else zero)
                    k_ += 1 if t else 0
                ys.append(jnp.dot(m_sel.astype(BF16), jnp.concatenate(slabs, axis=0),
                                  preferred_element_type=F32))
            new_ms = []
            for m, m_sel, y, hit in zip(ms, m_sels, ys, hits):
                new_sel = m_sel + y
                slabs, k_ = [], 0
                for a, t in zip(starts, hit):
                    slabs.append(new_sel[k_ * s:(k_ + 1) * s] if t else m[a:a + s])
                    k_ += 1 if t else 0
                new_ms.append(jnp.concatenate(slabs, axis=0))
            ms = new_ms
            s *= 2
        return ms

    n_prep = PREP_CHUNKS if nc % PREP_CHUNKS == 0 else 1

    def prep(grp, carry):
        chs = [grp * n_prep + j for j in range(n_prep)]
        rows = [pl.ds(pl.multiple_of(ch * c, c), c) for ch in chs]
        gbc = [pltpu.roll(gb_ref[r, :], shift, 1) for r in rows]
        cumc = [pltpu.roll(cum_ref[r, :], shift, 1) for r in rows]
        cumtc = [cumt_ref[:, r] for r in rows]
        qs, ks, vs, aas = {}, {}, {}, {}
        for j in range(n_prep):
            for hl in range(hg):
                cols = slice(hl * hd, (hl + 1) * hd)
                qs[j, hl] = q_s[hl, rows[j], :]
                ks[j, hl] = k_s[hl, rows[j], :]
                vs[j, hl] = v_s[hl, rows[j], :]
                qk = jnp.concatenate([qs[j, hl], ks[j, hl]], axis=0).astype(BF16)
                aas[j, hl] = lax.dot_general(qk, ks[j, hl].astype(BF16), (((1,), (1,)), ((), ())),
                                             preferred_element_type=F32)
        probs = [(j, hl, d) for j in range(n_prep) for hl in range(hg) for d in range(N_DIR)]
        nms, betas, cums, decays = [], [], [], []
        for j, hl, d in probs:
            incl, strict = masks[d]
            base = hl * GATE_SLOTS
            beta_b = jnp.broadcast_to(gbc[j][:, base + N_DIR + d:base + N_DIR + d + 1], (c, hd))
            cum_b = jnp.broadcast_to(cumc[j][:, base + d:base + d + 1], (c, hd))
            cum_r = cumtc[j][base + d:base + d + 1, :]
            decay = jnp.where(incl, jnp.exp(jnp.where(incl, cum_b - cum_r, 0.0)), 0.0)
            nms.append(jnp.where(strict, -(beta_b * aas[j, hl][c:2 * c] * decay), 0.0))
            betas.append(beta_b)
            cums.append(cum_b)
            decays.append(decay)
        ms = tri_inverse(nms, [d == 0 for _, _, d in probs])
        e_cs = [jnp.exp(cum_b) for cum_b in cums]
        uws = [bdot(m, jnp.concatenate([vs[j, hl] * beta_b, ks[j, hl] * (beta_b * e_c)], axis=1))
               for (j, hl, d), m, beta_b, e_c in zip(probs, ms, betas, e_cs)]
        for i, (j, hl, d) in enumerate(probs):
            cum_b, e_c, uw = cums[i], e_cs[i], uws[i]
            gl_b = cum_b[c - 1:c, :] if d == 0 else cum_b[0:1, :]
            kt = ks[j, hl] * jnp.exp(gl_b - cum_b)
            slot = (hl * N_DIR + d) * nc + chs[j]
            u_s[pl.ds(pl.multiple_of(slot * c, c), c), :] = uw[:, 0:hd]
            wq_s[pl.ds(pl.multiple_of(slot * 2 * c, c), c), :] = uw[:, hd:2 * hd].astype(BF16)
            wq_s[pl.ds(pl.multiple_of(slot * 2 * c + c, c), c), :] = (qs[j, hl] * e_c).astype(BF16)
            ktt_s[pl.ds(pl.multiple_of(slot * hd, hd), hd), :] = kt.T.astype(BF16)
            at_s[pl.ds(pl.multiple_of(slot * c, c), c), :] = (aas[j, hl][0:c] * decays[i]).astype(BF16)
            egl_s[pl.ds(pl.multiple_of(slot * 8, 8), 8), :] = jnp.broadcast_to(jnp.exp(gl_b), (8, hd))
        nxt = jnp.minimum(grp + 1, n_groups_prep - 1)
        for j in range(n_prep):
            conv_chunk(nxt * n_prep + j)
        return carry

    n_groups_prep = nc // n_prep
    for j in range(n_prep):
        conv_chunk(j)
    lax.fori_loop(0, n_groups_prep, prep, 0)

    st_s[...] = jnp.zeros(st_s.shape, F32)
    o_s[...] = jnp.zeros(o_s.shape, F32)

    def scan_step(ch_f, ch_b, with_q):
        chains = [(hl, d) for hl in range(hg) for d in range(N_DIR)]
        chs = [ch_f if d == 0 else ch_b for _, d in chains]
        slots = [i * nc + ch for i, ch in enumerate(chs)]
        s_olds = [st_s[i * hd:(i + 1) * hd, :] for i in range(len(chains))]
        n_lhs = 2 * c if with_q else c
        rs = [jnp.dot(wq_s[pl.ds(pl.multiple_of(slot * 2 * c, 2 * c), n_lhs), :], s_old.astype(BF16),
                      preferred_element_type=F32) for slot, s_old in zip(slots, s_olds)]
        v_bs = [(u_s[pl.ds(pl.multiple_of(slot * c, c), c), :] - r_[0:c]).astype(BF16)
                for slot, r_ in zip(slots, rs)]
        kvs = [jnp.dot(ktt_s[pl.ds(pl.multiple_of(slot * hd, hd), hd), :], v_b, preferred_element_type=F32)
               for slot, v_b in zip(slots, v_bs)]
        if with_q:
            avs = [jnp.dot(at_s[pl.ds(pl.multiple_of(slot * c, c), c), :], v_b, preferred_element_type=F32)
                   for slot, v_b in zip(slots, v_bs)]
        for i, (hl, d) in enumerate(chains):
            egl = egl_s[pl.ds(pl.multiple_of(slots[i] * 8, 8), 1), :]
            st_s[i * hd:(i + 1) * hd, :] = s_olds[i] * egl + kvs[i]
            if with_q:
                orow = pl.ds(pl.multiple_of(chs[i] * c, c), c)
                ocol = slice(hl * hd, (hl + 1) * hd)
                o_s[orow, ocol] = o_s[orow, ocol] + (rs[i][c:2 * c] + avs[i])

    def scan_h(it, carry):
        scan_step(it, nc_h - 1 - it, ctx_out)
        return carry

    def scan_x(it, carry):
        scan_step(nc_h + it, nc - 1 - it, True)
        return carry

    lax.fori_loop(0, nc_h, scan_h, 0)
    lax.fori_loop(0, nc_x, scan_x, 0)

    onw = onorm_ref[...]

    def finish(z_ref, out_ref, row0, n_rows):
        tile = 256 if n_rows % 256 == 0 else 128

        def body(r, carry):
            start = pl.multiple_of(r * tile, tile)
            src = pl.ds(pl.multiple_of(row0 + start, CHUNK), tile)
            dst = pl.ds(start, tile)
            outs = []
            for hl in range(hg):
                cols = slice(hl * hd, (hl + 1) * hd)
                o = o_s[src, cols]
                outs.append(_rms(o) * onw * _silu(z_ref[dst, cols].astype(F32)))
            y = outs[0] if hg == 1 else jnp.concatenate(outs, axis=1)
            out_ref[dst, :] = y.astype(out_ref.dtype)
            return carry

        lax.fori_loop(0, n_rows // tile, body, 0)

    finish(zx_ref, ox_ref, seq_h, seq_x)
    if ctx_out:
        finish(zh_ref, oh_ref, 0, seq_h)


def _delta(pxm, phm, gb, cum, cumt, conv_w, onorm, n_heads, ctx_out):
    nb, seq_x, _ = pxm.shape
    seq_h = phm.shape[1]
    hg = HEADS_PER_STEP
    hd = HEAD_DIM
    width = hg * hd
    n_groups = n_heads // hg
    lin_w = n_heads * hd
    seq_t = seq_x + seq_h
    nc = seq_t // CHUNK
    n_chain = hg * N_DIR
    n_gt = hg * GATE_SLOTS

    def main_spec(rows, part):
        return pl.BlockSpec((None, rows, width), lambda b, g: (b, 0, part * n_groups + g))

    in_specs = (
        [main_spec(seq_x, p) for p in range(4)]
        + [main_spec(seq_h, p) for p in range(4)]
        + [pl.BlockSpec((CONV_K, width), lambda b, g, p=p: (0, p * n_groups + g)) for p in range(3)]
        + [pl.BlockSpec((None, seq_t, 128), lambda b, g: (b, 0, 0)),
           pl.BlockSpec((None, seq_t, 128), lambda b, g: (b, 0, 0)),
           pl.BlockSpec((None, n_gt, seq_t), lambda b, g: (b, g, 0)),
           pl.BlockSpec((1, hd), lambda b, g: (0, 0))]
    )
    out_shape = [jax.ShapeDtypeStruct((nb, seq_x, lin_w), BF16)]
    out_specs = [pl.BlockSpec((None, seq_x, width), lambda b, g: (b, 0, g))]
    if ctx_out:
        out_shape.append(jax.ShapeDtypeStruct((nb, seq_h, lin_w), BF16))
        out_specs.append(pl.BlockSpec((None, seq_h, width), lambda b, g: (b, 0, g)))
    scratch = [
        pltpu.VMEM((hg, seq_t + 24, hd), F32),
        pltpu.VMEM((hg, seq_t + 24, hd), F32),
        pltpu.VMEM((hg, seq_t + 24, hd), F32),
        pltpu.VMEM((hg, seq_t, hd), F32),
        pltpu.VMEM((hg, seq_t, hd), F32),
        pltpu.VMEM((hg, seq_t, hd), F32),
        pltpu.VMEM((seq_t, width), F32),
        pltpu.VMEM((n_chain * nc * CHUNK, hd), F32),
        pltpu.VMEM((n_chain * nc * 2 * CHUNK, hd), BF16),
        pltpu.VMEM((n_chain * nc * hd, CHUNK), BF16),
        pltpu.VMEM((n_chain * nc * CHUNK, CHUNK), BF16),
        pltpu.VMEM((n_chain * nc * 8, hd), F32),
        pltpu.VMEM((n_chain * hd, hd), F32),
    ]
    args = [pxm] * 4 + [phm] * 4 + [conv_w] * 3 + [gb, cum, cumt, onorm]
    res = pl.pallas_call(
        functools.partial(_delta_kernel, seq_x=seq_x, seq_h=seq_h, hg=hg, ctx_out=ctx_out),
        out_shape=tuple(out_shape),
        grid=(nb, n_groups),
        in_specs=in_specs,
        out_specs=tuple(out_specs),
        scratch_shapes=scratch,
        compiler_params=_cparams(2),
        name="gated_delta",
    )(*args)
    return res if ctx_out else (res[0], None)


def _pool_kernel(p_ref, w_ref, sc_ref, o_ref, y_s, *, on_grid, n_groups, gw):
    n = p_ref.shape[0]
    blk = 2 * GRID_W if on_grid else n
    period = GRID_W if on_grid else n
    ri = lax.broadcasted_iota(jnp.int32, (blk, blk), 0)
    ci = lax.broadcasted_iota(jnp.int32, (blk, blk), 1)
    pos = lax.broadcasted_iota(jnp.int32, (blk, gw), 0) % period

    for g in range(n_groups):
        size = POOL_SIZES[g]
        half = size // 2
        cols = slice(g * gw, (g + 1) * gw)
        if on_grid:
            n_rows = n // GRID_W
            for r in range(n_rows):
                lo = max(r - half, 0)
                hi = min(r + size - 1 - half, n_rows - 1)
                acc = p_ref[lo * GRID_W:(lo + 1) * GRID_W, cols].astype(F32)
                for rr in range(lo + 1, hi + 1):
                    acc = acc + p_ref[rr * GRID_W:(rr + 1) * GRID_W, cols].astype(F32)
                y_s[r * GRID_W:(r + 1) * GRID_W, :] = acc / float(hi - lo + 1)
            src = y_s
            src_cols = slice(0, gw)
        else:
            src = p_ref
            src_cols = cols
        band = ((ci >= ri - half) & (ci <= ri + size - 1 - half) & (ci // period == ri // period))
        band = jnp.where(band, 1.0, 0.0).astype(BF16)
        lo_c = jnp.maximum(pos - half, 0)
        hi_c = jnp.minimum(pos + size - 1 - half, period - 1)
        cnt = (hi_c - lo_c + 1).astype(F32)
        w_g = w_ref[g]
        scale = sc_ref[:, cols]

        n_blk = n // blk
        group = POOL_BLOCKS if n_blk % POOL_BLOCKS == 0 else 1

        def body(t, carry):
            rows = [pl.ds(pl.multiple_of((t * group + j) * blk, blk), blk) for j in range(group)]
            pieces = []
            for r in rows:
                y = src[r, src_cols].astype(F32)
                hi_p = y.astype(BF16)
                r1 = y - hi_p.astype(F32)
                mid_p = r1.astype(BF16)
                lo_p = (r1 - mid_p.astype(F32)).astype(BF16)
                pieces.append(jnp.concatenate([hi_p, mid_p, lo_p], axis=1))
            tots = [jnp.dot(band, pc, preferred_element_type=F32) for pc in pieces]
            dlts = []
            for r, tot in zip(rows, tots):
                mean = (tot[:, 0:gw] + tot[:, gw:2 * gw] + tot[:, 2 * gw:3 * gw]) / cnt
                dlts.append((mean - p_ref[r, cols].astype(F32)).astype(BF16))
            outs = [jnp.dot(dlt, w_g, preferred_element_type=F32) for dlt in dlts]
            for r, out in zip(rows, outs):
                o_ref[r, cols] = (out * scale).astype(o_ref.dtype)
            return carry

        lax.fori_loop(0, n_blk // group, body, 0)


def _pool(pm, pool_w, pool_scale, part0, on_grid):
    nb, rows, _ = pm.shape
    n_groups, gw, _ = pool_w.shape
    pw = n_groups * gw
    return pl.pallas_call(
        functools.partial(_pool_kernel, on_grid=on_grid, n_groups=n_groups, gw=gw),
        out_shape=jax.ShapeDtypeStruct((nb, rows, pw), BF16),
        grid=(nb,),
        in_specs=[pl.BlockSpec((None, rows, pw), lambda b: (b, 0, part0)),
                  pl.BlockSpec((n_groups, gw, gw), lambda b: (0, 0, 0)),
                  pl.BlockSpec((1, pw), lambda b: (0, 0))],
        out_specs=pl.BlockSpec((None, rows, pw), lambda b: (b, 0, 0)),
        scratch_shapes=[pltpu.VMEM((rows, gw), F32)],
        compiler_params=_cparams(1),
        name="pool_mixer",
    )(pm, pool_w, pool_scale)


def _outproj_kernel(lin_ref, pool_ref, x_ref, mod_ref, wa_ref, wb_ref, o_ref):
    y = (jnp.dot(lin_ref[...], wa_ref[...], preferred_element_type=F32)
         + jnp.dot(pool_ref[...], wb_ref[...], preferred_element_type=F32))
    o_ref[...] = x_ref[...] + mod_ref[2:3, :] * y


def _outproj(lin, pool, x3, mod, mod_base, w_a, w_b):
    nb, rows, d = x3.shape
    tm = min(512, rows)
    ka = w_a.shape[0]
    kb = w_b.shape[0]
    return pl.pallas_call(
        _outproj_kernel,
        out_shape=jax.ShapeDtypeStruct((nb, rows, d), F32),
        grid=(nb, rows // tm),
        in_specs=[pl.BlockSpec((None, tm, ka), lambda b, i: (b, i, 0)),
                  pl.BlockSpec((None, tm, kb), lambda b, i: (b, i, 0)),
                  pl.BlockSpec((None, tm, d), lambda b, i: (b, i, 0)),
                  pl.BlockSpec((None, N_MOD, d), lambda b, i: (mod_base + b, 0, 0)),
                  pl.BlockSpec((ka, d), lambda b, i: (0, 0)),
                  pl.BlockSpec((kb, d), lambda b, i: (0, 0))],
        out_specs=pl.BlockSpec((None, tm, d), lambda b, i: (b, i, 0)),
        compiler_params=_cparams(2),
        name="out_proj",
    )(lin, pool, x3, mod, w_a, w_b)


def _mlp_kernel(x_ref, mod_ref, nw_ref, wu_ref, wd_ref, fw_ref, o_ref, xn_ref, *, final_norm):
    f = pl.program_id(2)

    @pl.when(f == 0)
    def _():
        _norm_mod_to(x_ref, nw_ref, mod_ref, 3, 4, xn_ref)

    @pl.when(f == 0)
    def _():
        o_ref[...] = jnp.zeros(o_ref.shape, F32)

    wu = wu_ref[...]
    wd = wd_ref[...]
    slabs = range(0, xn_ref.shape[0], ROW_SLAB)
    hs = [jnp.dot(xn_ref[r:r + ROW_SLAB, :], wu, preferred_element_type=F32) for r in slabs]
    hs = [jnp.square(jnp.maximum(h, 0.0)).astype(BF16) for h in hs]
    for r, h in zip(slabs, hs):
        o_ref[r:r + ROW_SLAB, :] += jnp.dot(h, wd, preferred_element_type=F32)

    @pl.when(f == pl.num_programs(2) - 1)
    def _():
        gate = mod_ref[5:6, :]
        fw = fw_ref[...]
        rows = 128

        def body(r, carry):
            sl = pl.ds(pl.multiple_of(r * rows, rows), rows)
            y = x_ref[sl, :] + gate * o_ref[sl, :]
            if final_norm:
                y = _rms(y) * fw
            o_ref[sl, :] = y
            return carry

        lax.fori_loop(0, x_ref.shape[0] // rows, body, 0)


def _mlp(x3, mod, mod_base, nw, w_up, w_down, final_w, final_norm):
    nb, rows, d = x3.shape
    dff = w_up.shape[1]
    tm = min(1024, rows)
    tf = 512
    return pl.pallas_call(
        functools.partial(_mlp_kernel, final_norm=final_norm),
        out_shape=jax.ShapeDtypeStruct((nb, rows, d), F32),
        grid=(nb, rows // tm, dff // tf),
        in_specs=[pl.BlockSpec((None, tm, d), lambda b, i, f: (b, i, 0)),
                  pl.BlockSpec((None, N_MOD, d), lambda b, i, f: (mod_base + b, 0, 0)),
                  pl.BlockSpec((1, d), lambda b, i, f: (0, 0)),
                  pl.BlockSpec((d, tf), lambda b, i, f: (0, f)),
                  pl.BlockSpec((tf, d), lambda b, i, f: (f, 0)),
                  pl.BlockSpec((1, d), lambda b, i, f: (0, 0))],
        out_specs=pl.BlockSpec((None, tm, d), lambda b, i, f: (b, i, 0)),
        scratch_shapes=[pltpu.VMEM((tm, d), BF16)],
        compiler_params=_cparams(3),
        name="mlp",
    )(x3, mod, nw, w_up, w_down, final_w)


def _gate_layouts(w_in_l, a_log_l, dt_bias_l, lin_w, n_heads):
    d = w_in_l.shape[0]
    decay_start = 4 * lin_w
    ab = w_in_l[:, decay_start:decay_start + 2 * N_DIR * n_heads]
    ab = ab.reshape(d, 2 * N_DIR, n_heads)
    ab = jnp.transpose(ab, (0, 2, 1))
    ab = jnp.pad(ab, ((0, 0), (0, 0), (0, GATE_SLOTS - 2 * N_DIR)))
    w_ab = ab.reshape(d, n_heads * GATE_SLOTS)
    w_ab = jnp.pad(w_ab, ((0, 0), (0, 128 - n_heads * GATE_SLOTS))).astype(BF16)

    def per_head(p):
        return jnp.pad(p.T.astype(F32), ((0, 0), (0, GATE_SLOTS - N_DIR)))

    alog = per_head(a_log_l)
    dtb = per_head(dt_bias_l)
    alog_c = jnp.pad(alog.reshape(1, n_heads * GATE_SLOTS), ((0, 0), (0, 128 - n_heads * GATE_SLOTS)))
    dtb_c = jnp.pad(dtb.reshape(1, n_heads * GATE_SLOTS), ((0, 0), (0, 128 - n_heads * GATE_SLOTS)))
    alog_t = jnp.broadcast_to(alog.reshape(n_heads * GATE_SLOTS, 1), (n_heads * GATE_SLOTS, 128))
    dtb_t = jnp.broadcast_to(dtb.reshape(n_heads * GATE_SLOTS, 1), (n_heads * GATE_SLOTS, 128))
    return w_ab, alog_c, dtb_c, alog_t, dtb_t


def kernel(x, c, ctx, c_ctx, w_ada, b_ada, norm1_w, norm2_w, w_in, conv_w, a_log, dt_bias, onorm_w,
           pool_w, pool_scale, w_out, w_up, w_down, final_norm_w):
    nb, seq_x, d = x.shape
    seq_h = ctx.shape[1]
    depth = w_ada.shape[0]
    n_heads = a_log.shape[2]
    lin_w = n_heads * HEAD_DIM
    pool_width = pool_scale.shape[1]
    pool_start = 4 * lin_w + 2 * N_DIR * n_heads
    assert nb + 1 <= MOD_ROWS and seq_x % CHUNK == 0 and seq_h % CHUNK == 0
    assert n_heads % HEADS_PER_STEP == 0 and seq_x % GRID_W == 0

    c_all = jnp.concatenate([c, c_ctx[None, :], jnp.zeros((MOD_ROWS - nb - 1, d), F32)], axis=0)
    mod_all = _ada(c_all, w_ada, b_ada).reshape(depth, MOD_ROWS, N_MOD, d)

    h = ctx
    for l in range(depth):
        ctx_out = l < depth - 1
        mod = mod_all[l]
        w_main = jnp.concatenate([w_in[l, :, :4 * lin_w], w_in[l, :, pool_start:]], axis=1).astype(BF16)
        w_ab, alog_c, dtb_c, alog_t, dtb_t = _gate_layouts(w_in[l], a_log[l], dt_bias[l], lin_w, n_heads)
        n_gate_rows = n_heads * GATE_SLOTS
        nw1 = norm1_w[l].reshape(1, d)
        nw2 = norm2_w[l].reshape(1, d)
        w_oa = w_out[l, :lin_w].astype(BF16)
        w_ob = w_out[l, lin_w:].astype(BF16)
        w_u = w_up[l].astype(BF16)
        w_d = w_down[l].astype(BF16)
        pw = pool_w[l].astype(BF16)
        ps = pool_scale[l].reshape(1, pool_width)
        fw = final_norm_w.reshape(1, d)

        pxm, abx, abtx = _inproj(x, mod, 0, nw1, w_main, w_ab, n_gate_rows)
        h3 = h.reshape(1, nb * seq_h, d)
        phm, abh, abth = _inproj(h3, mod, nb, nw1, w_main, w_ab, n_gate_rows)
        phm = phm.reshape(nb, seq_h, -1)
        abh = abh.reshape(nb, seq_h, 128)
        abth = jnp.transpose(abth.reshape(-1, nb, seq_h), (1, 0, 2))

        gb, cum, cumt = _gates(abx, abtx, abh, abth, alog_c, dtb_c, alog_t, dtb_t)
        lin_x, lin_h = _delta(pxm, phm, gb, cum, cumt, conv_w[l], onorm_w[l].reshape(1, HEAD_DIM),
                              n_heads, ctx_out)
        part0 = (4 * lin_w) // pool_width
        pool_x = _pool(pxm, pw, ps, part0, True)
        x = _outproj(lin_x, pool_x, x, mod, 0, w_oa, w_ob)
        x = _mlp(x, mod, 0, nw2, w_u, w_d, fw, l == depth - 1)
        if ctx_out:
            pool_h = _pool(phm, pw, ps, part0, False)
            h3 = _outproj(lin_h.reshape(1, nb * seq_h, lin_w), pool_h.reshape(1, nb * seq_h, pool_width),
                          h3, mod, nb, w_oa, w_ob)
            h = _mlp(h3, mod, nb, nw2, w_u, w_d, fw, False).reshape(nb, seq_h, d)
    return x
```

```python
import functools
import math

import jax
import jax.numpy as jnp
from jax import lax
from jax.experimental import pallas as pl
from jax.experimental.pallas import tpu as pltpu

F32 = jnp.float32
BF16 = jnp.bfloat16

EPS = 1e-6
N_MOD = 6
HEAD_DIM = 128
N_DIR = 2
CONV_K = 5
GRID_W = 64
POOL_SIZES = (2, 4, 8, 16)
CHUNK = 128
INV_BLOCK = 16
GATE_SLOTS = 8
HEADS_PER_STEP = 2
CONV_PHASE = 4
POOL_BLOCKS = 4
PREP_CHUNKS = 3
ROW_SLAB = 256
MOD_ROWS = 24
VMEM_LIMIT = 56 * 1024 * 1024


def _silu(x):
    return x * (1.0 / (1.0 + jnp.exp(-x)))


def _softplus(x):
    return jnp.maximum(x, 0.0) + jnp.log1p(jnp.exp(-jnp.abs(x)))


def _cparams(n_axes):
    return pltpu.CompilerParams(dimension_semantics=("arbitrary",) * n_axes,
                                vmem_limit_bytes=VMEM_LIMIT)


def _ada_kernel(c_ref, w_ref, b_ref, o_ref):
    sc = _silu(c_ref[...]).astype(BF16)
    o_ref[...] = jnp.dot(sc, w_ref[...].astype(BF16), preferred_element_type=F32) + b_ref[...]


def _ada(c_all, w_ada, b_ada):
    depth, d, n = w_ada.shape
    tn = 1024
    return pl.pallas_call(
        _ada_kernel,
        out_shape=jax.ShapeDtypeStruct((depth, MOD_ROWS, n), F32),
        grid=(depth, n // tn),
        in_specs=[pl.BlockSpec((MOD_ROWS, d), lambda l, j: (0, 0)),
                  pl.BlockSpec((None, d, tn), lambda l, j: (l, 0, j)),
                  pl.BlockSpec((None, 1, tn), lambda l, j: (l, 0, j))],
        out_specs=pl.BlockSpec((None, MOD_ROWS, tn), lambda l, j: (l, 0, j)),
        compiler_params=_cparams(2),
        name="ada_mod",
    )(c_all, w_ada, b_ada.reshape(depth, 1, n))


def _rms(x):
    return x * lax.rsqrt(jnp.mean(x * x, axis=-1, keepdims=True) + EPS)


def _norm_mod_to(x_ref, nw_ref, mod_ref, i_shift, i_scale, out_ref, rows=128):
    nw = nw_ref[...]
    shift = mod_ref[i_shift:i_shift + 1, :]
    scale = mod_ref[i_scale:i_scale + 1, :]

    def body(r, carry):
        sl = pl.ds(pl.multiple_of(r * rows, rows), rows)
        y = _rms(x_ref[sl, :]) * nw
        out_ref[sl, :] = (y * (1.0 + scale) + shift).astype(out_ref.dtype)
        return carry

    lax.fori_loop(0, x_ref.shape[0] // rows, body, 0)


def _inproj_kernel(x_ref, mod_ref, nw_ref, w_ref, wab_ref, o_ref, ab_ref, abt_ref, xn_a, xn_b):
    s = pl.program_id(0)
    j = pl.program_id(1)
    tm = x_ref.shape[0]
    n_gate_rows = abt_ref.shape[0]

    @pl.when((s == 0) & (j == 0))
    def _():
        xn_b[...] = jnp.zeros(xn_b.shape, xn_b.dtype)

    def step(xn_fill, xn_use):
        @pl.when(j == 0)
        def _():
            ab = jnp.dot(xn_use[...], wab_ref[...], preferred_element_type=F32)
            ab_ref[...] = ab
            abt_ref[...] = ab.T[0:n_gate_rows, :]

        nw = nw_ref[...]
        shift = mod_ref[0:1, :]
        scale = mod_ref[1:2, :]
        n_parts = pl.num_programs(1) - 1
        part_rows = tm // 4
        row0 = jnp.minimum(j, n_parts - 1) * part_rows
        for r in range(0, part_rows, 128):
            sl = pl.ds(pl.multiple_of(row0 + r, 128), 128)
            y = _rms(x_ref[sl, :]) * nw
            xn_fill[sl, :] = (y * (1.0 + scale) + shift).astype(BF16)

        w = w_ref[...]
        for r in range(0, tm, ROW_SLAB):
            o_ref[r:r + ROW_SLAB, :] = jnp.dot(xn_use[r:r + ROW_SLAB, :], w,
                                               preferred_element_type=F32).astype(o_ref.dtype)

    @pl.when(s % 2 == 0)
    def _():
        step(xn_a, xn_b)

    @pl.when(s % 2 == 1)
    def _():
        step(xn_b, xn_a)


def _inproj(x3, mod, mod_base, nw, w_main, w_ab, n_gate_rows):
    nb, rows, d = x3.shape
    n = w_main.shape[1]
    tm = min(1024, rows)
    tn = 1024
    tpb = rows // tm
    n_tiles = nb * tpb
    assert n // tn == 5 and tm % 512 == 0

    def tile_in(s):
        t = jnp.minimum(s, n_tiles - 1)
        return t // tpb, t % tpb

    def tile_out(s):
        t = jnp.maximum(s - 1, 0)
        return t // tpb, t % tpb

    return pl.pallas_call(
        _inproj_kernel,
        out_shape=(jax.ShapeDtypeStruct((nb, rows, n), BF16),
                   jax.ShapeDtypeStruct((nb, rows, 128), F32),
                   jax.ShapeDtypeStruct((nb, n_gate_rows, rows), F32)),
        grid=(n_tiles + 1, n // tn),
        in_specs=[pl.BlockSpec((None, tm, d), lambda s, j: (*tile_in(s), 0)),
                  pl.BlockSpec((None, N_MOD, d), lambda s, j: (mod_base + tile_in(s)[0], 0, 0)),
                  pl.BlockSpec((1, d), lambda s, j: (0, 0)),
                  pl.BlockSpec((d, tn), lambda s, j: (0, j)),
                  pl.BlockSpec((d, 128), lambda s, j: (0, 0))],
        out_specs=(pl.BlockSpec((None, tm, tn), lambda s, j: (*tile_out(s), jnp.where(s == 0, 0, j))),
                   pl.BlockSpec((None, tm, 128), lambda s, j: (*tile_out(s), 0)),
                   pl.BlockSpec((None, n_gate_rows, tm), lambda s, j: (tile_out(s)[0], 0, tile_out(s)[1]))),
        scratch_shapes=[pltpu.VMEM((tm, d), BF16), pltpu.VMEM((tm, d), BF16)],
        compiler_params=_cparams(2),
        name="in_proj",
    )(x3, mod, nw, w_main, w_ab)


def _split3(x):
    hi = x.astype(BF16)
    r1 = x - hi.astype(F32)
    mid = r1.astype(BF16)
    lo = (r1 - mid.astype(F32)).astype(BF16)
    return hi, mid, lo


def _gate_kernel(abx_ref, abtx_ref, abh_ref, abth_ref, alog_ref, dtb_ref, alogt_ref, dtbt_ref,
                 gb_ref, cum_ref, cumt_ref, *, seq_x, seq_h):
    c = CHUNK
    n_gt = abtx_ref.shape[0]
    slot_l = lax.broadcasted_iota(jnp.int32, (c, 128), 1) % GATE_SLOTS
    row_slot = lax.broadcasted_iota(jnp.int32, (n_gt, c), 0) % GATE_SLOTS
    ri = lax.broadcasted_iota(jnp.int32, (c, c), 0)
    ci = lax.broadcasted_iota(jnp.int32, (c, c), 1)
    m_le = jnp.where(ci <= ri, 1.0, 0.0).astype(BF16)
    m_ge = jnp.where(ci >= ri, 1.0, 0.0).astype(BF16)
    alog_r = alog_ref[...]
    dtb_r = dtb_ref[...]
    alogt = alogt_ref[...]
    dtbt = dtbt_ref[...]

    def part(ab_ref, abt_ref, row0, n_rows):
        def body(r, carry):
            src = pl.ds(pl.multiple_of(r * c, c), c)
            dst = pl.ds(pl.multiple_of(row0 + r * c, c), c)
            a = ab_ref[src, :]
            g = -jnp.exp(alog_r) * _softplus(a + dtb_r)
            gb = jnp.where(slot_l < N_DIR, g, 1.0 / (1.0 + jnp.exp(-a)))
            gb_ref[dst, :] = gb
            pieces = jnp.concatenate(_split3(gb), axis=1)
            pre = jnp.dot(m_le, pieces, preferred_element_type=F32)
            suf = jnp.dot(m_ge, pieces, preferred_element_type=F32)
            pre = pre[:, 0:128] + pre[:, 128:256] + pre[:, 256:384]
            suf = suf[:, 0:128] + suf[:, 128:256] + suf[:, 256:384]
            cum_ref[dst, :] = jnp.where(slot_l == 0, pre, suf)
            gt = -jnp.exp(alogt) * _softplus(abt_ref[:, src] + dtbt)
            pieces_t = jnp.concatenate(_split3(gt), axis=0)
            pre_t = jnp.dot(pieces_t, m_ge, preferred_element_type=F32)
            suf_t = jnp.dot(pieces_t, m_le, preferred_element_type=F32)
            pre_t = pre_t[0:n_gt] + pre_t[n_gt:2 * n_gt] + pre_t[2 * n_gt:3 * n_gt]
            suf_t = suf_t[0:n_gt] + suf_t[n_gt:2 * n_gt] + suf_t[2 * n_gt:3 * n_gt]
            cumt_ref[:, dst] = jnp.where(row_slot == 0, pre_t, suf_t)
            return carry

        lax.fori_loop(0, n_rows // c, body, 0, unroll=2)

    part(abh_ref, abth_ref, 0, seq_h)
    part(abx_ref, abtx_ref, seq_h, seq_x)


def _gates(abx, abtx, abh, abth, alog_c, dtb_c, alog_t, dtb_t):
    nb, seq_x, _ = abx.shape
    seq_h = abh.shape[1]
    n_gt = abtx.shape[1]
    seq_t = seq_x + seq_h
    return pl.pallas_call(
        functools.partial(_gate_kernel, seq_x=seq_x, seq_h=seq_h),
        out_shape=(jax.ShapeDtypeStruct((nb, seq_t, 128), F32),
                   jax.ShapeDtypeStruct((nb, seq_t, 128), F32),
                   jax.ShapeDtypeStruct((nb, n_gt, seq_t), F32)),
        grid=(nb,),
        in_specs=[pl.BlockSpec((None, seq_x, 128), lambda b: (b, 0, 0)),
                  pl.BlockSpec((None, n_gt, seq_x), lambda b: (b, 0, 0)),
                  pl.BlockSpec((None, seq_h, 128), lambda b: (b, 0, 0)),
                  pl.BlockSpec((None, n_gt, seq_h), lambda b: (b, 0, 0)),
                  pl.BlockSpec((1, 128), lambda b: (0, 0)),
                  pl.BlockSpec((1, 128), lambda b: (0, 0)),
                  pl.BlockSpec((n_gt, 128), lambda b: (0, 0)),
                  pl.BlockSpec((n_gt, 128), lambda b: (0, 0))],
        out_specs=(pl.BlockSpec((None, seq_t, 128), lambda b: (b, 0, 0)),
                   pl.BlockSpec((None, seq_t, 128), lambda b: (b, 0, 0)),
                   pl.BlockSpec((None, n_gt, seq_t), lambda b: (b, 0, 0))),
        compiler_params=_cparams(1),
        name="gates",
    )(abx, abtx, abh, abth, alog_c, dtb_c, alog_t, dtb_t)


def _delta_kernel(*refs, seq_x, seq_h, hg, ctx_out):
    n_in = 14
    (qx_ref, kx_ref, vx_ref, zx_ref, qh_ref, kh_ref, vh_ref, zh_ref,
     cq_ref, ck_ref, cv_ref, gb_ref, cum_ref, cumt_ref) = refs[:n_in]
    onorm_ref = refs[n_in]
    if ctx_out:
        ox_ref, oh_ref = refs[n_in + 1:n_in + 3]
        scratch = refs[n_in + 3:]
    else:
        ox_ref = refs[n_in + 1]
        oh_ref = None
        scratch = refs[n_in + 2:]
    (padq_s, padk_s, padv_s, q_s, k_s, v_s, o_s, u_s, wq_s, ktt_s, at_s, egl_s, st_s) = scratch

    c = CHUNK
    hd = HEAD_DIM
    width = hg * hd
    seq_t = seq_h + seq_x
    nc_h = seq_h // c
    nc_x = seq_x // c
    nc = nc_h + nc_x
    n_chain = hg * N_DIR

    zero8 = jnp.zeros((8, hd), F32)
    for pad, h_ref, x_ref in ((padq_s, qh_ref, qx_ref), (padk_s, kh_ref, kx_ref), (padv_s, vh_ref, vx_ref)):
        for hl in range(hg):
            cols = slice(hl * hd, (hl + 1) * hd)
            pad[hl, 0:8, :] = zero8
            pad[hl, 8:8 + seq_h, :] = h_ref[:, cols].astype(F32)
            pad[hl, 8 + seq_h:16 + seq_h, :] = zero8
            pad[hl, 16 + seq_h:16 + seq_t, :] = x_ref[:, cols].astype(F32)
            pad[hl, 16 + seq_t:24 + seq_t, :] = zero8
    conv_taps = [[[w_ref[i:i + 1, hl * hd:(hl + 1) * hd] for i in range(CONV_K)] for hl in range(hg)]
                 for w_ref in (cq_ref, ck_ref, cv_ref)]

    def conv_chunk(ch):
        base = ch * c + jnp.where(ch >= nc_h, 16, 8)
        n_ph = c // CONV_PHASE
        for pad, taps, dst, mode in ((padq_s, conv_taps[0], q_s, "q"), (padk_s, conv_taps[1], k_s, "k"),
                                     (padv_s, conv_taps[2], v_s, "v")):
            for hl in range(hg):
                shifted = {off: pad[hl, pl.ds(base + off, n_ph, stride=CONV_PHASE), :]
                           for off in range(-(CONV_K // 2), CONV_PHASE + CONV_K // 2)}
                for ph in range(CONV_PHASE):
                    acc = None
                    for i in range(CONV_K):
                        term = shifted[ph + i - CONV_K // 2] * taps[hl][i]
                        acc = term if acc is None else acc + term
                    y = _silu(acc)
                    if mode != "v":
                        y = y * lax.rsqrt(jnp.sum(y * y, axis=-1, keepdims=True) + EPS)
                        if mode == "q":
                            y = y * (hd ** -0.5)
                    dst[hl, pl.ds(ch * c + ph, n_ph, stride=CONV_PHASE), :] = y

    shift = (128 - pl.program_id(1) * (hg * GATE_SLOTS)) % 128
    ri = lax.broadcasted_iota(jnp.int32, (c, c), 0)
    ci = lax.broadcasted_iota(jnp.int32, (c, c), 1)
    lower_incl = (ci <= ri)
    upper_incl = (ci >= ri)

    eye = jnp.where(ri == ci, 1.0, 0.0).astype(F32)
    masks = ((lower_incl, ci < ri), (upper_incl, ci > ri))

    def bdot(a, b):
        return jnp.dot(a.astype(BF16), b.astype(BF16), preferred_element_type=F32)

    same_blk = (ri // INV_BLOCK) == (ci // INV_BLOCK)

    def tri_inverse(nms, lowers):
        nds = [jnp.where(same_blk, nm, 0.0) for nm in nms]
        ms = [eye + nd for nd in nds]
        ps = [bdot(nd, nd) for nd in nds]
        for _ in range(int(math.log2(INV_BLOCK)) - 2):
            rs = [bdot(p, jnp.concatenate([m, p], axis=1)) for m, p in zip(ms, ps)]
            ms = [m + r_[:, 0:c] for m, r_ in zip(ms, rs)]
            ps = [r_[:, c:2 * c] for r_ in rs]
        rs = [bdot(p, m) for m, p in zip(ms, ps)]
        ms = [m + r_ for m, r_ in zip(ms, rs)]
        s = INV_BLOCK
        while s < c:
            starts = list(range(0, c, s))
            pair = ((ri // (2 * s)) == (ci // (2 * s))) & ((ri // s) != (ci // s))
            zero = jnp.zeros((s, c), BF16)
            hits = [[(a // s) % 2 == (1 if lower else 0) for a in starts] for lower in lowers]

            def take(xm, hit):
                return jnp.concatenate([xm[a:a + s] for a, t in zip(starts, hit) if t], axis=0)

            m_sels = [take(m, hit) for m, hit in zip(ms, hits)]
            xs = [bdot(take(jnp.where(pair, nm, 0.0), hit), m).astype(BF16)
                  for nm, m, hit in zip(nms, ms, hits)]
            ys = []
            for x, m_sel, hit in zip(xs, m_sels, hits):
                slabs, k_ = [], 0
                for t in hit:
                    slabs.append(x[k_ * s:(k_ + 1) * s] if t else zero)
                    k_ += 1 if t else 0
                ys.append(jnp.dot(m_sel.astype(BF16), jnp.concatenate(slabs, axis=0),
                                  preferred_element_type=F32))
            new_ms = []
            for m, m_sel, y, hit in zip(ms, m_sels, ys, hits):
                new_sel = m_sel + y
                slabs, k_ = [], 0
                for a, t in zip(starts, hit):
                    slabs.append(new_sel[k_ * s:(k_ + 1) * s] if t else m[a:a + s])
                    k_ += 1 if t else 0
                new_ms.append(jnp.concatenate(slabs, axis=0))
            ms = new_ms
            s *= 2
        return ms

    n_prep = PREP_CHUNKS if nc % PREP_CHUNKS == 0 else 1

    def prep(grp, carry):
        chs = [grp * n_prep + j for j in range(n_prep)]
        rows = [pl.ds(pl.multiple_of(ch * c, c), c) for ch in chs]
        gbc = [pltpu.roll(gb_ref[r, :], shift, 1) for r in rows]
        cumc = [pltpu.roll(cum_ref[r, :], shift, 1) for r in rows]
        cumtc = [cumt_ref[:, r] for r in rows]
        qs, ks, vs, aas = {}, {}, {}, {}
        for j in range(n_prep):
            for hl in range(hg):
                cols = slice(hl * hd, (hl + 1) * hd)
                qs[j, hl] = q_s[hl, rows[j], :]
                ks[j, hl] = k_s[hl, rows[j], :]
                vs[j, hl] = v_s[hl, rows[j], :]
                qk = jnp.concatenate([qs[j, hl], ks[j, hl]], axis=0).astype(BF16)
                aas[j, hl] = lax.dot_general(qk, ks[j, hl].astype(BF16), (((1,), (1,)), ((), ())),
                                             preferred_element_type=F32)
        probs = [(j, hl, d) for j in range(n_prep) for hl in range(hg) for d in range(N_DIR)]
        nms, betas, cums, decays = [], [], [], []
        for j, hl, d in probs:
            incl, strict = masks[d]
            base = hl * GATE_SLOTS
            beta_b = jnp.broadcast_to(gbc[j][:, base + N_DIR + d:base + N_DIR + d + 1], (c, hd))
            cum_b = jnp.broadcast_to(cumc[j][:, base + d:base + d + 1], (c, hd))
            cum_r = cumtc[j][base + d:base + d + 1, :]
            decay = jnp.where(incl, jnp.exp(jnp.where(incl, cum_b - cum_r, 0.0)), 0.0)
            nms.append(jnp.where(strict, -(beta_b * aas[j, hl][c:2 * c] * decay), 0.0))
            betas.append(beta_b)
            cums.append(cum_b)
            decays.append(decay)
        ms = tri_inverse(nms, [d == 0 for _, _, d in probs])
        e_cs = [jnp.exp(cum_b) for cum_b in cums]
        uws = [bdot(m, jnp.concatenate([vs[j, hl] * beta_b, ks[j, hl] * (beta_b * e_c)], axis=1))
               for (j, hl, d), m, beta_b, e_c in zip(probs, ms, betas, e_cs)]
        for i, (j, hl, d) in enumerate(probs):
            cum_b, e_c, uw = cums[i], e_cs[i], uws[i]
            gl_b = cum_b[c - 1:c, :] if d == 0 else cum_b[0:1, :]
            kt = ks[j, hl] * jnp.exp(gl_b - cum_b)
            slot = (hl * N_DIR + d) * nc + chs[j]
            u_s[pl.ds(pl.multiple_of(slot * c, c), c), :] = uw[:, 0:hd]
            wq_s[pl.ds(pl.multiple_of(slot * 2 * c, c), c), :] = uw[:, hd:2 * hd].astype(BF16)
            wq_s[pl.ds(pl.multiple_of(slot * 2 * c + c, c), c), :] = (qs[j, hl] * e_c).astype(BF16)
            ktt_s[pl.ds(pl.multiple_of(slot * hd, hd), hd), :] = kt.T.astype(BF16)
            at_s[pl.ds(pl.multiple_of(slot * c, c), c), :] = (aas[j, hl][0:c] * decays[i]).astype(BF16)
            egl_s[pl.ds(pl.multiple_of(slot * 8, 8), 8), :] = jnp.broadcast_to(jnp.exp(gl_b), (8, hd))
        nxt = jnp.minimum(grp + 1, n_groups_prep - 1)
        for j in range(n_prep):
            conv_chunk(nxt * n_prep + j)
        return carry

    n_groups_prep = nc // n_prep
    for j in range(n_prep):
        conv_chunk(j)
    lax.fori_loop(0, n_groups_prep, prep, 0)

    st_s[...] = jnp.zeros(st_s.shape, F32)
    o_s[...] = jnp.zeros(o_s.shape, F32)

    def scan_step(ch_f, ch_b, with_q):
        chains = [(hl, d) for hl in range(hg) for d in range(N_DIR)]
        chs = [ch_f if d == 0 else ch_b for _, d in chains]
        slots = [i * nc + ch for i, ch in enumerate(chs)]
        s_olds = [st_s[i * hd:(i + 1) * hd, :] for i in range(len(chains))]
        n_lhs = 2 * c if with_q else c
        rs = [jnp.dot(wq_s[pl.ds(pl.multiple_of(slot * 2 * c, 2 * c), n_lhs), :], s_old.astype(BF16),
                      preferred_element_type=F32) for slot, s_old in zip(slots, s_olds)]
        v_bs = [(u_s[pl.ds(pl.multiple_of(slot * c, c), c), :] - r_[0:c]).astype(BF16)
                for slot, r_ in zip(slots, rs)]
        kvs = [jnp.dot(ktt_s[pl.ds(pl.multiple_of(slot * hd, hd), hd), :], v_b, preferred_element_type=F32)
               for slot, v_b in zip(slots, v_bs)]
        if with_q:
            avs = [jnp.dot(at_s[pl.ds(pl.multiple_of(slot * c, c), c), :], v_b, preferred_element_type=F32)
                   for slot, v_b in zip(slots, v_bs)]
        for i, (hl, d) in enumerate(chains):
            egl = egl_s[pl.ds(pl.multiple_of(slots[i] * 8, 8), 1), :]
            st_s[i * hd:(i + 1) * hd, :] = s_olds[i] * egl + kvs[i]
            if with_q:
                orow = pl.ds(pl.multiple_of(chs[i] * c, c), c)
                ocol = slice(hl * hd, (hl + 1) * hd)
                o_s[orow, ocol] = o_s[orow, ocol] + (rs[i][c:2 * c] + avs[i])

    def scan_h(it, carry):
        scan_step(it, nc_h - 1 - it, ctx_out)
        return carry

    def scan_x(it, carry):
        scan_step(nc_h + it, nc - 1 - it, True)
        return carry

    lax.fori_loop(0, nc_h, scan_h, 0)
    lax.fori_loop(0, nc_x, scan_x, 0)

    onw = onorm_ref[...]

    def finish(z_ref, out_ref, row0, n_rows):
        tile = 256 if n_rows % 256 == 0 else 128

        def body(r, carry):
            start = pl.multiple_of(r * tile, tile)
            src = pl.ds(pl.multiple_of(row0 + start, CHUNK), tile)
            dst = pl.ds(start, tile)
            outs = []
            for hl in range(hg):
                cols = slice(hl * hd, (hl + 1) * hd)
                o = o_s[src, cols]
                outs.append(_rms(o) * onw * _silu(z_ref[dst, cols].astype(F32)))
            y = outs[0] if hg == 1 else jnp.concatenate(outs, axis=1)
            out_ref[dst, :] = y.astype(out_ref.dtype)
            return carry

        lax.fori_loop(0, n_rows // tile, body, 0)

    finish(zx_ref, ox_ref, seq_h, seq_x)
    if ctx_out:
        finish(zh_ref, oh_ref, 0, seq_h)


def _delta(pxm, phm, gb, cum, cumt, conv_w, onorm, n_heads, ctx_out):
    nb, seq_x, _ = pxm.shape
    seq_h = phm.shape[1]
    hg = HEADS_PER_STEP
    hd = HEAD_DIM
    width = hg * hd
    n_groups = n_heads // hg
    lin_w = n_heads * hd
    seq_t = seq_x + seq_h
    nc = seq_t // CHUNK
    n_chain = hg * N_DIR
    n_gt = hg * GATE_SLOTS

    def main_spec(rows, part):
        return pl.BlockSpec((None, rows, width), lambda b, g: (b, 0, part * n_groups + g))

    in_specs = (
        [main_spec(seq_x, p) for p in range(4)]
        + [main_spec(seq_h, p) for p in range(4)]
        + [pl.BlockSpec((CONV_K, width), lambda b, g, p=p: (0, p * n_groups + g)) for p in range(3)]
        + [pl.BlockSpec((None, seq_t, 128), lambda b, g: (b, 0, 0)),
           pl.BlockSpec((None, seq_t, 128), lambda b, g: (b, 0, 0)),
           pl.BlockSpec((None, n_gt, seq_t), lambda b, g: (b, g, 0)),
           pl.BlockSpec((1, hd), lambda b, g: (0, 0))]
    )
    out_shape = [jax.ShapeDtypeStruct((nb, seq_x, lin_w), BF16)]
    out_specs = [pl.BlockSpec((None, seq_x, width), lambda b, g: (b, 0, g))]
    if ctx_out:
        out_shape.append(jax.ShapeDtypeStruct((nb, seq_h, lin_w), BF16))
        out_specs.append(pl.BlockSpec((None, seq_h, width), lambda b, g: (b, 0, g)))
    scratch = [
        pltpu.VMEM((hg, seq_t + 24, hd), F32),
        pltpu.VMEM((hg, seq_t + 24, hd), F32),
        pltpu.VMEM((hg, seq_t + 24, hd), F32),
        pltpu.VMEM((hg, seq_t, hd), F32),
        pltpu.VMEM((hg, seq_t, hd), F32),
        pltpu.VMEM((hg, seq_t, hd), F32),
        pltpu.VMEM((seq_t, width), F32),
        pltpu.VMEM((n_chain * nc * CHUNK, hd), F32),
        pltpu.VMEM((n_chain * nc * 2 * CHUNK, hd), BF16),
        pltpu.VMEM((n_chain * nc * hd, CHUNK), BF16),
        pltpu.VMEM((n_chain * nc * CHUNK, CHUNK), BF16),
        pltpu.VMEM((n_chain * nc * 8, hd), F32),
        pltpu.VMEM((n_chain * hd, hd), F32),
    ]
    args = [pxm] * 4 + [phm] * 4 + [conv_w] * 3 + [gb, cum, cumt, onorm]
    res = pl.pallas_call(
        functools.partial(_delta_kernel, seq_x=seq_x, seq_h=seq_h, hg=hg, ctx_out=ctx_out),
        out_shape=tuple(out_shape),
        grid=(nb, n_groups),
        in_specs=in_specs,
        out_specs=tuple(out_specs),
        scratch_shapes=scratch,
        compiler_params=_cparams(2),
        name="gated_delta",
    )(*args)
    return res if ctx_out else (res[0], None)


def _pool_kernel(p_ref, w_ref, sc_ref, o_ref, y_s, *, on_grid, n_groups, gw):
    n = p_ref.shape[0]
    blk = 2 * GRID_W if on_grid else n
    period = GRID_W if on_grid else n
    ri = lax.broadcasted_iota(jnp.int32, (blk, blk), 0)
    ci = lax.broadcasted_iota(jnp.int32, (blk, blk), 1)
    pos = lax.broadcasted_iota(jnp.int32, (blk, gw), 0) % period

    for g in range(n_groups):
        size = POOL_SIZES[g]
        half = size // 2
        cols = slice(g * gw, (g + 1) * gw)
        if on_grid:
            n_rows = n // GRID_W
            for r in range(n_rows):
                lo = max(r - half, 0)
                hi = min(r + size - 1 - half, n_rows - 1)
                acc = p_ref[lo * GRID_W:(lo + 1) * GRID_W, cols].astype(F32)
                for rr in range(lo + 1, hi + 1):
                    acc = acc + p_ref[rr * GRID_W:(rr + 1) * GRID_W, cols].astype(F32)
                y_s[r * GRID_W:(r + 1) * GRID_W, :] = acc / float(hi - lo + 1)
            src = y_s
            src_cols = slice(0, gw)
        else:
            src = p_ref
            src_cols = cols
        band = ((ci >= ri - half) & (ci <= ri + size - 1 - half) & (ci // period == ri // period))
        band = jnp.where(band, 1.0, 0.0).astype(BF16)
        lo_c = jnp.maximum(pos - half, 0)
        hi_c = jnp.minimum(pos + size - 1 - half, period - 1)
        cnt = (hi_c - lo_c + 1).astype(F32)
        w_g = w_ref[g]
        scale = sc_ref[:, cols]

        n_blk = n // blk
        group = POOL_BLOCKS if n_blk % POOL_BLOCKS == 0 else 1

        def body(t, carry):
            rows = [pl.ds(pl.multiple_of((t * group + j) * blk, blk), blk) for j in range(group)]
            pieces = []
            for r in rows:
                y = src[r, src_cols].astype(F32)
                hi_p = y.astype(BF16)
                r1 = y - hi_p.astype(F32)
                mid_p = r1.astype(BF16)
                lo_p = (r1 - mid_p.astype(F32)).astype(BF16)
                pieces.append(jnp.concatenate([hi_p, mid_p, lo_p], axis=1))
            tots = [jnp.dot(band, pc, preferred_element_type=F32) for pc in pieces]
            dlts = []
            for r, tot in zip(rows, tots):
                mean = (tot[:, 0:gw] + tot[:, gw:2 * gw] + tot[:, 2 * gw:3 * gw]) / cnt
                dlts.append((mean - p_ref[r, cols].astype(F32)).astype(BF16))
            outs = [jnp.dot(dlt, w_g, preferred_element_type=F32) for dlt in dlts]
            for r, out in zip(rows, outs):
                o_ref[r, cols] = (out * scale).astype(o_ref.dtype)
            return carry

        lax.fori_loop(0, n_blk // group, body, 0)


def _pool(pm, pool_w, pool_scale, part0, on_grid):
    nb, rows, _ = pm.shape
    n_groups, gw, _ = pool_w.shape
    pw = n_groups * gw
    return pl.pallas_call(
        functools.partial(_pool_kernel, on_grid=on_grid, n_groups=n_groups, gw=gw),
        out_shape=jax.ShapeDtypeStruct((nb, rows, pw), BF16),
        grid=(nb,),
        in_specs=[pl.BlockSpec((None, rows, pw), lambda b: (b, 0, part0)),
                  pl.BlockSpec((n_groups, gw, gw), lambda b: (0, 0, 0)),
                  pl.BlockSpec((1, pw), lambda b: (0, 0))],
        out_specs=pl.BlockSpec((None, rows, pw), lambda b: (b, 0, 0)),
        scratch_shapes=[pltpu.VMEM((rows, gw), F32)],
        compiler_params=_cparams(1),
        name="pool_mixer",
    )(pm, pool_w, pool_scale)


def _outproj_kernel(lin_ref, pool_ref, x_ref, mod_ref, wa_ref, wb_ref, o_ref):
    y = (jnp.dot(lin_ref[...], wa_ref[...], preferred_element_type=F32)
         + jnp.dot(pool_ref[...], wb_ref[...], preferred_element_type=F32))
    o_ref[...] = x_ref[...] + mod_ref[2:3, :] * y


def _outproj(lin, pool, x3, mod, mod_base, w_a, w_b):
    nb, rows, d = x3.shape
    tm = min(512, rows)
    ka = w_a.shape[0]
    kb = w_b.shape[0]
    return pl.pallas_call(
        _outproj_kernel,
        out_shape=jax.ShapeDtypeStruct((nb, rows, d), F32),
        grid=(nb, rows // tm),
        in_specs=[pl.BlockSpec((None, tm, ka), lambda b, i: (b, i, 0)),
                  pl.BlockSpec((None, tm, kb), lambda b, i: (b, i, 0)),
                  pl.BlockSpec((None, tm, d), lambda b, i: (b, i, 0)),
                  pl.BlockSpec((None, N_MOD, d), lambda b, i: (mod_base + b, 0, 0)),
                  pl.BlockSpec((ka, d), lambda b, i: (0, 0)),
                  pl.BlockSpec((kb, d), lambda b, i: (0, 0))],
        out_specs=pl.BlockSpec((None, tm, d), lambda b, i: (b, i, 0)),
        compiler_params=_cparams(2),
        name="out_proj",
    )(lin, pool, x3, mod, w_a, w_b)


def _mlp_kernel(x_ref, mod_ref, nw_ref, wu_ref, wd_ref, fw_ref, o_ref, xn_ref, *, final_norm):
    f = pl.program_id(2)
    n_f = pl.num_programs(2)
    slabs = range(0, xn_ref.shape[0], ROW_SLAB)

    def step(first, last):
        wu = wu_ref[...]
        wd = wd_ref[...]
        if first:
            nw = nw_ref[...]
            shift = mod_ref[3:4, :]
            scale = mod_ref[4:5, :]
            for r in slabs:
                for q in range(r, r + ROW_SLAB, 128):
                    y = _rms(x_ref[q:q + 128, :]) * nw
                    xn_ref[q:q + 128, :] = (y * (1.0 + scale) + shift).astype(BF16)
        hs = [jnp.dot(xn_ref[r:r + ROW_SLAB, :], wu, preferred_element_type=F32) for r in slabs]
        hs = [jnp.square(jnp.maximum(h, 0.0)).astype(BF16) for h in hs]
        parts = [jnp.dot(h, wd, preferred_element_type=F32) for h in hs]
        for r, part in zip(slabs, parts):
            acc = part if first else o_ref[r:r + ROW_SLAB, :] + part
            if last:
                acc = x_ref[r:r + ROW_SLAB, :] + mod_ref[5:6, :] * acc
                if final_norm:
                    acc = _rms(acc) * fw_ref[...]
            o_ref[r:r + ROW_SLAB, :] = acc

    @pl.when(f == 0)
    def _():
        step(True, False)

    @pl.when((f > 0) & (f < n_f - 1))
    def _():
        step(False, False)

    @pl.when(f == n_f - 1)
    def _():
        step(False, True)


def _mlp(x3, mod, mod_base, nw, w_up, w_down, final_w, final_norm):
    nb, rows, d = x3.shape
    dff = w_up.shape[1]
    tm = min(1024, rows)
    tf = 512
    return pl.pallas_call(
        functools.partial(_mlp_kernel, final_norm=final_norm),
        out_shape=jax.ShapeDtypeStruct((nb, rows, d), F32),
        grid=(nb, rows // tm, dff // tf),
        in_specs=[pl.BlockSpec((None, tm, d), lambda b, i, f: (b, i, 0)),
                  pl.BlockSpec((None, N_MOD, d), lambda b, i, f: (mod_base + b, 0, 0)),
                  pl.BlockSpec((1, d), lambda b, i, f: (0, 0)),
                  pl.BlockSpec((d, tf), lambda b, i, f: (0, f)),
                  pl.BlockSpec((tf, d), lambda b, i, f: (f, 0)),
                  pl.BlockSpec((1, d), lambda b, i, f: (0, 0))],
        out_specs=pl.BlockSpec((None, tm, d), lambda b, i, f: (b, i, 0)),
        scratch_shapes=[pltpu.VMEM((tm, d), BF16)],
        compiler_params=_cparams(3),
        name="mlp",
    )(x3, mod, nw, w_up, w_down, final_w)


def _gate_layouts(w_in_l, a_log_l, dt_bias_l, lin_w, n_heads):
    d = w_in_l.shape[0]
    decay_start = 4 * lin_w
    ab = w_in_l[:, decay_start:decay_start + 2 * N_DIR * n_heads]
    ab = ab.reshape(d, 2 * N_DIR, n_heads)
    ab = jnp.transpose(ab, (0, 2, 1))
    ab = jnp.pad(ab, ((0, 0), (0, 0), (0, GATE_SLOTS - 2 * N_DIR)))
    w_ab = ab.reshape(d, n_heads * GATE_SLOTS)
    w_ab = jnp.pad(w_ab, ((0, 0), (0, 128 - n_heads * GATE_SLOTS))).astype(BF16)

    def per_head(p):
        return jnp.pad(p.T.astype(F32), ((0, 0), (0, GATE_SLOTS - N_DIR)))

    alog = per_head(a_log_l)
    dtb = per_head(dt_bias_l)
    alog_c = jnp.pad(alog.reshape(1, n_heads * GATE_SLOTS), ((0, 0), (0, 128 - n_heads * GATE_SLOTS)))
    dtb_c = jnp.pad(dtb.reshape(1, n_heads * GATE_SLOTS), ((0, 0), (0, 128 - n_heads * GATE_SLOTS)))
    alog_t = jnp.broadcast_to(alog.reshape(n_heads * GATE_SLOTS, 1), (n_heads * GATE_SLOTS, 128))
    dtb_t = jnp.broadcast_to(dtb.reshape(n_heads * GATE_SLOTS, 1), (n_heads * GATE_SLOTS, 128))
    return w_ab, alog_c, dtb_c, alog_t, dtb_t


def kernel(x, c, ctx, c_ctx, w_ada, b_ada, norm1_w, norm2_w, w_in, conv_w, a_log, dt_bias, onorm_w,
           pool_w, pool_scale, w_out, w_up, w_down, final_norm_w):
    nb, seq_x, d = x.shape
    seq_h = ctx.shape[1]
    depth = w_ada.shape[0]
    n_heads = a_log.shape[2]
    lin_w = n_heads * HEAD_DIM
    pool_width = pool_scale.shape[1]
    pool_start = 4 * lin_w + 2 * N_DIR * n_heads
    assert nb + 1 <= MOD_ROWS and seq_x % CHUNK == 0 and seq_h % CHUNK == 0
    assert n_heads % HEADS_PER_STEP == 0 and seq_x % GRID_W == 0

    c_all = jnp.concatenate([c, c_ctx[None, :], jnp.zeros((MOD_ROWS - nb - 1, d), F32)], axis=0)
    mod_all = _ada(c_all, w_ada, b_ada).reshape(depth, MOD_ROWS, N_MOD, d)

    h = ctx
    for l in range(depth):
        ctx_out = l < depth - 1
        mod = mod_all[l]
        w_main = jnp.concatenate([w_in[l, :, :4 * lin_w], w_in[l, :, pool_start:]], axis=1).astype(BF16)
        w_ab, alog_c, dtb_c, alog_t, dtb_t = _gate_layouts(w_in[l], a_log[l], dt_bias[l], lin_w, n_heads)
        n_gate_rows = n_heads * GATE_SLOTS
        nw1 = norm1_w[l].reshape(1, d)
        nw2 = norm2_w[l].reshape(1, d)
        w_oa = w_out[l, :lin_w].astype(BF16)
        w_ob = w_out[l, lin_w:].astype(BF16)
        w_u = w_up[l].astype(BF16)
        w_d = w_down[l].astype(BF16)
        pw = pool_w[l].astype(BF16)
        ps = pool_scale[l].reshape(1, pool_width)
        fw = final_norm_w.reshape(1, d)

        pxm, abx, abtx = _inproj(x, mod, 0, nw1, w_main, w_ab, n_gate_rows)
        h3 = h.reshape(1, nb * seq_h, d)
        phm, abh, abth = _inproj(h3, mod, nb, nw1, w_main, w_ab, n_gate_rows)
        phm = phm.reshape(nb, seq_h, -1)
        abh = abh.reshape(nb, seq_h, 128)
        abth = jnp.transpose(abth.reshape(-1, nb, seq_h), (1, 0, 2))

        gb, cum, cumt = _gates(abx, abtx, abh, abth, alog_c, dtb_c, alog_t, dtb_t)
        lin_x, lin_h = _delta(pxm, phm, gb, cum, cumt, conv_w[l], onorm_w[l].reshape(1, HEAD_DIM),
                              n_heads, ctx_out)
        part0 = (4 * lin_w) // pool_width
        pool_x = _pool(pxm, pw, ps, part0, True)
        x = _outproj(lin_x, pool_x, x, mod, 0, w_oa, w_ob)
        x = _mlp(x, mod, 0, nw2, w_u, w_d, fw, l == depth - 1)
        if ctx_out:
            pool_h = _pool(phm, pw, ps, part0, False)
            h3 = _outproj(lin_h.reshape(1, nb * seq_h, lin_w), pool_h.reshape(1, nb * seq_h, pool_width),
                          h3, mod, nb, w_oa, w_ob)
            h = _mlp(h3, mod, nb, nw2, w_u, w_d, fw, False).reshape(nb, seq_h, d)
    return x
```

```python
import functools
import math

import jax
import jax.numpy as jnp
from jax import lax
from jax.experimental import pallas as pl
from jax.experimental.pallas import tpu as pltpu

F32 = jnp.float32
BF16 = jnp.bfloat16

EPS = 1e-6
N_MOD = 6
HEAD_DIM = 128
N_DIR = 2
CONV_K = 5
GRID_W = 64
POOL_SIZES = (2, 4, 8, 16)
CHUNK = 128
INV_BLOCK = 16
GATE_SLOTS = 8
HEADS_PER_STEP = 2
CONV_PHASE = 4
POOL_BLOCKS = 4
PREP_CHUNKS = 3
ROW_SLAB = 256
MOD_ROWS = 24
VMEM_LIMIT = 56 * 1024 * 1024


def _silu(x):
    return x * (1.0 / (1.0 + jnp.exp(-x)))


def _softplus(x):
    return jnp.maximum(x, 0.0) + jnp.log1p(jnp.exp(-jnp.abs(x)))


def _cparams(n_axes):
    return pltpu.CompilerParams(dimension_semantics=("arbitrary",) * n_axes,
                                vmem_limit_bytes=VMEM_LIMIT)


def _ada_kernel(c_ref, w_ref, b_ref, o_ref):
    sc = _silu(c_ref[...]).astype(BF16)
    o_ref[...] = jnp.dot(sc, w_ref[...].astype(BF16), preferred_element_type=F32) + b_ref[...]


def _ada(c_all, w_ada, b_ada):
    depth, d, n = w_ada.shape
    tn = 1024
    return pl.pallas_call(
        _ada_kernel,
        out_shape=jax.ShapeDtypeStruct((depth, MOD_ROWS, n), F32),
        grid=(depth, n // tn),
        in_specs=[pl.BlockSpec((MOD_ROWS, d), lambda l, j: (0, 0)),
                  pl.BlockSpec((None, d, tn), lambda l, j: (l, 0, j)),
                  pl.BlockSpec((None, 1, tn), lambda l, j: (l, 0, j))],
        out_specs=pl.BlockSpec((None, MOD_ROWS, tn), lambda l, j: (l, 0, j)),
        compiler_params=_cparams(2),
        name="ada_mod",
    )(c_all, w_ada, b_ada.reshape(depth, 1, n))


def _rms(x):
    return x * lax.rsqrt(jnp.mean(x * x, axis=-1, keepdims=True) + EPS)


def _norm_mod_to(x_ref, nw_ref, mod_ref, i_shift, i_scale, out_ref, rows=128):
    nw = nw_ref[...]
    shift = mod_ref[i_shift:i_shift + 1, :]
    scale = mod_ref[i_scale:i_scale + 1, :]

    def body(r, carry):
        sl = pl.ds(pl.multiple_of(r * rows, rows), rows)
        y = _rms(x_ref[sl, :]) * nw
        out_ref[sl, :] = (y * (1.0 + scale) + shift).astype(out_ref.dtype)
        return carry

    lax.fori_loop(0, x_ref.shape[0] // rows, body, 0)


def _inproj_kernel(x_ref, mod_ref, nw_ref, w_ref, wab_ref, o_ref, ab_ref, abt_ref, xn_a, xn_b):
    s = pl.program_id(0)
    j = pl.program_id(1)
    tm = x_ref.shape[0]
    n_gate_rows = abt_ref.shape[0]

    @pl.when((s == 0) & (j == 0))
    def _():
        xn_b[...] = jnp.zeros(xn_b.shape, xn_b.dtype)

    def step(xn_fill, xn_use):
        @pl.when(j == 0)
        def _():
            ab = jnp.dot(xn_use[...], wab_ref[...], preferred_element_type=F32)
            ab_ref[...] = ab
            abt_ref[...] = ab.T[0:n_gate_rows, :]

        nw = nw_ref[...]
        shift = mod_ref[0:1, :]
        scale = mod_ref[1:2, :]
        n_parts = pl.num_programs(1) - 1
        part_rows = tm // 4
        row0 = jnp.minimum(j, n_parts - 1) * part_rows
        for r in range(0, part_rows, 128):
            sl = pl.ds(pl.multiple_of(row0 + r, 128), 128)
            y = _rms(x_ref[sl, :]) * nw
            xn_fill[sl, :] = (y * (1.0 + scale) + shift).astype(BF16)

        w = w_ref[...]
        for r in range(0, tm, ROW_SLAB):
            o_ref[r:r + ROW_SLAB, :] = jnp.dot(xn_use[r:r + ROW_SLAB, :], w,
                                               preferred_element_type=F32).astype(o_ref.dtype)

    @pl.when(s % 2 == 0)
    def _():
        step(xn_a, xn_b)

    @pl.when(s % 2 == 1)
    def _():
        step(xn_b, xn_a)


def _inproj(x3, mod, mod_base, nw, w_main, layer, w_ab, n_gate_rows):
    nb, rows, d = x3.shape
    n = w_main.shape[2]
    tm = min(1024, rows)
    tn = 1024
    tpb = rows // tm
    n_tiles = nb * tpb
    assert n // tn == 5 and tm % 512 == 0

    def tile_in(s):
        t = jnp.minimum(s, n_tiles - 1)
        return t // tpb, t % tpb

    def tile_out(s):
        t = jnp.maximum(s - 1, 0)
        return t // tpb, t % tpb

    return pl.pallas_call(
        _inproj_kernel,
        out_shape=(jax.ShapeDtypeStruct((nb, rows, n), BF16),
                   jax.ShapeDtypeStruct((nb, rows, 128), F32),
                   jax.ShapeDtypeStruct((nb, n_gate_rows, rows), F32)),
        grid=(n_tiles + 1, n // tn),
        in_specs=[pl.BlockSpec((None, tm, d), lambda s, j: (*tile_in(s), 0)),
                  pl.BlockSpec((None, N_MOD, d), lambda s, j: (mod_base + tile_in(s)[0], 0, 0)),
                  pl.BlockSpec((1, d), lambda s, j: (0, 0)),
                  pl.BlockSpec((None, d, tn), lambda s, j: (layer, 0, j)),
                  pl.BlockSpec((d, 128), lambda s, j: (0, 0))],
        out_specs=(pl.BlockSpec((None, tm, tn), lambda s, j: (*tile_out(s), jnp.where(s == 0, 0, j))),
                   pl.BlockSpec((None, tm, 128), lambda s, j: (*tile_out(s), 0)),
                   pl.BlockSpec((None, n_gate_rows, tm), lambda s, j: (tile_out(s)[0], 0, tile_out(s)[1]))),
        scratch_shapes=[pltpu.VMEM((tm, d), BF16), pltpu.VMEM((tm, d), BF16)],
        compiler_params=_cparams(2),
        name="in_proj",
    )(x3, mod, nw, w_main, w_ab)


def _split3(x):
    hi = x.astype(BF16)
    r1 = x - hi.astype(F32)
    mid = r1.astype(BF16)
    lo = (r1 - mid.astype(F32)).astype(BF16)
    return hi, mid, lo


def _gate_kernel(abx_ref, abtx_ref, abh_ref, abth_ref, alog_ref, dtb_ref, alogt_ref, dtbt_ref,
                 gb_ref, cum_ref, cumt_ref, *, seq_x, seq_h):
    c = CHUNK
    n_gt = abtx_ref.shape[0]
    slot_l = lax.broadcasted_iota(jnp.int32, (c, 128), 1) % GATE_SLOTS
    row_slot = lax.broadcasted_iota(jnp.int32, (n_gt, c), 0) % GATE_SLOTS
    ri = lax.broadcasted_iota(jnp.int32, (c, c), 0)
    ci = lax.broadcasted_iota(jnp.int32, (c, c), 1)
    m_le = jnp.where(ci <= ri, 1.0, 0.0).astype(BF16)
    m_ge = jnp.where(ci >= ri, 1.0, 0.0).astype(BF16)
    alog_r = alog_ref[...]
    dtb_r = dtb_ref[...]
    alogt = alogt_ref[...]
    dtbt = dtbt_ref[...]

    def part(ab_ref, abt_ref, row0, n_rows):
        def body(r, carry):
            src = pl.ds(pl.multiple_of(r * c, c), c)
            dst = pl.ds(pl.multiple_of(row0 + r * c, c), c)
            a = ab_ref[src, :]
            g = -jnp.exp(alog_r) * _softplus(a + dtb_r)
            gb = jnp.where(slot_l < N_DIR, g, 1.0 / (1.0 + jnp.exp(-a)))
            gb_ref[dst, :] = gb
            pieces = jnp.concatenate(_split3(gb), axis=1)
            pre = jnp.dot(m_le, pieces, preferred_element_type=F32)
            suf = jnp.dot(m_ge, pieces, preferred_element_type=F32)
            pre = pre[:, 0:128] + pre[:, 128:256] + pre[:, 256:384]
            suf = suf[:, 0:128] + suf[:, 128:256] + suf[:, 256:384]
            cum_ref[dst, :] = jnp.where(slot_l == 0, pre, suf)
            gt = -jnp.exp(alogt) * _softplus(abt_ref[:, src] + dtbt)
            pieces_t = jnp.concatenate(_split3(gt), axis=0)
            pre_t = jnp.dot(pieces_t, m_ge, preferred_element_type=F32)
            suf_t = jnp.dot(pieces_t, m_le, preferred_element_type=F32)
            pre_t = pre_t[0:n_gt] + pre_t[n_gt:2 * n_gt] + pre_t[2 * n_gt:3 * n_gt]
            suf_t = suf_t[0:n_gt] + suf_t[n_gt:2 * n_gt] + suf_t[2 * n_gt:3 * n_gt]
            cumt_ref[:, dst] = jnp.where(row_slot == 0, pre_t, suf_t)
            return carry

        lax.fori_loop(0, n_rows // c, body, 0, unroll=2)

    part(abh_ref, abth_ref, 0, seq_h)
    part(abx_ref, abtx_ref, seq_h, seq_x)


def _gates(abx, abtx, abh, abth, alog_c, dtb_c, alog_t, dtb_t):
    nb, seq_x, _ = abx.shape
    seq_h = abh.shape[1]
    n_gt = abtx.shape[1]
    seq_t = seq_x + seq_h
    return pl.pallas_call(
        functools.partial(_gate_kernel, seq_x=seq_x, seq_h=seq_h),
        out_shape=(jax.ShapeDtypeStruct((nb, seq_t, 128), F32),
                   jax.ShapeDtypeStruct((nb, seq_t, 128), F32),
                   jax.ShapeDtypeStruct((nb, n_gt, seq_t), F32)),
        grid=(nb,),
        in_specs=[pl.BlockSpec((None, seq_x, 128), lambda b: (b, 0, 0)),
                  pl.BlockSpec((None, n_gt, seq_x), lambda b: (b, 0, 0)),
                  pl.BlockSpec((None, seq_h, 128), lambda b: (b, 0, 0)),
                  pl.BlockSpec((None, n_gt, seq_h), lambda b: (b, 0, 0)),
                  pl.BlockSpec((1, 128), lambda b: (0, 0)),
                  pl.BlockSpec((1, 128), lambda b: (0, 0)),
                  pl.BlockSpec((n_gt, 128), lambda b: (0, 0)),
                  pl.BlockSpec((n_gt, 128), lambda b: (0, 0))],
        out_specs=(pl.BlockSpec((None, seq_t, 128), lambda b: (b, 0, 0)),
                   pl.BlockSpec((None, seq_t, 128), lambda b: (b, 0, 0)),
                   pl.BlockSpec((None, n_gt, seq_t), lambda b: (b, 0, 0))),
        compiler_params=_cparams(1),
        name="gates",
    )(abx, abtx, abh, abth, alog_c, dtb_c, alog_t, dtb_t)


def _delta_kernel(*refs, seq_x, seq_h, hg, ctx_out):
    n_in = 14
    (qx_ref, kx_ref, vx_ref, zx_ref, qh_ref, kh_ref, vh_ref, zh_ref,
     cq_ref, ck_ref, cv_ref, gb_ref, cum_ref, cumt_ref) = refs[:n_in]
    onorm_ref = refs[n_in]
    if ctx_out:
        ox_ref, oh_ref = refs[n_in + 1:n_in + 3]
        scratch = refs[n_in + 3:]
    else:
        ox_ref = refs[n_in + 1]
        oh_ref = None
        scratch = refs[n_in + 2:]
    (padq_s, padk_s, padv_s, q_s, k_s, v_s, o_s, u_s, wq_s, ktt_s, at_s, egl_s, st_s) = scratch

    c = CHUNK
    hd = HEAD_DIM
    width = hg * hd
    seq_t = seq_h + seq_x
    nc_h = seq_h // c
    nc_x = seq_x // c
    nc = nc_h + nc_x
    n_chain = hg * N_DIR

    zero8 = jnp.zeros((8, hd), F32)
    for pad, h_ref, x_ref in ((padq_s, qh_ref, qx_ref), (padk_s, kh_ref, kx_ref), (padv_s, vh_ref, vx_ref)):
        for hl in range(hg):
            cols = slice(hl * hd, (hl + 1) * hd)
            pad[hl, 0:8, :] = zero8
            pad[hl, 8:8 + seq_h, :] = h_ref[:, cols].astype(F32)
            pad[hl, 8 + seq_h:16 + seq_h, :] = zero8
            pad[hl, 16 + seq_h:16 + seq_t, :] = x_ref[:, cols].astype(F32)
            pad[hl, 16 + seq_t:24 + seq_t, :] = zero8
    conv_taps = [[[w_ref[i:i + 1, hl * hd:(hl + 1) * hd] for i in range(CONV_K)] for hl in range(hg)]
                 for w_ref in (cq_ref, ck_ref, cv_ref)]

    def conv_chunk(ch):
        base = ch * c + jnp.where(ch >= nc_h, 16, 8)
        n_ph = c // CONV_PHASE
        for pad, taps, dst, mode in ((padq_s, conv_taps[0], q_s, "q"), (padk_s, conv_taps[1], k_s, "k"),
                                     (padv_s, conv_taps[2], v_s, "v")):
            for hl in range(hg):
                shifted = {off: pad[hl, pl.ds(base + off, n_ph, stride=CONV_PHASE), :]
                           for off in range(-(CONV_K // 2), CONV_PHASE + CONV_K // 2)}
                for ph in range(CONV_PHASE):
                    acc = None
                    for i in range(CONV_K):
                        term = shifted[ph + i - CONV_K // 2] * taps[hl][i]
                        acc = term if acc is None else acc + term
                    y = _silu(acc)
                    if mode != "v":
                        y = y * lax.rsqrt(jnp.sum(y * y, axis=-1, keepdims=True) + EPS)
                        if mode == "q":
                            y = y * (hd ** -0.5)
                    dst[hl, pl.ds(ch * c + ph, n_ph, stride=CONV_PHASE), :] = y

    shift = (128 - pl.program_id(1) * (hg * GATE_SLOTS)) % 128
    ri = lax.broadcasted_iota(jnp.int32, (c, c), 0)
    ci = lax.broadcasted_iota(jnp.int32, (c, c), 1)
    lower_incl = (ci <= ri)
    upper_incl = (ci >= ri)

    eye = jnp.where(ri == ci, 1.0, 0.0).astype(F32)
    masks = ((lower_incl, ci < ri), (upper_incl, ci > ri))

    def bdot(a, b):
        return jnp.dot(a.astype(BF16), b.astype(BF16), preferred_element_type=F32)

    same_blk = (ri // INV_BLOCK) == (ci // INV_BLOCK)

    def tri_inverse(nms, lowers):
        nds = [jnp.where(same_blk, nm, 0.0) for nm in nms]
        ms = [eye + nd for nd in nds]
        ps = [bdot(nd, nd) for nd in nds]
        for _ in range(int(math.log2(INV_BLOCK)) - 2):
            rs = [bdot(p, jnp.concatenate([m, p], axis=1)) for m, p in zip(ms, ps)]
            ms = [m + r_[:, 0:c] for m, r_ in zip(ms, rs)]
            ps = [r_[:, c:2 * c] for r_ in rs]
        rs = [bdot(p, m) for m, p in zip(ms, ps)]
        ms = [m + r_ for m, r_ in zip(ms, rs)]
        s = INV_BLOCK
        while s < c:
            starts = list(range(0, c, s))
            pair = ((ri // (2 * s)) == (ci // (2 * s))) & ((ri // s) != (ci // s))
            zero = jnp.zeros((s, c), BF16)
            hits = [[(a // s) % 2 == (1 if lower else 0) for a in starts] for lower in lowers]

            def take(xm, hit):
                return jnp.concatenate([xm[a:a + s] for a, t in zip(starts, hit) if t], axis=0)

            m_sels = [take(m, hit) for m, hit in zip(ms, hits)]
            xs = [bdot(take(jnp.where(pair, nm, 0.0), hit), m).astype(BF16)
                  for nm, m, hit in zip(nms, ms, hits)]
            ys = []
            for x, m_sel, hit in zip(xs, m_sels, hits):
                slabs, k_ = [], 0
                for t in hit:
                    slabs.append(x[k_ * s:(k_ + 1) * s] if t else zero)
                    k_ += 1 if t else 0
                ys.append(jnp.dot(m_sel.astype(BF16), jnp.concatenate(slabs, axis=0),
                                  preferred_element_type=F32))
            new_ms = []
            for m, m_sel, y, hit in zip(ms, m_sels, ys, hits):
                new_sel = m_sel + y
                slabs, k_ = [], 0
                for a, t in zip(starts, hit):
                    slabs.append(new_sel[k_ * s:(k_ + 1) * s] if t else m[a:a + s])
                    k_ += 1 if t else 0
                new_ms.append(jnp.concatenate(slabs, axis=0))
            ms = new_ms
            s *= 2
        return ms

    n_prep = PREP_CHUNKS if nc % PREP_CHUNKS == 0 else 1

    def prep(grp, carry):
        chs = [grp * n_prep + j for j in range(n_prep)]
        rows = [pl.ds(pl.multiple_of(ch * c, c), c) for ch in chs]
        gbc = [pltpu.roll(gb_ref[r, :], shift, 1) for r in rows]
        cumc = [pltpu.roll(cum_ref[r, :], shift, 1) for r in rows]
        cumtc = [cumt_ref[:, r] for r in rows]
        qs, ks, vs, aas = {}, {}, {}, {}
        for j in range(n_prep):
            for hl in range(hg):
                cols = slice(hl * hd, (hl + 1) * hd)
                qs[j, hl] = q_s[hl, rows[j], :]
                ks[j, hl] = k_s[hl, rows[j], :]
                vs[j, hl] = v_s[hl, rows[j], :]
                qk = jnp.concatenate([qs[j, hl], ks[j, hl]], axis=0).astype(BF16)
                aas[j, hl] = lax.dot_general(qk, ks[j, hl].astype(BF16), (((1,), (1,)), ((), ())),
                                             preferred_element_type=F32)
        probs = [(j, hl, d) for j in range(n_prep) for hl in range(hg) for d in range(N_DIR)]
        nms, betas, cums, decays = [], [], [], []
        for j, hl, d in probs:
            incl, strict = masks[d]
            base = hl * GATE_SLOTS
            beta_b = jnp.broadcast_to(gbc[j][:, base + N_DIR + d:base + N_DIR + d + 1], (c, hd))
            cum_b = jnp.broadcast_to(cumc[j][:, base + d:base + d + 1], (c, hd))
            cum_r = cumtc[j][base + d:base + d + 1, :]
            decay = jnp.where(incl, jnp.exp(jnp.where(incl, cum_b - cum_r, 0.0)), 0.0)
            nms.append(jnp.where(strict, -(beta_b * aas[j, hl][c:2 * c] * decay), 0.0))
            betas.append(beta_b)
            cums.append(cum_b)
            decays.append(decay)
        ms = tri_inverse(nms, [d == 0 for _, _, d in probs])
        e_cs = [jnp.exp(cum_b) for cum_b in cums]
        uws = [bdot(m, jnp.concatenate([vs[j, hl] * beta_b, ks[j, hl] * (beta_b * e_c)], axis=1))
               for (j, hl, d), m, beta_b, e_c in zip(probs, ms, betas, e_cs)]
        for i, (j, hl, d) in enumerate(probs):
            cum_b, e_c, uw = cums[i], e_cs[i], uws[i]
            gl_b = cum_b[c - 1:c, :] if d == 0 else cum_b[0:1, :]
            kt = ks[j, hl] * jnp.exp(gl_b - cum_b)
            slot = (hl * N_DIR + d) * nc + chs[j]
            u_s[pl.ds(pl.multiple_of(slot * c, c), c), :] = uw[:, 0:hd]
            wq_s[pl.ds(pl.multiple_of(slot * 2 * c, c), c), :] = uw[:, hd:2 * hd].astype(BF16)
            wq_s[pl.ds(pl.multiple_of(slot * 2 * c + c, c), c), :] = (qs[j, hl] * e_c).astype(BF16)
            ktt_s[pl.ds(pl.multiple_of(slot * hd, hd), hd), :] = kt.T.astype(BF16)
            at_s[pl.ds(pl.multiple_of(slot * c, c), c), :] = (aas[j, hl][0:c] * decays[i]).astype(BF16)
            egl_s[pl.ds(pl.multiple_of(slot * 8, 8), 8), :] = jnp.broadcast_to(jnp.exp(gl_b), (8, hd))
        nxt = jnp.minimum(grp + 1, n_groups_prep - 1)
        for j in range(n_prep):
            conv_chunk(nxt * n_prep + j)
        return carry

    n_groups_prep = nc // n_prep
    for j in range(n_prep):
        conv_chunk(j)
    lax.fori_loop(0, n_groups_prep, prep, 0)

    st_s[...] = jnp.zeros(st_s.shape, F32)
    o_s[...] = jnp.zeros(o_s.shape, F32)

    def scan_step(ch_f, ch_b, with_q):
        chains = [(hl, d) for hl in range(hg) for d in range(N_DIR)]
        chs = [ch_f if d == 0 else ch_b for _, d in chains]
        slots = [i * nc + ch for i, ch in enumerate(chs)]
        s_olds = [st_s[i * hd:(i + 1) * hd, :] for i in range(len(chains))]
        n_lhs = 2 * c if with_q else c
        rs = [jnp.dot(wq_s[pl.ds(pl.multiple_of(slot * 2 * c, 2 * c), n_lhs), :], s_old.astype(BF16),
                      preferred_element_type=F32) for slot, s_old in zip(slots, s_olds)]
        v_bs = [(u_s[pl.ds(pl.multiple_of(slot * c, c), c), :] - r_[0:c]).astype(BF16)
                for slot, r_ in zip(slots, rs)]
        kvs = [jnp.dot(ktt_s[pl.ds(pl.multiple_of(slot * hd, hd), hd), :], v_b, preferred_element_type=F32)
               for slot, v_b in zip(slots, v_bs)]
        if with_q:
            avs = [jnp.dot(at_s[pl.ds(pl.multiple_of(slot * c, c), c), :], v_b, preferred_element_type=F32)
                   for slot, v_b in zip(slots, v_bs)]
        for i, (hl, d) in enumerate(chains):
            egl = egl_s[pl.ds(pl.multiple_of(slots[i] * 8, 8), 1), :]
            st_s[i * hd:(i + 1) * hd, :] = s_olds[i] * egl + kvs[i]
            if with_q:
                orow = pl.ds(pl.multiple_of(chs[i] * c, c), c)
                ocol = slice(hl * hd, (hl + 1) * hd)
                o_s[orow, ocol] = o_s[orow, ocol] + (rs[i][c:2 * c] + avs[i])

    def scan_h(it, carry):
        scan_step(it, nc_h - 1 - it, ctx_out)
        return carry

    def scan_x(it, carry):
        scan_step(nc_h + it, nc - 1 - it, True)
        return carry

    lax.fori_loop(0, nc_h, scan_h, 0)
    lax.fori_loop(0, nc_x, scan_x, 0)

    onw = onorm_ref[...]

    def finish(z_ref, out_ref, row0, n_rows):
        tile = 256 if n_rows % 256 == 0 else 128

        def body(r, carry):
            start = pl.multiple_of(r * tile, tile)
            src = pl.ds(pl.multiple_of(row0 + start, CHUNK), tile)
            dst = pl.ds(start, tile)
            outs = []
            for hl in range(hg):
                cols = slice(hl * hd, (hl + 1) * hd)
                o = o_s[src, cols]
                outs.append(_rms(o) * onw * _silu(z_ref[dst, cols].astype(F32)))
            y = outs[0] if hg == 1 else jnp.concatenate(outs, axis=1)
            out_ref[dst, :] = y.astype(out_ref.dtype)
            return carry

        lax.fori_loop(0, n_rows // tile, body, 0)

    finish(zx_ref, ox_ref, seq_h, seq_x)
    if ctx_out:
        finish(zh_ref, oh_ref, 0, seq_h)


def _delta(pxm, phm, gb, cum, cumt, conv_w, onorm, n_heads, ctx_out):
    nb, seq_x, _ = pxm.shape
    seq_h = phm.shape[1]
    hg = HEADS_PER_STEP
    hd = HEAD_DIM
    width = hg * hd
    n_groups = n_heads // hg
    lin_w = n_heads * hd
    seq_t = seq_x + seq_h
    nc = seq_t // CHUNK
    n_chain = hg * N_DIR
    n_gt = hg * GATE_SLOTS

    def main_spec(rows, part):
        return pl.BlockSpec((None, rows, width), lambda b, g: (b, 0, part * n_groups + g))

    in_specs = (
        [main_spec(seq_x, p) for p in range(4)]
        + [main_spec(seq_h, p) for p in range(4)]
        + [pl.BlockSpec((CONV_K, width), lambda b, g, p=p: (0, p * n_groups + g)) for p in range(3)]
        + [pl.BlockSpec((None, seq_t, 128), lambda b, g: (b, 0, 0)),
           pl.BlockSpec((None, seq_t, 128), lambda b, g: (b, 0, 0)),
           pl.BlockSpec((None, n_gt, seq_t), lambda b, g: (b, g, 0)),
           pl.BlockSpec((1, hd), lambda b, g: (0, 0))]
    )
    out_shape = [jax.ShapeDtypeStruct((nb, seq_x, lin_w), BF16)]
    out_specs = [pl.BlockSpec((None, seq_x, width), lambda b, g: (b, 0, g))]
    if ctx_out:
        out_shape.append(jax.ShapeDtypeStruct((nb, seq_h, lin_w), BF16))
        out_specs.append(pl.BlockSpec((None, seq_h, width), lambda b, g: (b, 0, g)))
    scratch = [
        pltpu.VMEM((hg, seq_t + 24, hd), F32),
        pltpu.VMEM((hg, seq_t + 24, hd), F32),
        pltpu.VMEM((hg, seq_t + 24, hd), F32),
        pltpu.VMEM((hg, seq_t, hd), F32),
        pltpu.VMEM((hg, seq_t, hd), F32),
        pltpu.VMEM((hg, seq_t, hd), F32),
        pltpu.VMEM((seq_t, width), F32),
        pltpu.VMEM((n_chain * nc * CHUNK, hd), F32),
        pltpu.VMEM((n_chain * nc * 2 * CHUNK, hd), BF16),
        pltpu.VMEM((n_chain * nc * hd, CHUNK), BF16),
        pltpu.VMEM((n_chain * nc * CHUNK, CHUNK), BF16),
        pltpu.VMEM((n_chain * nc * 8, hd), F32),
        pltpu.VMEM((n_chain * hd, hd), F32),
    ]
    args = [pxm] * 4 + [phm] * 4 + [conv_w] * 3 + [gb, cum, cumt, onorm]
    res = pl.pallas_call(
        functools.partial(_delta_kernel, seq_x=seq_x, seq_h=seq_h, hg=hg, ctx_out=ctx_out),
        out_shape=tuple(out_shape),
        grid=(nb, n_groups),
        in_specs=in_specs,
        out_specs=tuple(out_specs),
        scratch_shapes=scratch,
        compiler_params=_cparams(2),
        name="gated_delta",
    )(*args)
    return res if ctx_out else (res[0], None)


def _pool_kernel(p_ref, w_ref, sc_ref, o_ref, y_s, *, on_grid, n_groups, gw):
    n = p_ref.shape[0]
    blk = 2 * GRID_W if on_grid else n
    period = GRID_W if on_grid else n
    ri = lax.broadcasted_iota(jnp.int32, (blk, blk), 0)
    ci = lax.broadcasted_iota(jnp.int32, (blk, blk), 1)
    pos = lax.broadcasted_iota(jnp.int32, (blk, gw), 0) % period

    for g in range(n_groups):
        size = POOL_SIZES[g]
        half = size // 2
        cols = slice(g * gw, (g + 1) * gw)
        if on_grid:
            n_rows = n // GRID_W
            for r in range(n_rows):
                lo = max(r - half, 0)
                hi = min(r + size - 1 - half, n_rows - 1)
                acc = p_ref[lo * GRID_W:(lo + 1) * GRID_W, cols].astype(F32)
                for rr in range(lo + 1, hi + 1):
                    acc = acc + p_ref[rr * GRID_W:(rr + 1) * GRID_W, cols].astype(F32)
                y_s[r * GRID_W:(r + 1) * GRID_W, :] = acc / float(hi - lo + 1)
            src = y_s
            src_cols = slice(0, gw)
        else:
            src = p_ref
            src_cols = cols
        band = ((ci >= ri - half) & (ci <= ri + size - 1 - half) & (ci // period == ri // period))
        band = jnp.where(band, 1.0, 0.0).astype(BF16)
        lo_c = jnp.maximum(pos - half, 0)
        hi_c = jnp.minimum(pos + size - 1 - half, period - 1)
        cnt = (hi_c - lo_c + 1).astype(F32)
        w_g = w_ref[g]
        scale = sc_ref[:, cols]

        n_blk = n // blk
        group = POOL_BLOCKS if n_blk % POOL_BLOCKS == 0 else 1

        def body(t, carry):
            rows = [pl.ds(pl.multiple_of((t * group + j) * blk, blk), blk) for j in range(group)]
            pieces = []
            for r in rows:
                y = src[r, src_cols].astype(F32)
                hi_p = y.astype(BF16)
                r1 = y - hi_p.astype(F32)
                mid_p = r1.astype(BF16)
                lo_p = (r1 - mid_p.astype(F32)).astype(BF16)
                pieces.append(jnp.concatenate([hi_p, mid_p, lo_p], axis=1))
            tots = [jnp.dot(band, pc, preferred_element_type=F32) for pc in pieces]
            dlts = []
            for r, tot in zip(rows, tots):
                mean = (tot[:, 0:gw] + tot[:, gw:2 * gw] + tot[:, 2 * gw:3 * gw]) / cnt
                dlts.append((mean - p_ref[r, cols].astype(F32)).astype(BF16))
            outs = [jnp.dot(dlt, w_g, preferred_element_type=F32) for dlt in dlts]
            for r, out in zip(rows, outs):
                o_ref[r, cols] = (out * scale).astype(o_ref.dtype)
            return carry

        lax.fori_loop(0, n_blk // group, body, 0)


def _pool(pm, pool_w, pool_scale, part0, on_grid):
    nb, rows, _ = pm.shape
    n_groups, gw, _ = pool_w.shape
    pw = n_groups * gw
    return pl.pallas_call(
        functools.partial(_pool_kernel, on_grid=on_grid, n_groups=n_groups, gw=gw),
        out_shape=jax.ShapeDtypeStruct((nb, rows, pw), BF16),
        grid=(nb,),
        in_specs=[pl.BlockSpec((None, rows, pw), lambda b: (b, 0, part0)),
                  pl.BlockSpec((n_groups, gw, gw), lambda b: (0, 0, 0)),
                  pl.BlockSpec((1, pw), lambda b: (0, 0))],
        out_specs=pl.BlockSpec((None, rows, pw), lambda b: (b, 0, 0)),
        scratch_shapes=[pltpu.VMEM((rows, gw), F32)],
        compiler_params=_cparams(1),
        name="pool_mixer",
    )(pm, pool_w, pool_scale)


def _outproj_kernel(lin_ref, pool_ref, x_ref, mod_ref, wa_ref, wb_ref, o_ref):
    y = (jnp.dot(lin_ref[...], wa_ref[...], preferred_element_type=F32)
         + jnp.dot(pool_ref[...], wb_ref[...], preferred_element_type=F32))
    o_ref[...] = x_ref[...] + mod_ref[2:3, :] * y


def _outproj(lin, pool, x3, mod, mod_base, w_out_all, layer):
    nb, rows, d = x3.shape
    tm = min(512, rows)
    ka = lin.shape[2]
    kb = pool.shape[2]
    assert ka == kb and w_out_all.shape[1] == ka + kb
    return pl.pallas_call(
        _outproj_kernel,
        out_shape=jax.ShapeDtypeStruct((nb, rows, d), F32),
        grid=(nb, rows // tm),
        in_specs=[pl.BlockSpec((None, tm, ka), lambda b, i: (b, i, 0)),
                  pl.BlockSpec((None, tm, kb), lambda b, i: (b, i, 0)),
                  pl.BlockSpec((None, tm, d), lambda b, i: (b, i, 0)),
                  pl.BlockSpec((None, N_MOD, d), lambda b, i: (mod_base + b, 0, 0)),
                  pl.BlockSpec((None, ka, d), lambda b, i: (layer, 0, 0)),
                  pl.BlockSpec((None, kb, d), lambda b, i: (layer, 1, 0))],
        out_specs=pl.BlockSpec((None, tm, d), lambda b, i: (b, i, 0)),
        compiler_params=_cparams(2),
        name="out_proj",
    )(lin, pool, x3, mod, w_out_all, w_out_all)


def _mlp_kernel(x_ref, mod_ref, nw_ref, wu_ref, wd_ref, fw_ref, o_ref, xn_ref, *, final_norm):
    f = pl.program_id(2)
    n_f = pl.num_programs(2)
    slabs = range(0, xn_ref.shape[0], ROW_SLAB)

    def step(first, last):
        wu = wu_ref[...]
        wd = wd_ref[...]
        if first:
            nw = nw_ref[...]
            shift = mod_ref[3:4, :]
            scale = mod_ref[4:5, :]
            for r in slabs:
                for q in range(r, r + ROW_SLAB, 128):
                    y = _rms(x_ref[q:q + 128, :]) * nw
                    xn_ref[q:q + 128, :] = (y * (1.0 + scale) + shift).astype(BF16)
        hs = [jnp.dot(xn_ref[r:r + ROW_SLAB, :], wu, preferred_element_type=F32) for r in slabs]
        hs = [jnp.square(jnp.maximum(h, 0.0)).astype(BF16) for h in hs]
        parts = [jnp.dot(h, wd, preferred_element_type=F32) for h in hs]
        for r, part in zip(slabs, parts):
            acc = part if first else o_ref[r:r + ROW_SLAB, :] + part
            if last:
                acc = x_ref[r:r + ROW_SLAB, :] + mod_ref[5:6, :] * acc
                if final_norm:
                    acc = _rms(acc) * fw_ref[...]
            o_ref[r:r + ROW_SLAB, :] = acc

    @pl.when(f == 0)
    def _():
        step(True, False)

    @pl.when((f > 0) & (f < n_f - 1))
    def _():
        step(False, False)

    @pl.when(f == n_f - 1)
    def _():
        step(False, True)


def _mlp(x3, mod, mod_base, nw, w_up, w_down, layer, final_w, final_norm):
    nb, rows, d = x3.shape
    dff = w_up.shape[2]
    tm = min(1024, rows)
    tf = 512
    return pl.pallas_call(
        functools.partial(_mlp_kernel, final_norm=final_norm),
        out_shape=jax.ShapeDtypeStruct((nb, rows, d), F32),
        grid=(nb, rows // tm, dff // tf),
        in_specs=[pl.BlockSpec((None, tm, d), lambda b, i, f: (b, i, 0)),
                  pl.BlockSpec((None, N_MOD, d), lambda b, i, f: (mod_base + b, 0, 0)),
                  pl.BlockSpec((1, d), lambda b, i, f: (0, 0)),
                  pl.BlockSpec((None, d, tf), lambda b, i, f: (layer, 0, f)),
                  pl.BlockSpec((None, tf, d), lambda b, i, f: (layer, f, 0)),
                  pl.BlockSpec((1, d), lambda b, i, f: (0, 0))],
        out_specs=pl.BlockSpec((None, tm, d), lambda b, i, f: (b, i, 0)),
        scratch_shapes=[pltpu.VMEM((tm, d), BF16)],
        compiler_params=_cparams(3),
        name="mlp",
    )(x3, mod, nw, w_up, w_down, final_w)


def _gate_layouts(w_in_l, a_log_l, dt_bias_l, lin_w, n_heads):
    d = w_in_l.shape[0]
    decay_start = 4 * lin_w
    ab = w_in_l[:, decay_start:decay_start + 2 * N_DIR * n_heads]
    ab = ab.reshape(d, 2 * N_DIR, n_heads)
    ab = jnp.transpose(ab, (0, 2, 1))
    ab = jnp.pad(ab, ((0, 0), (0, 0), (0, GATE_SLOTS - 2 * N_DIR)))
    w_ab = ab.reshape(d, n_heads * GATE_SLOTS)
    w_ab = jnp.pad(w_ab, ((0, 0), (0, 128 - n_heads * GATE_SLOTS))).astype(BF16)

    def per_head(p):
        return jnp.pad(p.T.astype(F32), ((0, 0), (0, GATE_SLOTS - N_DIR)))

    alog = per_head(a_log_l)
    dtb = per_head(dt_bias_l)
    alog_c = jnp.pad(alog.reshape(1, n_heads * GATE_SLOTS), ((0, 0), (0, 128 - n_heads * GATE_SLOTS)))
    dtb_c = jnp.pad(dtb.reshape(1, n_heads * GATE_SLOTS), ((0, 0), (0, 128 - n_heads * GATE_SLOTS)))
    alog_t = jnp.broadcast_to(alog.reshape(n_heads * GATE_SLOTS, 1), (n_heads * GATE_SLOTS, 128))
    dtb_t = jnp.broadcast_to(dtb.reshape(n_heads * GATE_SLOTS, 1), (n_heads * GATE_SLOTS, 128))
    return w_ab, alog_c, dtb_c, alog_t, dtb_t


def kernel(x, c, ctx, c_ctx, w_ada, b_ada, norm1_w, norm2_w, w_in, conv_w, a_log, dt_bias, onorm_w,
           pool_w, pool_scale, w_out, w_up, w_down, final_norm_w):
    nb, seq_x, d = x.shape
    seq_h = ctx.shape[1]
    depth = w_ada.shape[0]
    n_heads = a_log.shape[2]
    lin_w = n_heads * HEAD_DIM
    pool_width = pool_scale.shape[1]
    pool_start = 4 * lin_w + 2 * N_DIR * n_heads
    assert nb + 1 <= MOD_ROWS and seq_x % CHUNK == 0 and seq_h % CHUNK == 0
    assert n_heads % HEADS_PER_STEP == 0 and seq_x % GRID_W == 0

    c_all = jnp.concatenate([c, c_ctx[None, :], jnp.zeros((MOD_ROWS - nb - 1, d), F32)], axis=0)
    mod_all = _ada(c_all, w_ada, b_ada).reshape(depth, MOD_ROWS, N_MOD, d)

    w_main = jnp.concatenate([w_in[:, :, :4 * lin_w], w_in[:, :, pool_start:]], axis=2).astype(BF16)
    w_o = w_out.astype(BF16)
    w_u = w_up.astype(BF16)
    w_d = w_down.astype(BF16)
    fw = final_norm_w.reshape(1, d)

    h = ctx
    for l in range(depth):
        ctx_out = l < depth - 1
        mod = mod_all[l]
        w_ab, alog_c, dtb_c, alog_t, dtb_t = _gate_layouts(w_in[l], a_log[l], dt_bias[l], lin_w, n_heads)
        n_gate_rows = n_heads * GATE_SLOTS
        nw1 = norm1_w[l].reshape(1, d)
        nw2 = norm2_w[l].reshape(1, d)
        pw = pool_w[l].astype(BF16)
        ps = pool_scale[l].reshape(1, pool_width)

        pxm, abx, abtx = _inproj(x, mod, 0, nw1, w_main, l, w_ab, n_gate_rows)
        h3 = h.reshape(1, nb * seq_h, d)
        phm, abh, abth = _inproj(h3, mod, nb, nw1, w_main, l, w_ab, n_gate_rows)
        phm = phm.reshape(nb, seq_h, -1)
        abh = abh.reshape(nb, seq_h, 128)
        abth = jnp.transpose(abth.reshape(-1, nb, seq_h), (1, 0, 2))

        gb, cum, cumt = _gates(abx, abtx, abh, abth, alog_c, dtb_c, alog_t, dtb_t)
        lin_x, lin_h = _delta(pxm, phm, gb, cum, cumt, conv_w[l], onorm_w[l].reshape(1, HEAD_DIM),
                              n_heads, ctx_out)
        part0 = (4 * lin_w) // pool_width
        pool_x = _pool(pxm, pw, ps, part0, True)
        x = _outproj(lin_x, pool_x, x, mod, 0, w_o, l)
        x = _mlp(x, mod, 0, nw2, w_u, w_d, l, fw, l == depth - 1)
        if ctx_out:
            pool_h = _pool(phm, pw, ps, part0, False)
            h3 = _outproj(lin_h.reshape(1, nb * seq_h, lin_w), pool_h.reshape(1, nb * seq_h, pool_width),
                          h3, mod, nb, w_o, l)
            h = _mlp(h3, mod, nb, nw2, w_u, w_d, l, fw, False).reshape(nb, seq_h, d)
    return x
```

```python
import functools
import math

import jax
import jax.numpy as jnp
from jax import lax
from jax.experimental import pallas as pl
from jax.experimental.pallas import tpu as pltpu

F32 = jnp.float32
BF16 = jnp.bfloat16

EPS = 1e-6
N_MOD = 6
HEAD_DIM = 128
N_DIR = 2
CONV_K = 5
GRID_W = 64
POOL_SIZES = (2, 4, 8, 16)
CHUNK = 128
INV_BLOCK = 16
GATE_SLOTS = 8
HEADS_PER_STEP = 2
CONV_PHASE = 4
POOL_BLOCKS = 4
PREP_CHUNKS = 3
ROW_SLAB = 256
MOD_ROWS = 24
VMEM_LIMIT = 56 * 1024 * 1024


def _silu(x):
    return x * (1.0 / (1.0 + jnp.exp(-x)))


def _softplus(x):
    return jnp.maximum(x, 0.0) + jnp.log1p(jnp.exp(-jnp.abs(x)))


def _cparams(n_axes):
    return pltpu.CompilerParams(dimension_semantics=("arbitrary",) * n_axes,
                                vmem_limit_bytes=VMEM_LIMIT)


def _ada_kernel(c_ref, w_ref, b_ref, o_ref):
    sc = _silu(c_ref[...]).astype(BF16)
    o_ref[...] = jnp.dot(sc, w_ref[...].astype(BF16), preferred_element_type=F32) + b_ref[...]


def _ada(c_all, w_ada, b_ada):
    depth, d, n = w_ada.shape
    tn = 1024
    return pl.pallas_call(
        _ada_kernel,
        out_shape=jax.ShapeDtypeStruct((depth, MOD_ROWS, n), F32),
        grid=(depth, n // tn),
        in_specs=[pl.BlockSpec((MOD_ROWS, d), lambda l, j: (0, 0)),
                  pl.BlockSpec((None, d, tn), lambda l, j: (l, 0, j)),
                  pl.BlockSpec((None, 1, tn), lambda l, j: (l, 0, j))],
        out_specs=pl.BlockSpec((None, MOD_ROWS, tn), lambda l, j: (l, 0, j)),
        compiler_params=_cparams(2),
        name="ada_mod",
    )(c_all, w_ada, b_ada.reshape(depth, 1, n))


def _rms(x):
    return x * lax.rsqrt(jnp.mean(x * x, axis=-1, keepdims=True) + EPS)


def _norm_mod_to(x_ref, nw_ref, mod_ref, i_shift, i_scale, out_ref, rows=128):
    nw = nw_ref[...]
    shift = mod_ref[i_shift:i_shift + 1, :]
    scale = mod_ref[i_scale:i_scale + 1, :]

    def body(r, carry):
        sl = pl.ds(pl.multiple_of(r * rows, rows), rows)
        y = _rms(x_ref[sl, :]) * nw
        out_ref[sl, :] = (y * (1.0 + scale) + shift).astype(out_ref.dtype)
        return carry

    lax.fori_loop(0, x_ref.shape[0] // rows, body, 0)


def _inproj_kernel(x_ref, mod_ref, nw_ref, w_ref, wab_ref, o_ref, ab_ref, abt_ref, xn_ref):
    j = pl.program_id(2)
    tm = x_ref.shape[0]
    n_gate_rows = abt_ref.shape[0]
    slabs = range(0, tm, ROW_SLAB)

    def step(first):
        w = w_ref[...]
        if first:
            nw = nw_ref[...]
            shift = mod_ref[0:1, :]
            scale = mod_ref[1:2, :]
            for r in slabs:
                for q in range(r, r + ROW_SLAB, 128):
                    y = _rms(x_ref[q:q + 128, :]) * nw
                    xn_ref[q:q + 128, :] = (y * (1.0 + scale) + shift).astype(BF16)
            wab = wab_ref[...]
            gates = [jnp.dot(xn_ref[r:r + ROW_SLAB, :], wab, preferred_element_type=F32) for r in slabs]
        outs = [jnp.dot(xn_ref[r:r + ROW_SLAB, :], w, preferred_element_type=F32) for r in slabs]
        for r, out in zip(slabs, outs):
            o_ref[r:r + ROW_SLAB, :] = out.astype(o_ref.dtype)
        if first:
            for r, ab in zip(slabs, gates):
                ab_ref[r:r + ROW_SLAB, :] = ab
                abt_ref[:, r:r + ROW_SLAB] = ab.T[0:n_gate_rows, :]

    @pl.when(j == 0)
    def _():
        step(True)

    @pl.when(j > 0)
    def _():
        step(False)


def _inproj(x3, mod, mod_base, nw, w_main, layer, w_ab, n_gate_rows):
    nb, rows, d = x3.shape
    n = w_main.shape[2]
    tm = min(1024, rows)
    tn = 1024
    return pl.pallas_call(
        _inproj_kernel,
        out_shape=(jax.ShapeDtypeStruct((nb, rows, n), BF16),
                   jax.ShapeDtypeStruct((nb, rows, 128), F32),
                   jax.ShapeDtypeStruct((nb, n_gate_rows, rows), F32)),
        grid=(nb, rows // tm, n // tn),
        in_specs=[pl.BlockSpec((None, tm, d), lambda b, i, j: (b, i, 0)),
                  pl.BlockSpec((None, N_MOD, d), lambda b, i, j: (mod_base + b, 0, 0)),
                  pl.BlockSpec((1, d), lambda b, i, j: (0, 0)),
                  pl.BlockSpec((None, d, tn), lambda b, i, j: (layer, 0, j)),
                  pl.BlockSpec((d, 128), lambda b, i, j: (0, 0))],
        out_specs=(pl.BlockSpec((None, tm, tn), lambda b, i, j: (b, i, j)),
                   pl.BlockSpec((None, tm, 128), lambda b, i, j: (b, i, 0)),
                   pl.BlockSpec((None, n_gate_rows, tm), lambda b, i, j: (b, 0, i))),
        scratch_shapes=[pltpu.VMEM((tm, d), BF16)],
        compiler_params=_cparams(3),
        name="in_proj",
    )(x3, mod, nw, w_main, w_ab)


def _split3(x):
    hi = x.astype(BF16)
    r1 = x - hi.astype(F32)
    mid = r1.astype(BF16)
    lo = (r1 - mid.astype(F32)).astype(BF16)
    return hi, mid, lo


def _gate_kernel(abx_ref, abtx_ref, abh_ref, abth_ref, alog_ref, dtb_ref, alogt_ref, dtbt_ref,
                 gb_ref, cum_ref, cumt_ref, *, seq_x, seq_h):
    c = CHUNK
    n_gt = abtx_ref.shape[0]
    slot_l = lax.broadcasted_iota(jnp.int32, (c, 128), 1) % GATE_SLOTS
    row_slot = lax.broadcasted_iota(jnp.int32, (n_gt, c), 0) % GATE_SLOTS
    ri = lax.broadcasted_iota(jnp.int32, (c, c), 0)
    ci = lax.broadcasted_iota(jnp.int32, (c, c), 1)
    m_le = jnp.where(ci <= ri, 1.0, 0.0).astype(BF16)
    m_ge = jnp.where(ci >= ri, 1.0, 0.0).astype(BF16)
    alog_r = alog_ref[...]
    dtb_r = dtb_ref[...]
    alogt = alogt_ref[...]
    dtbt = dtbt_ref[...]

    def part(ab_ref, abt_ref, row0, n_rows):
        def body(r, carry):
            src = pl.ds(pl.multiple_of(r * c, c), c)
            dst = pl.ds(pl.multiple_of(row0 + r * c, c), c)
            a = ab_ref[src, :]
            g = -jnp.exp(alog_r) * _softplus(a + dtb_r)
            gb = jnp.where(slot_l < N_DIR, g, 1.0 / (1.0 + jnp.exp(-a)))
            gb_ref[dst, :] = gb
            pieces = jnp.concatenate(_split3(gb), axis=1)
            pre = jnp.dot(m_le, pieces, preferred_element_type=F32)
            suf = jnp.dot(m_ge, pieces, preferred_element_type=F32)
            pre = pre[:, 0:128] + pre[:, 128:256] + pre[:, 256:384]
            suf = suf[:, 0:128] + suf[:, 128:256] + suf[:, 256:384]
            cum_ref[dst, :] = jnp.where(slot_l == 0, pre, suf)
            gt = -jnp.exp(alogt) * _softplus(abt_ref[:, src] + dtbt)
            pieces_t = jnp.concatenate(_split3(gt), axis=0)
            pre_t = jnp.dot(pieces_t, m_ge, preferred_element_type=F32)
            suf_t = jnp.dot(pieces_t, m_le, preferred_element_type=F32)
            pre_t = pre_t[0:n_gt] + pre_t[n_gt:2 * n_gt] + pre_t[2 * n_gt:3 * n_gt]
            suf_t = suf_t[0:n_gt] + suf_t[n_gt:2 * n_gt] + suf_t[2 * n_gt:3 * n_gt]
            cumt_ref[:, dst] = jnp.where(row_slot == 0, pre_t, suf_t)
            return carry

        lax.fori_loop(0, n_rows // c, body, 0, unroll=2)

    part(abh_ref, abth_ref, 0, seq_h)
    part(abx_ref, abtx_ref, seq_h, seq_x)


def _gates(abx, abtx, abh, abth, alog_c, dtb_c, alog_t, dtb_t):
    nb, seq_x, _ = abx.shape
    seq_h = abh.shape[1]
    n_gt = abtx.shape[1]
    seq_t = seq_x + seq_h
    return pl.pallas_call(
        functools.partial(_gate_kernel, seq_x=seq_x, seq_h=seq_h),
        out_shape=(jax.ShapeDtypeStruct((nb, seq_t, 128), F32),
                   jax.ShapeDtypeStruct((nb, seq_t, 128), F32),
                   jax.ShapeDtypeStruct((nb, n_gt, seq_t), F32)),
        grid=(nb,),
        in_specs=[pl.BlockSpec((None, seq_x, 128), lambda b: (b, 0, 0)),
                  pl.BlockSpec((None, n_gt, seq_x), lambda b: (b, 0, 0)),
                  pl.BlockSpec((None, seq_h, 128), lambda b: (b, 0, 0)),
                  pl.BlockSpec((None, n_gt, seq_h), lambda b: (b, 0, 0)),
                  pl.BlockSpec((1, 128), lambda b: (0, 0)),
                  pl.BlockSpec((1, 128), lambda b: (0, 0)),
                  pl.BlockSpec((n_gt, 128), lambda b: (0, 0)),
                  pl.BlockSpec((n_gt, 128), lambda b: (0, 0))],
        out_specs=(pl.BlockSpec((None, seq_t, 128), lambda b: (b, 0, 0)),
                   pl.BlockSpec((None, seq_t, 128), lambda b: (b, 0, 0)),
                   pl.BlockSpec((None, n_gt, seq_t), lambda b: (b, 0, 0))),
        compiler_params=_cparams(1),
        name="gates",
    )(abx, abtx, abh, abth, alog_c, dtb_c, alog_t, dtb_t)


def _delta_kernel(*refs, seq_x, seq_h, hg, ctx_out):
    n_in = 14
    (qx_ref, kx_ref, vx_ref, zx_ref, qh_ref, kh_ref, vh_ref, zh_ref,
     cq_ref, ck_ref, cv_ref, gb_ref, cum_ref, cumt_ref) = refs[:n_in]
    onorm_ref = refs[n_in]
    if ctx_out:
        ox_ref, oh_ref = refs[n_in + 1:n_in + 3]
        scratch = refs[n_in + 3:]
    else:
        ox_ref = refs[n_in + 1]
        oh_ref = None
        scratch = refs[n_in + 2:]
    (padq_s, padk_s, padv_s, q_s, k_s, v_s, o_s, u_s, wq_s, ktt_s, at_s, egl_s, st_s) = scratch

    c = CHUNK
    hd = HEAD_DIM
    width = hg * hd
    seq_t = seq_h + seq_x
    nc_h = seq_h // c
    nc_x = seq_x // c
    nc = nc_h + nc_x
    n_chain = hg * N_DIR

    zero8 = jnp.zeros((8, hd), F32)
    for pad, h_ref, x_ref in ((padq_s, qh_ref, qx_ref), (padk_s, kh_ref, kx_ref), (padv_s, vh_ref, vx_ref)):
        for hl in range(hg):
            cols = slice(hl * hd, (hl + 1) * hd)
            pad[hl, 0:8, :] = zero8
            pad[hl, 8:8 + seq_h, :] = h_ref[:, cols].astype(F32)
            pad[hl, 8 + seq_h:16 + seq_h, :] = zero8
            pad[hl, 16 + seq_h:16 + seq_t, :] = x_ref[:, cols].astype(F32)
            pad[hl, 16 + seq_t:24 + seq_t, :] = zero8
    conv_taps = [[[w_ref[i:i + 1, hl * hd:(hl + 1) * hd] for i in range(CONV_K)] for hl in range(hg)]
                 for w_ref in (cq_ref, ck_ref, cv_ref)]

    def conv_chunk(ch):
        base = ch * c + jnp.where(ch >= nc_h, 16, 8)
        n_ph = c // CONV_PHASE
        for pad, taps, dst, mode in ((padq_s, conv_taps[0], q_s, "q"), (padk_s, conv_taps[1], k_s, "k"),
                                     (padv_s, conv_taps[2], v_s, "v")):
            for hl in range(hg):
                shifted = {off: pad[hl, pl.ds(base + off, n_ph, stride=CONV_PHASE), :]
                           for off in range(-(CONV_K // 2), CONV_PHASE + CONV_K // 2)}
                for ph in range(CONV_PHASE):
                    acc = None
                    for i in range(CONV_K):
                        term = shifted[ph + i - CONV_K // 2] * taps[hl][i]
                        acc = term if acc is None else acc + term
                    y = _silu(acc)
                    if mode != "v":
                        y = y * lax.rsqrt(jnp.sum(y * y, axis=-1, keepdims=True) + EPS)
                        if mode == "q":
                            y = y * (hd ** -0.5)
                    dst[hl, pl.ds(ch * c + ph, n_ph, stride=CONV_PHASE), :] = y

    shift = (128 - pl.program_id(1) * (hg * GATE_SLOTS)) % 128
    ri = lax.broadcasted_iota(jnp.int32, (c, c), 0)
    ci = lax.broadcasted_iota(jnp.int32, (c, c), 1)
    lower_incl = (ci <= ri)
    upper_incl = (ci >= ri)

    eye = jnp.where(ri == ci, 1.0, 0.0).astype(F32)
    masks = ((lower_incl, ci < ri), (upper_incl, ci > ri))

    def bdot(a, b):
        return jnp.dot(a.astype(BF16), b.astype(BF16), preferred_element_type=F32)

    same_blk = (ri // INV_BLOCK) == (ci // INV_BLOCK)

    def tri_inverse(nms, lowers):
        nds = [jnp.where(same_blk, nm, 0.0) for nm in nms]
        ms = [eye + nd for nd in nds]
        ps = [bdot(nd, nd) for nd in nds]
        for _ in range(int(math.log2(INV_BLOCK)) - 2):
            rs = [bdot(p, jnp.concatenate([m, p], axis=1)) for m, p in zip(ms, ps)]
            ms = [m + r_[:, 0:c] for m, r_ in zip(ms, rs)]
            ps = [r_[:, c:2 * c] for r_ in rs]
        rs = [bdot(p, m) for m, p in zip(ms, ps)]
        ms = [m + r_ for m, r_ in zip(ms, rs)]
        s = INV_BLOCK
        while s < c:
            starts = list(range(0, c, s))
            pair = ((ri // (2 * s)) == (ci // (2 * s))) & ((ri // s) != (ci // s))
            zero = jnp.zeros((s, c), BF16)
            hits = [[(a // s) % 2 == (1 if lower else 0) for a in starts] for lower in lowers]

            def take(xm, hit):
                return jnp.concatenate([xm[a:a + s] for a, t in zip(starts, hit) if t], axis=0)

            m_sels = [take(m, hit) for m, hit in zip(ms, hits)]
            xs = [bdot(take(jnp.where(pair, nm, 0.0), hit), m).astype(BF16)
                  for nm, m, hit in zip(nms, ms, hits)]
            ys = []
            for x, m_sel, hit in zip(xs, m_sels, hits):
                slabs, k_ = [], 0
                for t in hit:
                    slabs.append(x[k_ * s:(k_ + 1) * s] if t else zero)
                    k_ += 1 if t else 0
                ys.append(jnp.dot(m_sel.astype(BF16), jnp.concatenate(slabs, axis=0),
                                  preferred_element_type=F32))
            new_ms = []
            for m, m_sel, y, hit in zip(ms, m_sels, ys, hits):
                new_sel = m_sel + y
                slabs, k_ = [], 0
                for a, t in zip(starts, hit):
                    slabs.append(new_sel[k_ * s:(k_ + 1) * s] if t else m[a:a + s])
                    k_ += 1 if t else 0
                new_ms.append(jnp.concatenate(slabs, axis=0))
            ms = new_ms
            s *= 2
        return ms

    n_prep = PREP_CHUNKS if nc % PREP_CHUNKS == 0 else 1

    def prep(grp, carry):
        chs = [grp * n_prep + j for j in range(n_prep)]
        rows = [pl.ds(pl.multiple_of(ch * c, c), c) for ch in chs]
        gbc = [pltpu.roll(gb_ref[r, :], shift, 1) for r in rows]
        cumc = [pltpu.roll(cum_ref[r, :], shift, 1) for r in rows]
        cumtc = [cumt_ref[:, r] for r in rows]
        qs, ks, vs, aas = {}, {}, {}, {}
        for j in range(n_prep):
            for hl in range(hg):
                cols = slice(hl * hd, (hl + 1) * hd)
                qs[j, hl] = q_s[hl, rows[j], :]
                ks[j, hl] = k_s[hl, rows[j], :]
                vs[j, hl] = v_s[hl, rows[j], :]
                qk = jnp.concatenate([qs[j, hl], ks[j, hl]], axis=0).astype(BF16)
                aas[j, hl] = lax.dot_general(qk, ks[j, hl].astype(BF16), (((1,), (1,)), ((), ())),
                                             preferred_element_type=F32)
        probs = [(j, hl, d) for j in range(n_prep) for hl in range(hg) for d in range(N_DIR)]
        nms, betas, cums, decays = [], [], [], []
        for j, hl, d in probs:
            incl, strict = masks[d]
            base = hl * GATE_SLOTS
            beta_b = jnp.broadcast_to(gbc[j][:, base + N_DIR + d:base + N_DIR + d + 1], (c, hd))
            cum_b = jnp.broadcast_to(cumc[j][:, base + d:base + d + 1], (c, hd))
            cum_r = cumtc[j][base + d:base + d + 1, :]
            decay = jnp.where(incl, jnp.exp(jnp.where(incl, cum_b - cum_r, 0.0)), 0.0)
            nms.append(jnp.where(strict, -(beta_b * aas[j, hl][c:2 * c] * decay), 0.0))
            betas.append(beta_b)
            cums.append(cum_b)
            decays.append(decay)
        ms = tri_inverse(nms, [d == 0 for _, _, d in probs])
        e_cs = [jnp.exp(cum_b) for cum_b in cums]
        uws = [bdot(m, jnp.concatenate([vs[j, hl] * beta_b, ks[j, hl] * (beta_b * e_c)], axis=1))
               for (j, hl, d), m, beta_b, e_c in zip(probs, ms, betas, e_cs)]
        for i, (j, hl, d) in enumerate(probs):
            cum_b, e_c, uw = cums[i], e_cs[i], uws[i]
            gl_b = cum_b[c - 1:c, :] if d == 0 else cum_b[0:1, :]
            kt = ks[j, hl] * jnp.exp(gl_b - cum_b)
            slot = (hl * N_DIR + d) * nc + chs[j]
            u_s[pl.ds(pl.multiple_of(slot * c, c), c), :] = uw[:, 0:hd]
            wq_s[pl.ds(pl.multiple_of(slot * 2 * c, c), c), :] = uw[:, hd:2 * hd].astype(BF16)
            wq_s[pl.ds(pl.multiple_of(slot * 2 * c + c, c), c), :] = (qs[j, hl] * e_c).astype(BF16)
            ktt_s[pl.ds(pl.multiple_of(slot * hd, hd), hd), :] = kt.T.astype(BF16)
            at_s[pl.ds(pl.multiple_of(slot * c, c), c), :] = (aas[j, hl][0:c] * decays[i]).astype(BF16)
            egl_s[pl.ds(pl.multiple_of(slot * 8, 8), 8), :] = jnp.broadcast_to(jnp.exp(gl_b), (8, hd))
        nxt = jnp.minimum(grp + 1, n_groups_prep - 1)
        for j in range(n_prep):
            conv_chunk(nxt * n_prep + j)
        return carry

    n_groups_prep = nc // n_prep
    for j in range(n_prep):
        conv_chunk(j)
    lax.fori_loop(0, n_groups_prep, prep, 0)

    st_s[...] = jnp.zeros(st_s.shape, F32)
    o_s[...] = jnp.zeros(o_s.shape, F32)

    def scan_step(ch_f, ch_b, with_q):
        chains = [(hl, d) for hl in range(hg) for d in range(N_DIR)]
        chs = [ch_f if d == 0 else ch_b for _, d in chains]
        slots = [i * nc + ch for i, ch in enumerate(chs)]
        s_olds = [st_s[i * hd:(i + 1) * hd, :] for i in range(len(chains))]
        n_lhs = 2 * c if with_q else c
        rs = [jnp.dot(wq_s[pl.ds(pl.multiple_of(slot * 2 * c, 2 * c), n_lhs), :], s_old.astype(BF16),
                      preferred_element_type=F32) for slot, s_old in zip(slots, s_olds)]
        v_bs = [(u_s[pl.ds(pl.multiple_of(slot * c, c), c), :] - r_[0:c]).astype(BF16)
                for slot, r_ in zip(slots, rs)]
        kvs = [jnp.dot(ktt_s[pl.ds(pl.multiple_of(slot * hd, hd), hd), :], v_b, preferred_element_type=F32)
               for slot, v_b in zip(slots, v_bs)]
        if with_q:
            avs = [jnp.dot(at_s[pl.ds(pl.multiple_of(slot * c, c), c), :], v_b, preferred_element_type=F32)
                   for slot, v_b in zip(slots, v_bs)]
        for i, (hl, d) in enumerate(chains):
            egl = egl_s[pl.ds(pl.multiple_of(slots[i] * 8, 8), 1), :]
            st_s[i * hd:(i + 1) * hd, :] = s_olds[i] * egl + kvs[i]
            if with_q:
                orow = pl.ds(pl.multiple_of(chs[i] * c, c), c)
                ocol = slice(hl * hd, (hl + 1) * hd)
                o_s[orow, ocol] = o_s[orow, ocol] + (rs[i][c:2 * c] + avs[i])

    def scan_h(it, carry):
        scan_step(it, nc_h - 1 - it, ctx_out)
        return carry

    def scan_x(it, carry):
        scan_step(nc_h + it, nc - 1 - it, True)
        return carry

    lax.fori_loop(0, nc_h, scan_h, 0)
    lax.fori_loop(0, nc_x, scan_x, 0)

    onw = onorm_ref[...]

    def finish(z_ref, out_ref, row0, n_rows):
        tile = 256 if n_rows % 256 == 0 else 128

        def body(r, carry):
            start = pl.multiple_of(r * tile, tile)
            src = pl.ds(pl.multiple_of(row0 + start, CHUNK), tile)
            dst = pl.ds(start, tile)
            outs = []
            for hl in range(hg):
                cols = slice(hl * hd, (hl + 1) * hd)
                o = o_s[src, cols]
                outs.append(_rms(o) * onw * _silu(z_ref[dst, cols].astype(F32)))
            y = outs[0] if hg == 1 else jnp.concatenate(outs, axis=1)
            out_ref[dst, :] = y.astype(out_ref.dtype)
            return carry

        lax.fori_loop(0, n_rows // tile, body, 0)

    finish(zx_ref, ox_ref, seq_h, seq_x)
    if ctx_out:
        finish(zh_ref, oh_ref, 0, seq_h)


def _delta(pxm, phm, gb, cum, cumt, conv_w, onorm, n_heads, ctx_out):
    nb, seq_x, _ = pxm.shape
    seq_h = phm.shape[1]
    hg = HEADS_PER_STEP
    hd = HEAD_DIM
    width = hg * hd
    n_groups = n_heads // hg
    lin_w = n_heads * hd
    seq_t = seq_x + seq_h
    nc = seq_t // CHUNK
    n_chain = hg * N_DIR
    n_gt = hg * GATE_SLOTS

    def main_spec(rows, part):
        return pl.BlockSpec((None, rows, width), lambda b, g: (b, 0, part * n_groups + g))

    in_specs = (
        [main_spec(seq_x, p) for p in range(4)]
        + [main_spec(seq_h, p) for p in range(4)]
        + [pl.BlockSpec((CONV_K, width), lambda b, g, p=p: (0, p * n_groups + g)) for p in range(3)]
        + [pl.BlockSpec((None, seq_t, 128), lambda b, g: (b, 0, 0)),
           pl.BlockSpec((None, seq_t, 128), lambda b, g: (b, 0, 0)),
           pl.BlockSpec((None, n_gt, seq_t), lambda b, g: (b, g, 0)),
           pl.BlockSpec((1, hd), lambda b, g: (0, 0))]
    )
    out_shape = [jax.ShapeDtypeStruct((nb, seq_x, lin_w), BF16)]
    out_specs = [pl.BlockSpec((None, seq_x, width), lambda b, g: (b, 0, g))]
    if ctx_out:
        out_shape.append(jax.ShapeDtypeStruct((nb, seq_h, lin_w), BF16))
        out_specs.append(pl.BlockSpec((None, seq_h, width), lambda b, g: (b, 0, g)))
    scratch = [
        pltpu.VMEM((hg, seq_t + 24, hd), F32),
        pltpu.VMEM((hg, seq_t + 24, hd), F32),
        pltpu.VMEM((hg, seq_t + 24, hd), F32),
        pltpu.VMEM((hg, seq_t, hd), F32),
        pltpu.VMEM((hg, seq_t, hd), F32),
        pltpu.VMEM((hg, seq_t, hd), F32),
        pltpu.VMEM((seq_t, width), F32),
        pltpu.VMEM((n_chain * nc * CHUNK, hd), F32),
        pltpu.VMEM((n_chain * nc * 2 * CHUNK, hd), BF16),
        pltpu.VMEM((n_chain * nc * hd, CHUNK), BF16),
        pltpu.VMEM((n_chain * nc * CHUNK, CHUNK), BF16),
        pltpu.VMEM((n_chain * nc * 8, hd), F32),
        pltpu.VMEM((n_chain * hd, hd), F32),
    ]
    args = [pxm] * 4 + [phm] * 4 + [conv_w] * 3 + [gb, cum, cumt, onorm]
    res = pl.pallas_call(
        functools.partial(_delta_kernel, seq_x=seq_x, seq_h=seq_h, hg=hg, ctx_out=ctx_out),
        out_shape=tuple(out_shape),
        grid=(nb, n_groups),
        in_specs=in_specs,
        out_specs=tuple(out_specs),
        scratch_shapes=scratch,
        compiler_params=_cparams(2),
        name="gated_delta",
    )(*args)
    return res if ctx_out else (res[0], None)


def _pool_kernel(p_ref, w_ref, sc_ref, o_ref, y_s, *, on_grid, n_groups, gw):
    n = p_ref.shape[0]
    blk = 2 * GRID_W if on_grid else n
    period = GRID_W if on_grid else n
    ri = lax.broadcasted_iota(jnp.int32, (blk, blk), 0)
    ci = lax.broadcasted_iota(jnp.int32, (blk, blk), 1)
    pos = lax.broadcasted_iota(jnp.int32, (blk, gw), 0) % period

    for g in range(n_groups):
        size = POOL_SIZES[g]
        half = size // 2
        cols = slice(g * gw, (g + 1) * gw)
        if on_grid:
            n_rows = n // GRID_W
            def grid_row(rr):
                return p_ref[rr * GRID_W:(rr + 1) * GRID_W, cols].astype(F32)

            acc, lo_p, hi_p = None, 0, -1
            for r in range(n_rows):
                lo = max(r - half, 0)
                hi = min(r + size - 1 - half, n_rows - 1)
                for rr in range(hi_p + 1, hi + 1):
                    acc = grid_row(rr) if acc is None else acc + grid_row(rr)
                for rr in range(lo_p, lo):
                    acc = acc - grid_row(rr)
                lo_p, hi_p = lo, hi
                y_s[r * GRID_W:(r + 1) * GRID_W, :] = acc / float(hi - lo + 1)
            src = y_s
            src_cols = slice(0, gw)
        else:
            src = p_ref
            src_cols = cols
        band = ((ci >= ri - half) & (ci <= ri + size - 1 - half) & (ci // period == ri // period))
        band = jnp.where(band, 1.0, 0.0).astype(BF16)
        lo_c = jnp.maximum(pos - half, 0)
        hi_c = jnp.minimum(pos + size - 1 - half, period - 1)
        cnt = (hi_c - lo_c + 1).astype(F32)
        w_g = w_ref[g]
        scale = sc_ref[:, cols]

        n_blk = n // blk
        group = POOL_BLOCKS if n_blk % POOL_BLOCKS == 0 else 1

        def body(t, carry):
            rows = [pl.ds(pl.multiple_of((t * group + j) * blk, blk), blk) for j in range(group)]
            pieces = []
            for r in rows:
                y = src[r, src_cols].astype(F32)
                hi_p = y.astype(BF16)
                r1 = y - hi_p.astype(F32)
                mid_p = r1.astype(BF16)
                lo_p = (r1 - mid_p.astype(F32)).astype(BF16)
                pieces.append(jnp.concatenate([hi_p, mid_p, lo_p], axis=1))
            tots = [jnp.dot(band, pc, preferred_element_type=F32) for pc in pieces]
            dlts = []
            for r, tot in zip(rows, tots):
                mean = (tot[:, 0:gw] + tot[:, gw:2 * gw] + tot[:, 2 * gw:3 * gw]) / cnt
                dlts.append((mean - p_ref[r, cols].astype(F32)).astype(BF16))
            outs = [jnp.dot(dlt, w_g, preferred_element_type=F32) for dlt in dlts]
            for r, out in zip(rows, outs):
                o_ref[r, cols] = (out * scale).astype(o_ref.dtype)
            return carry

        lax.fori_loop(0, n_blk // group, body, 0)


def _pool(pm, pool_w, pool_scale, part0, on_grid):
    nb, rows, _ = pm.shape
    n_groups, gw, _ = pool_w.shape
    pw = n_groups * gw
    return pl.pallas_call(
        functools.partial(_pool_kernel, on_grid=on_grid, n_groups=n_groups, gw=gw),
        out_shape=jax.ShapeDtypeStruct((nb, rows, pw), BF16),
        grid=(nb,),
        in_specs=[pl.BlockSpec((None, rows, pw), lambda b: (b, 0, part0)),
                  pl.BlockSpec((n_groups, gw, gw), lambda b: (0, 0, 0)),
                  pl.BlockSpec((1, pw), lambda b: (0, 0))],
        out_specs=pl.BlockSpec((None, rows, pw), lambda b: (b, 0, 0)),
        scratch_shapes=[pltpu.VMEM((rows, gw), F32)],
        compiler_params=_cparams(1),
        name="pool_mixer",
    )(pm, pool_w, pool_scale)


def _outproj_kernel(lin_ref, pool_ref, x_ref, mod_ref, wa_ref, wb_ref, o_ref):
    y = (jnp.dot(lin_ref[...], wa_ref[...], preferred_element_type=F32)
         + jnp.dot(pool_ref[...], wb_ref[...], preferred_element_type=F32))
    o_ref[...] = x_ref[...] + mod_ref[2:3, :] * y


def _outproj(lin, pool, x3, mod, mod_base, w_out_all, layer):
    nb, rows, d = x3.shape
    tm = min(512, rows)
    ka = lin.shape[2]
    kb = pool.shape[2]
    assert ka == kb and w_out_all.shape[1] == ka + kb
    return pl.pallas_call(
        _outproj_kernel,
        out_shape=jax.ShapeDtypeStruct((nb, rows, d), F32),
        grid=(nb, rows // tm),
        in_specs=[pl.BlockSpec((None, tm, ka), lambda b, i: (b, i, 0)),
                  pl.BlockSpec((None, tm, kb), lambda b, i: (b, i, 0)),
                  pl.BlockSpec((None, tm, d), lambda b, i: (b, i, 0)),
                  pl.BlockSpec((None, N_MOD, d), lambda b, i: (mod_base + b, 0, 0)),
                  pl.BlockSpec((None, ka, d), lambda b, i: (layer, 0, 0)),
                  pl.BlockSpec((None, kb, d), lambda b, i: (layer, 1, 0))],
        out_specs=pl.BlockSpec((None, tm, d), lambda b, i: (b, i, 0)),
        compiler_params=_cparams(2),
        name="out_proj",
    )(lin, pool, x3, mod, w_out_all, w_out_all)


def _mlp_kernel(x_ref, mod_ref, nw_ref, wu_ref, wd_ref, fw_ref, o_ref, xn_ref, *, final_norm):
    f = pl.program_id(2)
    n_f = pl.num_programs(2)
    slabs = range(0, xn_ref.shape[0], ROW_SLAB)

    def step(first, last):
        wu = wu_ref[...]
        wd = wd_ref[...]
        if first:
            nw = nw_ref[...]
            shift = mod_ref[3:4, :]
            scale = mod_ref[4:5, :]
            for r in slabs:
                for q in range(r, r + ROW_SLAB, 128):
                    y = _rms(x_ref[q:q + 128, :]) * nw
                    xn_ref[q:q + 128, :] = (y * (1.0 + scale) + shift).astype(BF16)
        hs = [jnp.dot(xn_ref[r:r + ROW_SLAB, :], wu, preferred_element_type=F32) for r in slabs]
        hs = [jnp.square(jnp.maximum(h, 0.0)).astype(BF16) for h in hs]
        parts = [jnp.dot(h, wd, preferred_element_type=F32) for h in hs]
        for r, part in zip(slabs, parts):
            acc = part if first else o_ref[r:r + ROW_SLAB, :] + part
            if last:
                acc = x_ref[r:r + ROW_SLAB, :] + mod_ref[5:6, :] * acc
                if final_norm:
                    acc = _rms(acc) * fw_ref[...]
            o_ref[r:r + ROW_SLAB, :] = acc

    @pl.when(f == 0)
    def _():
        step(True, False)

    @pl.when((f > 0) & (f < n_f - 1))
    def _():
        step(False, False)

    @pl.when(f == n_f - 1)
    def _():
        step(False, True)


def _mlp(x3, mod, mod_base, nw, w_up, w_down, layer, final_w, final_norm):
    nb, rows, d = x3.shape
    dff = w_up.shape[2]
    tm = min(1024, rows)
    tf = 512
    return pl.pallas_call(
        functools.partial(_mlp_kernel, final_norm=final_norm),
        out_shape=jax.ShapeDtypeStruct((nb, rows, d), F32),
        grid=(nb, rows // tm, dff // tf),
        in_specs=[pl.BlockSpec((None, tm, d), lambda b, i, f: (b, i, 0)),
                  pl.BlockSpec((None, N_MOD, d), lambda b, i, f: (mod_base + b, 0, 0)),
                  pl.BlockSpec((1, d), lambda b, i, f: (0, 0)),
                  pl.BlockSpec((None, d, tf), lambda b, i, f: (layer, 0, f)),
                  pl.BlockSpec((None, tf, d), lambda b, i, f: (layer, f, 0)),
                  pl.BlockSpec((1, d), lambda b, i, f: (0, 0))],
        out_specs=pl.BlockSpec((None, tm, d), lambda b, i, f: (b, i, 0)),
        scratch_shapes=[pltpu.VMEM((tm, d), BF16)],
        compiler_params=_cparams(3),
        name="mlp",
    )(x3, mod, nw, w_up, w_down, final_w)


def _gate_layouts(w_in_l, a_log_l, dt_bias_l, lin_w, n_heads):
    d = w_in_l.shape[0]
    decay_start = 4 * lin_w
    ab = w_in_l[:, decay_start:decay_start + 2 * N_DIR * n_heads]
    ab = ab.reshape(d, 2 * N_DIR, n_heads)
    ab = jnp.transpose(ab, (0, 2, 1))
    ab = jnp.pad(ab, ((0, 0), (0, 0), (0, GATE_SLOTS - 2 * N_DIR)))
    w_ab = ab.reshape(d, n_heads * GATE_SLOTS)
    w_ab = jnp.pad(w_ab, ((0, 0), (0, 128 - n_heads * GATE_SLOTS))).astype(BF16)

    def per_head(p):
        return jnp.pad(p.T.astype(F32), ((0, 0), (0, GATE_SLOTS - N_DIR)))

    alog = per_head(a_log_l)
    dtb = per_head(dt_bias_l)
    alog_c = jnp.pad(alog.reshape(1, n_heads * GATE_SLOTS), ((0, 0), (0, 128 - n_heads * GATE_SLOTS)))
    dtb_c = jnp.pad(dtb.reshape(1, n_heads * GATE_SLOTS), ((0, 0), (0, 128 - n_heads * GATE_SLOTS)))
    alog_t = jnp.broadcast_to(alog.reshape(n_heads * GATE_SLOTS, 1), (n_heads * GATE_SLOTS, 128))
    dtb_t = jnp.broadcast_to(dtb.reshape(n_heads * GATE_SLOTS, 1), (n_heads * GATE_SLOTS, 128))
    return w_ab, alog_c, dtb_c, alog_t, dtb_t


def kernel(x, c, ctx, c_ctx, w_ada, b_ada, norm1_w, norm2_w, w_in, conv_w, a_log, dt_bias, onorm_w,
           pool_w, pool_scale, w_out, w_up, w_down, final_norm_w):
    nb, seq_x, d = x.shape
    seq_h = ctx.shape[1]
    depth = w_ada.shape[0]
    n_heads = a_log.shape[2]
    lin_w = n_heads * HEAD_DIM
    pool_width = pool_scale.shape[1]
    pool_start = 4 * lin_w + 2 * N_DIR * n_heads
    assert nb + 1 <= MOD_ROWS and seq_x % CHUNK == 0 and seq_h % CHUNK == 0
    assert n_heads % HEADS_PER_STEP == 0 and seq_x % GRID_W == 0

    c_all = jnp.concatenate([c, c_ctx[None, :], jnp.zeros((MOD_ROWS - nb - 1, d), F32)], axis=0)
    mod_all = _ada(c_all, w_ada, b_ada).reshape(depth, MOD_ROWS, N_MOD, d)

    w_main = jnp.concatenate([w_in[:, :, :4 * lin_w], w_in[:, :, pool_start:]], axis=2).astype(BF16)
    w_o = w_out.astype(BF16)
    w_u = w_up.astype(BF16)
    w_d = w_down.astype(BF16)
    fw = final_norm_w.reshape(1, d)

    h = ctx
    for l in range(depth):
        ctx_out = l < depth - 1
        mod = mod_all[l]
        w_ab, alog_c, dtb_c, alog_t, dtb_t = _gate_layouts(w_in[l], a_log[l], dt_bias[l], lin_w, n_heads)
        n_gate_rows = n_heads * GATE_SLOTS
        nw1 = norm1_w[l].reshape(1, d)
        nw2 = norm2_w[l].reshape(1, d)
        pw = pool_w[l].astype(BF16)
        ps = pool_scale[l].reshape(1, pool_width)

        pxm, abx, abtx = _inproj(x, mod, 0, nw1, w_main, l, w_ab, n_gate_rows)
        h3 = h.reshape(1, nb * seq_h, d)
        phm, abh, abth = _inproj(h3, mod, nb, nw1, w_main, l, w_ab, n_gate_rows)
        phm = phm.reshape(nb, seq_h, -1)
        abh = abh.reshape(nb, seq_h, 128)
        abth = jnp.transpose(abth.reshape(-1, nb, seq_h), (1, 0, 2))

        gb, cum, cumt = _gates(abx, abtx, abh, abth, alog_c, dtb_c, alog_t, dtb_t)
        lin_x, lin_h = _delta(pxm, phm, gb, cum, cumt, conv_w[l], onorm_w[l].reshape(1, HEAD_DIM),
                              n_heads, ctx_out)
        part0 = (4 * lin_w) // pool_width
        pool_x = _pool(pxm, pw, ps, part0, True)
        x = _outproj(lin_x, pool_x, x, mod, 0, w_o, l)
        x = _mlp(x, mod, 0, nw2, w_u, w_d, l, fw, l == depth - 1)
        if ctx_out:
            pool_h = _pool(phm, pw, ps, part0, False)
            h3 = _outproj(lin_h.reshape(1, nb * seq_h, lin_w), pool_h.reshape(1, nb * seq_h, pool_width),
                          h3, mod, nb, w_o, l)
            h = _mlp(h3, mod, nb, nw2, w_u, w_d, l, fw, False).reshape(nb, seq_h, d)
    return x
```

```python
import functools
import math

import jax
import jax.numpy as jnp
from jax import lax
from jax.experimental import pallas as pl
from jax.experimental.pallas import tpu as pltpu

F32 = jnp.float32
BF16 = jnp.bfloat16

EPS = 1e-6
N_MOD = 6
HEAD_DIM = 128
N_DIR = 2
CONV_K = 5
GRID_W = 64
POOL_SIZES = (2, 4, 8, 16)
CHUNK = 128
INV_BLOCK = 16
GATE_SLOTS = 8
HEADS_PER_STEP = 2
CONV_PHASE = 4
POOL_BLOCKS = 4
PREP_CHUNKS = 3
ROW_SLAB = 256
MOD_ROWS = 24
VMEM_LIMIT = 56 * 1024 * 1024


def _silu(x):
    return x * (1.0 / (1.0 + jnp.exp(-x)))


def _softplus(x):
    return jnp.maximum(x, 0.0) + jnp.log1p(jnp.exp(-jnp.abs(x)))


def _cparams(n_axes):
    return pltpu.CompilerParams(dimension_semantics=("arbitrary",) * n_axes,
                                vmem_limit_bytes=VMEM_LIMIT)


def _ada_kernel(c_ref, w_ref, b_ref, o_ref):
    sc = _silu(c_ref[...]).astype(BF16)
    o_ref[...] = jnp.dot(sc, w_ref[...].astype(BF16), preferred_element_type=F32) + b_ref[...]


def _ada(c_all, w_ada, b_ada):
    depth, d, n = w_ada.shape
    tn = 1024
    return pl.pallas_call(
        _ada_kernel,
        out_shape=jax.ShapeDtypeStruct((depth, MOD_ROWS, n), F32),
        grid=(depth, n // tn),
        in_specs=[pl.BlockSpec((MOD_ROWS, d), lambda l, j: (0, 0)),
                  pl.BlockSpec((None, d, tn), lambda l, j: (l, 0, j)),
                  pl.BlockSpec((None, 1, tn), lambda l, j: (l, 0, j))],
        out_specs=pl.BlockSpec((None, MOD_ROWS, tn), lambda l, j: (l, 0, j)),
        compiler_params=_cparams(2),
        name="ada_mod",
    )(c_all, w_ada, b_ada.reshape(depth, 1, n))


def _rms(x):
    return x * lax.rsqrt(jnp.mean(x * x, axis=-1, keepdims=True) + EPS)


def _norm_mod_to(x_ref, nw_ref, mod_ref, i_shift, i_scale, out_ref, rows=128):
    nw = nw_ref[...]
    shift = mod_ref[i_shift:i_shift + 1, :]
    scale = mod_ref[i_scale:i_scale + 1, :]

    def body(r, carry):
        sl = pl.ds(pl.multiple_of(r * rows, rows), rows)
        y = _rms(x_ref[sl, :]) * nw
        out_ref[sl, :] = (y * (1.0 + scale) + shift).astype(out_ref.dtype)
        return carry

    lax.fori_loop(0, x_ref.shape[0] // rows, body, 0)


def _inproj_kernel(x_ref, mod_ref, nw_ref, w_ref, wab_ref, o_ref, ab_ref, abt_ref, xn_ref):
    j = pl.program_id(2)
    tm = x_ref.shape[0]
    n_gate_rows = abt_ref.shape[0]
    slabs = range(0, tm, ROW_SLAB)

    def step(first):
        w = w_ref[...]
        if first:
            nw = nw_ref[...]
            shift = mod_ref[0:1, :]
            scale = mod_ref[1:2, :]
            for r in slabs:
                for q in range(r, r + ROW_SLAB, 128):
                    y = _rms(x_ref[q:q + 128, :]) * nw
                    xn_ref[q:q + 128, :] = (y * (1.0 + scale) + shift).astype(BF16)
            wab = wab_ref[...]
            gates = [jnp.dot(xn_ref[r:r + ROW_SLAB, :], wab, preferred_element_type=F32) for r in slabs]
        outs = [jnp.dot(xn_ref[r:r + ROW_SLAB, :], w, preferred_element_type=F32) for r in slabs]
        for r, out in zip(slabs, outs):
            o_ref[r:r + ROW_SLAB, :] = out.astype(o_ref.dtype)
        if first:
            for r, ab in zip(slabs, gates):
                ab_ref[r:r + ROW_SLAB, :] = ab
                abt_ref[:, r:r + ROW_SLAB] = ab.T[0:n_gate_rows, :]

    @pl.when(j == 0)
    def _():
        step(True)

    @pl.when(j > 0)
    def _():
        step(False)


def _inproj(x3, mod, mod_base, nw, w_main, layer, w_ab, n_gate_rows):
    nb, rows, d = x3.shape
    n = w_main.shape[2]
    tm = min(1024, rows)
    tn = 1024
    return pl.pallas_call(
        _inproj_kernel,
        out_shape=(jax.ShapeDtypeStruct((nb, rows, n), BF16),
                   jax.ShapeDtypeStruct((nb, rows, 128), F32),
                   jax.ShapeDtypeStruct((nb, n_gate_rows, rows), F32)),
        grid=(nb, rows // tm, n // tn),
        in_specs=[pl.BlockSpec((None, tm, d), lambda b, i, j: (b, i, 0)),
                  pl.BlockSpec((None, N_MOD, d), lambda b, i, j: (mod_base + b, 0, 0)),
                  pl.BlockSpec((1, d), lambda b, i, j: (0, 0)),
                  pl.BlockSpec((None, d, tn), lambda b, i, j: (layer, 0, j)),
                  pl.BlockSpec((d, 128), lambda b, i, j: (0, 0))],
        out_specs=(pl.BlockSpec((None, tm, tn), lambda b, i, j: (b, i, j)),
                   pl.BlockSpec((None, tm, 128), lambda b, i, j: (b, i, 0)),
                   pl.BlockSpec((None, n_gate_rows, tm), lambda b, i, j: (b, 0, i))),
        scratch_shapes=[pltpu.VMEM((tm, d), BF16)],
        compiler_params=_cparams(3),
        name="in_proj",
    )(x3, mod, nw, w_main, w_ab)


def _split3(x):
    hi = x.astype(BF16)
    r1 = x - hi.astype(F32)
    mid = r1.astype(BF16)
    lo = (r1 - mid.astype(F32)).astype(BF16)
    return hi, mid, lo


def _gate_kernel(abx_ref, abtx_ref, abh_ref, abth_ref, alog_ref, dtb_ref, alogt_ref, dtbt_ref,
                 gb_ref, cum_ref, cumt_ref, *, seq_x, seq_h):
    c = CHUNK
    n_gt = abtx_ref.shape[0]
    slot_l = lax.broadcasted_iota(jnp.int32, (c, 128), 1) % GATE_SLOTS
    row_slot = lax.broadcasted_iota(jnp.int32, (n_gt, c), 0) % GATE_SLOTS
    ri = lax.broadcasted_iota(jnp.int32, (c, c), 0)
    ci = lax.broadcasted_iota(jnp.int32, (c, c), 1)
    m_le = jnp.where(ci <= ri, 1.0, 0.0).astype(BF16)
    m_ge = jnp.where(ci >= ri, 1.0, 0.0).astype(BF16)
    alog_r = alog_ref[...]
    dtb_r = dtb_ref[...]
    alogt = alogt_ref[...]
    dtbt = dtbt_ref[...]

    def part(ab_ref, abt_ref, row0, n_rows):
        def body(r, carry):
            src = pl.ds(pl.multiple_of(r * c, c), c)
            dst = pl.ds(pl.multiple_of(row0 + r * c, c), c)
            a = ab_ref[src, :]
            g = -jnp.exp(alog_r) * _softplus(a + dtb_r)
            gb = jnp.where(slot_l < N_DIR, g, 1.0 / (1.0 + jnp.exp(-a)))
            gb_ref[dst, :] = gb
            pieces = jnp.concatenate(_split3(gb), axis=1)
            pre = jnp.dot(m_le, pieces, preferred_element_type=F32)
            suf = jnp.dot(m_ge, pieces, preferred_element_type=F32)
            pre = pre[:, 0:128] + pre[:, 128:256] + pre[:, 256:384]
            suf = suf[:, 0:128] + suf[:, 128:256] + suf[:, 256:384]
            cum_ref[dst, :] = jnp.where(slot_l == 0, pre, suf)
            gt = -jnp.exp(alogt) * _softplus(abt_ref[:, src] + dtbt)
            pieces_t = jnp.concatenate(_split3(gt), axis=0)
            pre_t = jnp.dot(pieces_t, m_ge, preferred_element_type=F32)
            suf_t = jnp.dot(pieces_t, m_le, preferred_element_type=F32)
            pre_t = pre_t[0:n_gt] + pre_t[n_gt:2 * n_gt] + pre_t[2 * n_gt:3 * n_gt]
            suf_t = suf_t[0:n_gt] + suf_t[n_gt:2 * n_gt] + suf_t[2 * n_gt:3 * n_gt]
            cumt_ref[:, dst] = jnp.where(row_slot == 0, pre_t, suf_t)
            return carry

        lax.fori_loop(0, n_rows // c, body, 0, unroll=2)

    part(abh_ref, abth_ref, 0, seq_h)
    part(abx_ref, abtx_ref, seq_h, seq_x)


def _gates(abx, abtx, abh, abth, alog_c, dtb_c, alog_t, dtb_t):
    nb, seq_x, _ = abx.shape
    seq_h = abh.shape[1]
    n_gt = abtx.shape[1]
    seq_t = seq_x + seq_h
    return pl.pallas_call(
        functools.partial(_gate_kernel, seq_x=seq_x, seq_h=seq_h),
        out_shape=(jax.ShapeDtypeStruct((nb, seq_t, 128), F32),
                   jax.ShapeDtypeStruct((nb, seq_t, 128), F32),
                   jax.ShapeDtypeStruct((nb, n_gt, seq_t), F32)),
        grid=(nb,),
        in_specs=[pl.BlockSpec((None, seq_x, 128), lambda b: (b, 0, 0)),
                  pl.BlockSpec((None, n_gt, seq_x), lambda b: (b, 0, 0)),
                  pl.BlockSpec((None, seq_h, 128), lambda b: (b, 0, 0)),
                  pl.BlockSpec((None, n_gt, seq_h), lambda b: (b, 0, 0)),
                  pl.BlockSpec((1, 128), lambda b: (0, 0)),
                  pl.BlockSpec((1, 128), lambda b: (0, 0)),
                  pl.BlockSpec((n_gt, 128), lambda b: (0, 0)),
                  pl.BlockSpec((n_gt, 128), lambda b: (0, 0))],
        out_specs=(pl.BlockSpec((None, seq_t, 128), lambda b: (b, 0, 0)),
                   pl.BlockSpec((None, seq_t, 128), lambda b: (b, 0, 0)),
                   pl.BlockSpec((None, n_gt, seq_t), lambda b: (b, 0, 0))),
        compiler_params=_cparams(1),
        name="gates",
    )(abx, abtx, abh, abth, alog_c, dtb_c, alog_t, dtb_t)


def _delta_kernel(*refs, seq_x, seq_h, hg, ctx_out):
    n_in = 14
    (qx_ref, kx_ref, vx_ref, zx_ref, qh_ref, kh_ref, vh_ref, zh_ref,
     cq_ref, ck_ref, cv_ref, gb_ref, cum_ref, cumt_ref) = refs[:n_in]
    onorm_ref = refs[n_in]
    if ctx_out:
        ox_ref, oh_ref = refs[n_in + 1:n_in + 3]
        scratch = refs[n_in + 3:]
    else:
        ox_ref = refs[n_in + 1]
        oh_ref = None
        scratch = refs[n_in + 2:]
    (padq_s, padk_s, padv_s, q_s, k_s, v_s, o_s, u_s, wq_s, ktt_s, at_s, egl_s, st_s) = scratch

    c = CHUNK
    hd = HEAD_DIM
    width = hg * hd
    seq_t = seq_h + seq_x
    nc_h = seq_h // c
    nc_x = seq_x // c
    nc = nc_h + nc_x
    n_chain = hg * N_DIR

    zero8 = jnp.zeros((8, hd), F32)
    for pad, h_ref, x_ref in ((padq_s, qh_ref, qx_ref), (padk_s, kh_ref, kx_ref), (padv_s, vh_ref, vx_ref)):
        for hl in range(hg):
            cols = slice(hl * hd, (hl + 1) * hd)
            pad[hl, 0:8, :] = zero8
            pad[hl, 8:8 + seq_h, :] = h_ref[:, cols].astype(F32)
            pad[hl, 8 + seq_h:16 + seq_h, :] = zero8
            pad[hl, 16 + seq_h:16 + seq_t, :] = x_ref[:, cols].astype(F32)
            pad[hl, 16 + seq_t:24 + seq_t, :] = zero8
    conv_taps = [[[w_ref[i:i + 1, hl * hd:(hl + 1) * hd] for i in range(CONV_K)] for hl in range(hg)]
                 for w_ref in (cq_ref, ck_ref, cv_ref)]

    def conv_chunk(ch):
        base = ch * c + jnp.where(ch >= nc_h, 16, 8)
        n_ph = c // CONV_PHASE
        for pad, taps, dst, mode in ((padq_s, conv_taps[0], q_s, "q"), (padk_s, conv_taps[1], k_s, "k"),
                                     (padv_s, conv_taps[2], v_s, "v")):
            for hl in range(hg):
                shifted = {off: pad[hl, pl.ds(base + off, n_ph, stride=CONV_PHASE), :]
                           for off in range(-(CONV_K // 2), CONV_PHASE + CONV_K // 2)}
                for ph in range(CONV_PHASE):
                    acc = None
                    for i in range(CONV_K):
                        term = shifted[ph + i - CONV_K // 2] * taps[hl][i]
                        acc = term if acc is None else acc + term
                    y = _silu(acc)
                    if mode != "v":
                        y = y * lax.rsqrt(jnp.sum(y * y, axis=-1, keepdims=True) + EPS)
                        if mode == "q":
                            y = y * (hd ** -0.5)
                    dst[hl, pl.ds(ch * c + ph, n_ph, stride=CONV_PHASE), :] = y

    shift = (128 - pl.program_id(1) * (hg * GATE_SLOTS)) % 128
    ri = lax.broadcasted_iota(jnp.int32, (c, c), 0)
    ci = lax.broadcasted_iota(jnp.int32, (c, c), 1)
    lower_incl = (ci <= ri)
    upper_incl = (ci >= ri)

    eye = jnp.where(ri == ci, 1.0, 0.0).astype(F32)
    masks = ((lower_incl, ci < ri), (upper_incl, ci > ri))

    def bdot(a, b):
        return jnp.dot(a.astype(BF16), b.astype(BF16), preferred_element_type=F32)

    same_blk = (ri // INV_BLOCK) == (ci // INV_BLOCK)

    def tri_inverse(nms, lowers):
        nds = [jnp.where(same_blk, nm, 0.0) for nm in nms]
        ms = [eye + nd for nd in nds]
        ps = [bdot(nd, nd) for nd in nds]
        for _ in range(int(math.log2(INV_BLOCK)) - 2):
            rs = [bdot(p, jnp.concatenate([m, p], axis=1)) for m, p in zip(ms, ps)]
            ms = [m + r_[:, 0:c] for m, r_ in zip(ms, rs)]
            ps = [r_[:, c:2 * c] for r_ in rs]
        rs = [bdot(p, m) for m, p in zip(ms, ps)]
        ms = [m + r_ for m, r_ in zip(ms, rs)]
        s = INV_BLOCK
        while s < c:
            starts = list(range(0, c, s))
            pair = ((ri // (2 * s)) == (ci // (2 * s))) & ((ri // s) != (ci // s))
            zero = jnp.zeros((s, c), BF16)
            hits = [[(a // s) % 2 == (1 if lower else 0) for a in starts] for lower in lowers]

            def take(xm, hit):
                return jnp.concatenate([xm[a:a + s] for a, t in zip(starts, hit) if t], axis=0)

            m_sels = [take(m, hit) for m, hit in zip(ms, hits)]
            xs = [bdot(take(jnp.where(pair, nm, 0.0), hit), m).astype(BF16)
                  for nm, m, hit in zip(nms, ms, hits)]
            ys = []
            for x, m_sel, hit in zip(xs, m_sels, hits):
                slabs, k_ = [], 0
                for t in hit:
                    slabs.append(x[k_ * s:(k_ + 1) * s] if t else zero)
                    k_ += 1 if t else 0
                ys.append(jnp.dot(m_sel.astype(BF16), jnp.concatenate(slabs, axis=0),
                                  preferred_element_type=F32))
            new_ms = []
            for m, m_sel, y, hit in zip(ms, m_sels, ys, hits):
                new_sel = m_sel + y
                slabs, k_ = [], 0
                for a, t in zip(starts, hit):
                    slabs.append(new_sel[k_ * s:(k_ + 1) * s] if t else m[a:a + s])
                    k_ += 1 if t else 0
                new_ms.append(jnp.concatenate(slabs, axis=0))
            ms = new_ms
            s *= 2
        return ms

    n_prep = PREP_CHUNKS if nc % PREP_CHUNKS == 0 else 1

    def prep(grp, carry):
        chs = [grp * n_prep + j for j in range(n_prep)]
        rows = [pl.ds(pl.multiple_of(ch * c, c), c) for ch in chs]
        gbc = [pltpu.roll(gb_ref[r, :], shift, 1) for r in rows]
        cumc = [pltpu.roll(cum_ref[r, :], shift, 1) for r in rows]
        cumtc = [cumt_ref[:, r] for r in rows]
        qs, ks, vs, aas = {}, {}, {}, {}
        for j in range(n_prep):
            for hl in range(hg):
                cols = slice(hl * hd, (hl + 1) * hd)
                qs[j, hl] = q_s[hl, rows[j], :]
                ks[j, hl] = k_s[hl, rows[j], :]
                vs[j, hl] = v_s[hl, rows[j], :]
                qk = jnp.concatenate([qs[j, hl], ks[j, hl]], axis=0).astype(BF16)
                aas[j, hl] = lax.dot_general(qk, ks[j, hl].astype(BF16), (((1,), (1,)), ((), ())),
                                             preferred_element_type=F32)
        probs = [(j, hl, d) for j in range(n_prep) for hl in range(hg) for d in range(N_DIR)]
        nms, betas, cums, decays = [], [], [], []
        for j, hl, d in probs:
            incl, strict = masks[d]
            base = hl * GATE_SLOTS
            beta_b = jnp.broadcast_to(gbc[j][:, base + N_DIR + d:base + N_DIR + d + 1], (c, hd))
            cum_b = jnp.broadcast_to(cumc[j][:, base + d:base + d + 1], (c, hd))
            cum_r = cumtc[j][base + d:base + d + 1, :]
            decay = jnp.where(incl, jnp.exp(jnp.where(incl, cum_b - cum_r, 0.0)), 0.0)
            nms.append(jnp.where(strict, -(beta_b * aas[j, hl][c:2 * c] * decay), 0.0))
            betas.append(beta_b)
            cums.append(cum_b)
            decays.append(decay)
        ms = tri_inverse(nms, [d == 0 for _, _, d in probs])
        e_cs = [jnp.exp(cum_b) for cum_b in cums]
        uws = [bdot(m, jnp.concatenate([vs[j, hl] * beta_b, ks[j, hl] * (beta_b * e_c)], axis=1))
               for (j, hl, d), m, beta_b, e_c in zip(probs, ms, betas, e_cs)]
        for i, (j, hl, d) in enumerate(probs):
            cum_b, e_c, uw = cums[i], e_cs[i], uws[i]
            gl_b = cum_b[c - 1:c, :] if d == 0 else cum_b[0:1, :]
            kt = ks[j, hl] * jnp.exp(gl_b - cum_b)
            slot = (hl * N_DIR + d) * nc + chs[j]
            u_s[pl.ds(pl.multiple_of(slot * c, c), c), :] = uw[:, 0:hd]
            wq_s[pl.ds(pl.multiple_of(slot * 2 * c, c), c), :] = uw[:, hd:2 * hd].astype(BF16)
            wq_s[pl.ds(pl.multiple_of(slot * 2 * c + c, c), c), :] = (qs[j, hl] * e_c).astype(BF16)
            ktt_s[pl.ds(pl.multiple_of(slot * hd, hd), hd), :] = kt.T.astype(BF16)
            at_s[pl.ds(pl.multiple_of(slot * c, c), c), :] = (aas[j, hl][0:c] * decays[i]).astype(BF16)
            egl_s[pl.ds(pl.multiple_of(slot * 8, 8), 8), :] = jnp.broadcast_to(jnp.exp(gl_b), (8, hd))
        nxt = jnp.minimum(grp + 1, n_groups_prep - 1)
        for j in range(n_prep):
            conv_chunk(nxt * n_prep + j)
        return carry

    n_groups_prep = nc // n_prep
    for j in range(n_prep):
        conv_chunk(j)
    lax.fori_loop(0, n_groups_prep, prep, 0)

    st_s[...] = jnp.zeros(st_s.shape, F32)
    o_s[...] = jnp.zeros(o_s.shape, F32)

    def scan_step(ch_f, ch_b, with_q):
        chains = [(hl, d) for hl in range(hg) for d in range(N_DIR)]
        chs = [ch_f if d == 0 else ch_b for _, d in chains]
        slots = [i * nc + ch for i, ch in enumerate(chs)]
        s_olds = [st_s[i * hd:(i + 1) * hd, :] for i in range(len(chains))]
        n_lhs = 2 * c if with_q else c
        rs = [jnp.dot(wq_s[pl.ds(pl.multiple_of(slot * 2 * c, 2 * c), n_lhs), :], s_old.astype(BF16),
                      preferred_element_type=F32) for slot, s_old in zip(slots, s_olds)]
        v_bs = [(u_s[pl.ds(pl.multiple_of(slot * c, c), c), :] - r_[0:c]).astype(BF16)
                for slot, r_ in zip(slots, rs)]
        kvs = [jnp.dot(ktt_s[pl.ds(pl.multiple_of(slot * hd, hd), hd), :], v_b, preferred_element_type=F32)
               for slot, v_b in zip(slots, v_bs)]
        if with_q:
            avs = [jnp.dot(at_s[pl.ds(pl.multiple_of(slot * c, c), c), :], v_b, preferred_element_type=F32)
                   for slot, v_b in zip(slots, v_bs)]
        for i, (hl, d) in enumerate(chains):
            egl = egl_s[pl.ds(pl.multiple_of(slots[i] * 8, 8), 1), :]
            st_s[i * hd:(i + 1) * hd, :] = s_olds[i] * egl + kvs[i]
            if with_q:
                orow = pl.ds(pl.multiple_of(chs[i] * c, c), c)
                ocol = slice(hl * hd, (hl + 1) * hd)
                o_s[orow, ocol] = o_s[orow, ocol] + (rs[i][c:2 * c] + avs[i])

    onw = onorm_ref[...]

    def finish_chunk(z_ref, out_ref, ch, ch0):
        src = pl.ds(pl.multiple_of(ch * c, c), c)
        dst = pl.ds(pl.multiple_of((ch - ch0) * c, c), c)
        outs = []
        for hl in range(hg):
            cols = slice(hl * hd, (hl + 1) * hd)
            outs.append(_rms(o_s[src, cols]) * onw * _silu(z_ref[dst, cols].astype(F32)))
        y = outs[0] if hg == 1 else jnp.concatenate(outs, axis=1)
        out_ref[dst, :] = y.astype(out_ref.dtype)

    def scan_h(it, carry):
        scan_step(it, nc_h - 1 - it, ctx_out)
        return carry

    def scan_x_first(it, carry):
        scan_step(nc_h + it, nc - 1 - it, True)
        return carry

    def finish_pair(it):
        finish_chunk(zx_ref, ox_ref, nc_h + it, nc_h)
        finish_chunk(zx_ref, ox_ref, nc - 1 - it, nc_h)

    def scan_x_second(it, carry):
        finish_pair(it - 1)
        scan_step(nc_h + it, nc - 1 - it, True)
        return carry

    lax.fori_loop(0, nc_h, scan_h, 0)
    lax.fori_loop(0, nc_x // 2 + 1, scan_x_first, 0)
    lax.fori_loop(nc_x // 2 + 1, nc_x, scan_x_second, 0)
    finish_pair(nc_x - 1)
    if ctx_out:
        for ch in range(nc_h):
            finish_chunk(zh_ref, oh_ref, ch, 0)


def _delta(pxm, phm, gb, cum, cumt, conv_w, onorm, n_heads, ctx_out):
    nb, seq_x, _ = pxm.shape
    seq_h = phm.shape[1]
    hg = HEADS_PER_STEP
    hd = HEAD_DIM
    width = hg * hd
    n_groups = n_heads // hg
    lin_w = n_heads * hd
    seq_t = seq_x + seq_h
    nc = seq_t // CHUNK
    n_chain = hg * N_DIR
    n_gt = hg * GATE_SLOTS

    def main_spec(rows, part):
        return pl.BlockSpec((None, rows, width), lambda b, g: (b, 0, part * n_groups + g))

    in_specs = (
        [main_spec(seq_x, p) for p in range(4)]
        + [main_spec(seq_h, p) for p in range(4)]
        + [pl.BlockSpec((CONV_K, width), lambda b, g, p=p: (0, p * n_groups + g)) for p in range(3)]
        + [pl.BlockSpec((None, seq_t, 128), lambda b, g: (b, 0, 0)),
           pl.BlockSpec((None, seq_t, 128), lambda b, g: (b, 0, 0)),
           pl.BlockSpec((None, n_gt, seq_t), lambda b, g: (b, g, 0)),
           pl.BlockSpec((1, hd), lambda b, g: (0, 0))]
    )
    out_shape = [jax.ShapeDtypeStruct((nb, seq_x, lin_w), BF16)]
    out_specs = [pl.BlockSpec((None, seq_x, width), lambda b, g: (b, 0, g))]
    if ctx_out:
        out_shape.append(jax.ShapeDtypeStruct((nb, seq_h, lin_w), BF16))
        out_specs.append(pl.BlockSpec((None, seq_h, width), lambda b, g: (b, 0, g)))
    scratch = [
        pltpu.VMEM((hg, seq_t + 24, hd), F32),
        pltpu.VMEM((hg, seq_t + 24, hd), F32),
        pltpu.VMEM((hg, seq_t + 24, hd), F32),
        pltpu.VMEM((hg, seq_t, hd), F32),
        pltpu.VMEM((hg, seq_t, hd), F32),
        pltpu.VMEM((hg, seq_t, hd), F32),
        pltpu.VMEM((seq_t, width), F32),
        pltpu.VMEM((n_chain * nc * CHUNK, hd), F32),
        pltpu.VMEM((n_chain * nc * 2 * CHUNK, hd), BF16),
        pltpu.VMEM((n_chain * nc * hd, CHUNK), BF16),
        pltpu.VMEM((n_chain * nc * CHUNK, CHUNK), BF16),
        pltpu.VMEM((n_chain * nc * 8, hd), F32),
        pltpu.VMEM((n_chain * hd, hd), F32),
    ]
    args = [pxm] * 4 + [phm] * 4 + [conv_w] * 3 + [gb, cum, cumt, onorm]
    res = pl.pallas_call(
        functools.partial(_delta_kernel, seq_x=seq_x, seq_h=seq_h, hg=hg, ctx_out=ctx_out),
        out_shape=tuple(out_shape),
        grid=(nb, n_groups),
        in_specs=in_specs,
        out_specs=tuple(out_specs),
        scratch_shapes=scratch,
        compiler_params=_cparams(2),
        name="gated_delta",
    )(*args)
    return res if ctx_out else (res[0], None)


def _pool_kernel(p_ref, w_ref, sc_ref, o_ref, y_s, *, on_grid, n_groups, gw):
    n = p_ref.shape[0]
    blk = 2 * GRID_W if on_grid else n
    period = GRID_W if on_grid else n
    ri = lax.broadcasted_iota(jnp.int32, (blk, blk), 0)
    ci = lax.broadcasted_iota(jnp.int32, (blk, blk), 1)
    pos = lax.broadcasted_iota(jnp.int32, (blk, gw), 0) % period

    for g in range(n_groups):
        size = POOL_SIZES[g]
        half = size // 2
        cols = slice(g * gw, (g + 1) * gw)
        if on_grid:
            n_rows = n // GRID_W
            def grid_row(rr):
                return p_ref[rr * GRID_W:(rr + 1) * GRID_W, cols].astype(F32)

            acc, lo_p, hi_p = None, 0, -1
            for r in range(n_rows):
                lo = max(r - half, 0)
                hi = min(r + size - 1 - half, n_rows - 1)
                for rr in range(hi_p + 1, hi + 1):
                    acc = grid_row(rr) if acc is None else acc + grid_row(rr)
                for rr in range(lo_p, lo):
                    acc = acc - grid_row(rr)
                lo_p, hi_p = lo, hi
                y_s[r * GRID_W:(r + 1) * GRID_W, :] = acc / float(hi - lo + 1)
            src = y_s
            src_cols = slice(0, gw)
        else:
            src = p_ref
            src_cols = cols
        band = ((ci >= ri - half) & (ci <= ri + size - 1 - half) & (ci // period == ri // period))
        band = jnp.where(band, 1.0, 0.0).astype(BF16)
        lo_c = jnp.maximum(pos - half, 0)
        hi_c = jnp.minimum(pos + size - 1 - half, period - 1)
        cnt = (hi_c - lo_c + 1).astype(F32)
        w_g = w_ref[g]
        scale = sc_ref[:, cols]

        n_blk = n // blk
        group = POOL_BLOCKS if n_blk % POOL_BLOCKS == 0 else 1

        def body(t, carry):
            rows = [pl.ds(pl.multiple_of((t * group + j) * blk, blk), blk) for j in range(group)]
            pieces = []
            for r in rows:
                y = src[r, src_cols].astype(F32)
                hi_p = y.astype(BF16)
                r1 = y - hi_p.astype(F32)
                mid_p = r1.astype(BF16)
                lo_p = (r1 - mid_p.astype(F32)).astype(BF16)
                pieces.append(jnp.concatenate([hi_p, mid_p, lo_p], axis=1))
            tots = [jnp.dot(band, pc, preferred_element_type=F32) for pc in pieces]
            dlts = []
            for r, tot in zip(rows, tots):
                mean = (tot[:, 0:gw] + tot[:, gw:2 * gw] + tot[:, 2 * gw:3 * gw]) / cnt
                dlts.append((mean - p_ref[r, cols].astype(F32)).astype(BF16))
            outs = [jnp.dot(dlt, w_g, preferred_element_type=F32) for dlt in dlts]
            for r, out in zip(rows, outs):
                o_ref[r, cols] = (out * scale).astype(o_ref.dtype)
            return carry

        lax.fori_loop(0, n_blk // group, body, 0)


def _pool(pm, pool_w, pool_scale, part0, on_grid):
    nb, rows, _ = pm.shape
    n_groups, gw, _ = pool_w.shape
    pw = n_groups * gw
    return pl.pallas_call(
        functools.partial(_pool_kernel, on_grid=on_grid, n_groups=n_groups, gw=gw),
        out_shape=jax.ShapeDtypeStruct((nb, rows, pw), BF16),
        grid=(nb,),
        in_specs=[pl.BlockSpec((None, rows, pw), lambda b: (b, 0, part0)),
                  pl.BlockSpec((n_groups, gw, gw), lambda b: (0, 0, 0)),
                  pl.BlockSpec((1, pw), lambda b: (0, 0))],
        out_specs=pl.BlockSpec((None, rows, pw), lambda b: (b, 0, 0)),
        scratch_shapes=[pltpu.VMEM((rows, gw), F32)],
        compiler_params=_cparams(1),
        name="pool_mixer",
    )(pm, pool_w, pool_scale)


def _outproj_kernel(lin_ref, pool_ref, x_ref, mod_ref, wa_ref, wb_ref, o_ref):
    y = (jnp.dot(lin_ref[...], wa_ref[...], preferred_element_type=F32)
         + jnp.dot(pool_ref[...], wb_ref[...], preferred_element_type=F32))
    o_ref[...] = x_ref[...] + mod_ref[2:3, :] * y


def _outproj(lin, pool, x3, mod, mod_base, w_out_all, layer):
    nb, rows, d = x3.shape
    tm = min(512, rows)
    ka = lin.shape[2]
    kb = pool.shape[2]
    assert ka == kb and w_out_all.shape[1] == ka + kb
    return pl.pallas_call(
        _outproj_kernel,
        out_shape=jax.ShapeDtypeStruct((nb, rows, d), F32),
        grid=(nb, rows // tm),
        in_specs=[pl.BlockSpec((None, tm, ka), lambda b, i: (b, i, 0)),
                  pl.BlockSpec((None, tm, kb), lambda b, i: (b, i, 0)),
                  pl.BlockSpec((None, tm, d), lambda b, i: (b, i, 0)),
                  pl.BlockSpec((None, N_MOD, d), lambda b, i: (mod_base + b, 0, 0)),
                  pl.BlockSpec((None, ka, d), lambda b, i: (layer, 0, 0)),
                  pl.BlockSpec((None, kb, d), lambda b, i: (layer, 1, 0))],
        out_specs=pl.BlockSpec((None, tm, d), lambda b, i: (b, i, 0)),
        compiler_params=_cparams(2),
        name="out_proj",
    )(lin, pool, x3, mod, w_out_all, w_out_all)


def _mlp_kernel(x_ref, mod_ref, nw_ref, wu_ref, wd_ref, fw_ref, o_ref, xn_ref, *, final_norm):
    f = pl.program_id(2)
    n_f = pl.num_programs(2)
    slabs = range(0, xn_ref.shape[0], ROW_SLAB)

    def step(first, last):
        wu = wu_ref[...]
        wd = wd_ref[...]
        if first:
            nw = nw_ref[...]
            shift = mod_ref[3:4, :]
            scale = mod_ref[4:5, :]
            for r in slabs:
                for q in range(r, r + ROW_SLAB, 128):
                    y = _rms(x_ref[q:q + 128, :]) * nw
                    xn_ref[q:q + 128, :] = (y * (1.0 + scale) + shift).astype(BF16)
        hs = [jnp.dot(xn_ref[r:r + ROW_SLAB, :], wu, preferred_element_type=F32) for r in slabs]
        hs = [jnp.square(jnp.maximum(h, 0.0)).astype(BF16) for h in hs]
        parts = [jnp.dot(h, wd, preferred_element_type=F32) for h in hs]
        for r, part in zip(slabs, parts):
            acc = part if first else o_ref[r:r + ROW_SLAB, :] + part
            if last:
                acc = x_ref[r:r + ROW_SLAB, :] + mod_ref[5:6, :] * acc
                if final_norm:
                    acc = _rms(acc) * fw_ref[...]
            o_ref[r:r + ROW_SLAB, :] = acc

    @pl.when(f == 0)
    def _():
        step(True, False)

    @pl.when((f > 0) & (f < n_f - 1))
    def _():
        step(False, False)

    @pl.when(f == n_f - 1)
    def _():
        step(False, True)


def _mlp(x3, mod, mod_base, nw, w_up, w_down, layer, final_w, final_norm):
    nb, rows, d = x3.shape
    dff = w_up.shape[2]
    tm = min(1024, rows)
    tf = 512
    return pl.pallas_call(
        functools.partial(_mlp_kernel, final_norm=final_norm),
        out_shape=jax.ShapeDtypeStruct((nb, rows, d), F32),
        grid=(nb, rows // tm, dff // tf),
        in_specs=[pl.BlockSpec((None, tm, d), lambda b, i, f: (b, i, 0)),
                  pl.BlockSpec((None, N_MOD, d), lambda b, i, f: (mod_base + b, 0, 0)),
                  pl.BlockSpec((1, d), lambda b, i, f: (0, 0)),
                  pl.BlockSpec((None, d, tf), lambda b, i, f: (layer, 0, f)),
                  pl.BlockSpec((None, tf, d), lambda b, i, f: (layer, f, 0)),
                  pl.BlockSpec((1, d), lambda b, i, f: (0, 0))],
        out_specs=pl.BlockSpec((None, tm, d), lambda b, i, f: (b, i, 0)),
        scratch_shapes=[pltpu.VMEM((tm, d), BF16)],
        compiler_params=_cparams(3),
        name="mlp",
    )(x3, mod, nw, w_up, w_down, final_w)


def _gate_layouts(w_in_l, a_log_l, dt_bias_l, lin_w, n_heads):
    d = w_in_l.shape[0]
    decay_start = 4 * lin_w
    ab = w_in_l[:, decay_start:decay_start + 2 * N_DIR * n_heads]
    ab = ab.reshape(d, 2 * N_DIR, n_heads)
    ab = jnp.transpose(ab, (0, 2, 1))
    ab = jnp.pad(ab, ((0, 0), (0, 0), (0, GATE_SLOTS - 2 * N_DIR)))
    w_ab = ab.reshape(d, n_heads * GATE_SLOTS)
    w_ab = jnp.pad(w_ab, ((0, 0), (0, 128 - n_heads * GATE_SLOTS))).astype(BF16)

    def per_head(p):
        return jnp.pad(p.T.astype(F32), ((0, 0), (0, GATE_SLOTS - N_DIR)))

    alog = per_head(a_log_l)
    dtb = per_head(dt_bias_l)
    alog_c = jnp.pad(alog.reshape(1, n_heads * GATE_SLOTS), ((0, 0), (0, 128 - n_heads * GATE_SLOTS)))
    dtb_c = jnp.pad(dtb.reshape(1, n_heads * GATE_SLOTS), ((0, 0), (0, 128 - n_heads * GATE_SLOTS)))
    alog_t = jnp.broadcast_to(alog.reshape(n_heads * GATE_SLOTS, 1), (n_heads * GATE_SLOTS, 128))
    dtb_t = jnp.broadcast_to(dtb.reshape(n_heads * GATE_SLOTS, 1), (n_heads * GATE_SLOTS, 128))
    return w_ab, alog_c, dtb_c, alog_t, dtb_t


def kernel(x, c, ctx, c_ctx, w_ada, b_ada, norm1_w, norm2_w, w_in, conv_w, a_log, dt_bias, onorm_w,
           pool_w, pool_scale, w_out, w_up, w_down, final_norm_w):
    nb, seq_x, d = x.shape
    seq_h = ctx.shape[1]
    depth = w_ada.shape[0]
    n_heads = a_log.shape[2]
    lin_w = n_heads * HEAD_DIM
    pool_width = pool_scale.shape[1]
    pool_start = 4 * lin_w + 2 * N_DIR * n_heads
    assert nb + 1 <= MOD_ROWS and seq_x % (2 * CHUNK) == 0 and seq_h % CHUNK == 0
    assert n_heads % HEADS_PER_STEP == 0 and seq_x % GRID_W == 0

    c_all = jnp.concatenate([c, c_ctx[None, :], jnp.zeros((MOD_ROWS - nb - 1, d), F32)], axis=0)
    mod_all = _ada(c_all, w_ada, b_ada).reshape(depth, MOD_ROWS, N_MOD, d)

    w_main = jnp.concatenate([w_in[:, :, :4 * lin_w], w_in[:, :, pool_start:]], axis=2).astype(BF16)
    w_o = w_out.astype(BF16)
    w_u = w_up.astype(BF16)
    w_d = w_down.astype(BF16)
    fw = final_norm_w.reshape(1, d)

    h = ctx
    for l in range(depth):
        ctx_out = l < depth - 1
        mod = mod_all[l]
        w_ab, alog_c, dtb_c, alog_t, dtb_t = _gate_layouts(w_in[l], a_log[l], dt_bias[l], lin_w, n_heads)
        n_gate_rows = n_heads * GATE_SLOTS
        nw1 = norm1_w[l].reshape(1, d)
        nw2 = norm2_w[l].reshape(1, d)
        pw = pool_w[l].astype(BF16)
        ps = pool_scale[l].reshape(1, pool_width)

        pxm, abx, abtx = _inproj(x, mod, 0, nw1, w_main, l, w_ab, n_gate_rows)
        h3 = h.reshape(1, nb * seq_h, d)
        phm, abh, abth = _inproj(h3, mod, nb, nw1, w_main, l, w_ab, n_gate_rows)
        phm = phm.reshape(nb, seq_h, -1)
        abh = abh.reshape(nb, seq_h, 128)
        abth = jnp.transpose(abth.reshape(-1, nb, seq_h), (1, 0, 2))

        gb, cum, cumt = _gates(abx, abtx, abh, abth, alog_c, dtb_c, alog_t, dtb_t)
        lin_x, lin_h = _delta(pxm, phm, gb, cum, cumt, conv_w[l], onorm_w[l].reshape(1, HEAD_DIM),
                              n_heads, ctx_out)
        part0 = (4 * lin_w) // pool_width
        pool_x = _pool(pxm, pw, ps, part0, True)
        x = _outproj(lin_x, pool_x, x, mod, 0, w_o, l)
        x = _mlp(x, mod, 0, nw2, w_u, w_d, l, fw, l == depth - 1)
        if ctx_out:
            pool_h = _pool(phm, pw, ps, part0, False)
            h3 = _outproj(lin_h.reshape(1, nb * seq_h, lin_w), pool_h.reshape(1, nb * seq_h, pool_width),
                          h3, mod, nb, w_o, l)
            h = _mlp(h3, mod, nb, nw2, w_u, w_d, l, fw, False).reshape(nb, seq_h, d)
    return x
```

```python
import functools
import math

import jax
import jax.numpy as jnp
from jax import lax
from jax.experimental import pallas as pl
from jax.experimental.pallas import tpu as pltpu

F32 = jnp.float32
BF16 = jnp.bfloat16

EPS = 1e-6
N_MOD = 6
HEAD_DIM = 128
N_DIR = 2
CONV_K = 5
GRID_W = 64
POOL_SIZES = (2, 4, 8, 16)

LANES = 128
SUBLANES = 8
VMEM_BYTES = 64 * 1024 * 1024
VMEM_LIMIT = VMEM_BYTES - 8 * 1024 * 1024

TOKEN_TILE = 1024
OUT_TOKEN_TILE = 512
COL_TILE = 1024
FF_TILE = 512
ROW_SLAB = 256
NORM_ROWS = 128
CHUNK = 128
INV_BLOCK = 16
GATE_SLOTS = 8
HEADS_PER_STEP = 2
CONV_PHASE = 4
POOL_BLOCKS = 4
PREP_CHUNKS = 3
GATE_UNROLL = 4
MOD_ROWS = 24


def _silu(x):
    return x * (1.0 / (1.0 + jnp.exp(-x)))


def _softplus(x):
    return jnp.maximum(x, 0.0) + jnp.log1p(jnp.exp(-jnp.abs(x)))


def _cparams(n_axes):
    return pltpu.CompilerParams(dimension_semantics=("arbitrary",) * n_axes,
                                vmem_limit_bytes=VMEM_LIMIT)


def _ada_kernel(c_ref, w_ref, b_ref, o_ref):
    sc = _silu(c_ref[...]).astype(BF16)
    o_ref[...] = jnp.dot(sc, w_ref[...].astype(BF16), preferred_element_type=F32) + b_ref[...]


def _ada(c_all, w_ada, b_ada):
    depth, d, n = w_ada.shape
    tn = COL_TILE
    return pl.pallas_call(
        _ada_kernel,
        out_shape=jax.ShapeDtypeStruct((depth, MOD_ROWS, n), F32),
        grid=(depth, n // tn),
        in_specs=[pl.BlockSpec((MOD_ROWS, d), lambda l, j: (0, 0)),
                  pl.BlockSpec((None, d, tn), lambda l, j: (l, 0, j)),
                  pl.BlockSpec((None, 1, tn), lambda l, j: (l, 0, j))],
        out_specs=pl.BlockSpec((None, MOD_ROWS, tn), lambda l, j: (l, 0, j)),
        compiler_params=_cparams(2),
        name="ada_mod",
    )(c_all, w_ada, b_ada.reshape(depth, 1, n))


def _rms(x):
    return x * lax.rsqrt(jnp.mean(x * x, axis=-1, keepdims=True) + EPS)


def _norm_mod_slab(x_ref, nw, shift, scale, out_ref, r0, n_rows):
    for q in range(r0, r0 + n_rows, NORM_ROWS):
        y = _rms(x_ref[q:q + NORM_ROWS, :]) * nw
        out_ref[q:q + NORM_ROWS, :] = (y * (1.0 + scale) + shift).astype(out_ref.dtype)


def _inproj_kernel(x_ref, mod_ref, nw_ref, w_ref, wab_ref, o_ref, ab_ref, abt_ref, xn_ref):
    j = pl.program_id(2)
    tm = x_ref.shape[0]
    n_gate_rows = abt_ref.shape[0]
    slabs = range(0, tm, ROW_SLAB)

    def step(first):
        w = w_ref[...]
        if first:
            nw = nw_ref[...]
            shift = mod_ref[0:1, :]
            scale = mod_ref[1:2, :]
            for r in slabs:
                _norm_mod_slab(x_ref, nw, shift, scale, xn_ref, r, ROW_SLAB)
            wab = wab_ref[...]
            gates = [jnp.dot(xn_ref[r:r + ROW_SLAB, :], wab, preferred_element_type=F32) for r in slabs]
        outs = [jnp.dot(xn_ref[r:r + ROW_SLAB, :], w, preferred_element_type=F32) for r in slabs]
        for r, out in zip(slabs, outs):
            o_ref[r:r + ROW_SLAB, :] = out.astype(o_ref.dtype)
        if first:
            for r, ab in zip(slabs, gates):
                ab_ref[r:r + ROW_SLAB, :] = ab
                abt_ref[:, r:r + ROW_SLAB] = ab.T[0:n_gate_rows, :]

    @pl.when(j == 0)
    def _():
        step(True)

    @pl.when(j > 0)
    def _():
        step(False)


def _inproj(x3, mod, mod_base, nw, w_main, layer, w_ab, n_gate_rows):
    nb, rows, d = x3.shape
    n = w_main.shape[2]
    tm = min(TOKEN_TILE, rows)
    tn = COL_TILE
    return pl.pallas_call(
        _inproj_kernel,
        out_shape=(jax.ShapeDtypeStruct((nb, rows, n), BF16),
                   jax.ShapeDtypeStruct((nb, rows, 128), F32),
                   jax.ShapeDtypeStruct((nb, n_gate_rows, rows), F32)),
        grid=(nb, rows // tm, n // tn),
        in_specs=[pl.BlockSpec((None, tm, d), lambda b, i, j: (b, i, 0)),
                  pl.BlockSpec((None, N_MOD, d), lambda b, i, j: (mod_base + b, 0, 0)),
                  pl.BlockSpec((1, d), lambda b, i, j: (0, 0)),
                  pl.BlockSpec((None, d, tn), lambda b, i, j: (layer, 0, j)),
                  pl.BlockSpec((d, 128), lambda b, i, j: (0, 0))],
        out_specs=(pl.BlockSpec((None, tm, tn), lambda b, i, j: (b, i, j)),
                   pl.BlockSpec((None, tm, 128), lambda b, i, j: (b, i, 0)),
                   pl.BlockSpec((None, n_gate_rows, tm), lambda b, i, j: (b, 0, i))),
        scratch_shapes=[pltpu.VMEM((tm, d), BF16)],
        compiler_params=_cparams(3),
        name="in_proj",
    )(x3, mod, nw, w_main, w_ab)


def _split3(x):
    hi = x.astype(BF16)
    r1 = x - hi.astype(F32)
    mid = r1.astype(BF16)
    lo = (r1 - mid.astype(F32)).astype(BF16)
    return hi, mid, lo


def _gate_kernel(abx_ref, abtx_ref, abh_ref, abth_ref, alog_ref, dtb_ref, alogt_ref, dtbt_ref,
                 gb_ref, cum_ref, cumt_ref, *, seq_x, seq_h):
    c = CHUNK
    n_gt = abtx_ref.shape[0]
    slot_l = lax.broadcasted_iota(jnp.int32, (c, 128), 1) % GATE_SLOTS
    row_slot = lax.broadcasted_iota(jnp.int32, (n_gt, c), 0) % GATE_SLOTS
    ri = lax.broadcasted_iota(jnp.int32, (c, c), 0)
    ci = lax.broadcasted_iota(jnp.int32, (c, c), 1)
    m_le = jnp.where(ci <= ri, 1.0, 0.0).astype(BF16)
    m_ge = jnp.where(ci >= ri, 1.0, 0.0).astype(BF16)
    alog_r = alog_ref[...]
    dtb_r = dtb_ref[...]
    alogt = alogt_ref[...]
    dtbt = dtbt_ref[...]

    def part(ab_ref, abt_ref, row0, n_rows):
        def body(r, carry):
            src = pl.ds(pl.multiple_of(r * c, c), c)
            dst = pl.ds(pl.multiple_of(row0 + r * c, c), c)
            a = ab_ref[src, :]
            g = -jnp.exp(alog_r) * _softplus(a + dtb_r)
            gb = jnp.where(slot_l < N_DIR, g, 1.0 / (1.0 + jnp.exp(-a)))
            gb_ref[dst, :] = gb
            pieces = jnp.concatenate(_split3(gb), axis=1)
            pre = jnp.dot(m_le, pieces, preferred_element_type=F32)
            suf = jnp.dot(m_ge, pieces, preferred_element_type=F32)
            pre = pre[:, 0:128] + pre[:, 128:256] + pre[:, 256:384]
            suf = suf[:, 0:128] + suf[:, 128:256] + suf[:, 256:384]
            cum_ref[dst, :] = jnp.where(slot_l == 0, pre, suf)
            gt = -jnp.exp(alogt) * _softplus(abt_ref[:, src] + dtbt)
            pieces_t = jnp.concatenate(_split3(gt), axis=0)
            pre_t = jnp.dot(pieces_t, m_ge, preferred_element_type=F32)
            suf_t = jnp.dot(pieces_t, m_le, preferred_element_type=F32)
            pre_t = pre_t[0:n_gt] + pre_t[n_gt:2 * n_gt] + pre_t[2 * n_gt:3 * n_gt]
            suf_t = suf_t[0:n_gt] + suf_t[n_gt:2 * n_gt] + suf_t[2 * n_gt:3 * n_gt]
            cumt_ref[:, dst] = jnp.where(row_slot == 0, pre_t, suf_t)
            return carry

        lax.fori_loop(0, n_rows // c, body, 0, unroll=GATE_UNROLL)

    part(abh_ref, abth_ref, 0, seq_h)
    part(abx_ref, abtx_ref, seq_h, seq_x)


def _gates(abx, abtx, abh, abth, alog_c, dtb_c, alog_t, dtb_t):
    nb, seq_x, _ = abx.shape
    seq_h = abh.shape[1]
    n_gt = abtx.shape[1]
    seq_t = seq_x + seq_h
    return pl.pallas_call(
        functools.partial(_gate_kernel, seq_x=seq_x, seq_h=seq_h),
        out_shape=(jax.ShapeDtypeStruct((nb, seq_t, 128), F32),
                   jax.ShapeDtypeStruct((nb, seq_t, 128), F32),
                   jax.ShapeDtypeStruct((nb, n_gt, seq_t), F32)),
        grid=(nb,),
        in_specs=[pl.BlockSpec((None, seq_x, 128), lambda b: (b, 0, 0)),
                  pl.BlockSpec((None, n_gt, seq_x), lambda b: (b, 0, 0)),
                  pl.BlockSpec((None, seq_h, 128), lambda b: (b, 0, 0)),
                  pl.BlockSpec((None, n_gt, seq_h), lambda b: (b, 0, 0)),
                  pl.BlockSpec((1, 128), lambda b: (0, 0)),
                  pl.BlockSpec((1, 128), lambda b: (0, 0)),
                  pl.BlockSpec((n_gt, 128), lambda b: (0, 0)),
                  pl.BlockSpec((n_gt, 128), lambda b: (0, 0))],
        out_specs=(pl.BlockSpec((None, seq_t, 128), lambda b: (b, 0, 0)),
                   pl.BlockSpec((None, seq_t, 128), lambda b: (b, 0, 0)),
                   pl.BlockSpec((None, n_gt, seq_t), lambda b: (b, 0, 0))),
        compiler_params=_cparams(1),
        name="gates",
    )(abx, abtx, abh, abth, alog_c, dtb_c, alog_t, dtb_t)


def _delta_kernel(*refs, seq_x, seq_h, hg, ctx_out):
    n_in = 14
    (qx_ref, kx_ref, vx_ref, zx_ref, qh_ref, kh_ref, vh_ref, zh_ref,
     cq_ref, ck_ref, cv_ref, gb_ref, cum_ref, cumt_ref) = refs[:n_in]
    onorm_ref = refs[n_in]
    if ctx_out:
        ox_ref, oh_ref = refs[n_in + 1:n_in + 3]
        scratch = refs[n_in + 3:]
    else:
        ox_ref = refs[n_in + 1]
        oh_ref = None
        scratch = refs[n_in + 2:]
    (padq_s, padk_s, padv_s, q_s, k_s, v_s, o_s, u_s, wq_s, ktt_s, at_s, egl_s, st_s) = scratch

    c = CHUNK
    hd = HEAD_DIM
    width = hg * hd
    seq_t = seq_h + seq_x
    nc_h = seq_h // c
    nc_x = seq_x // c
    nc = nc_h + nc_x

    zero8 = jnp.zeros((8, hd), F32)
    for pad, h_ref, x_ref in ((padq_s, qh_ref, qx_ref), (padk_s, kh_ref, kx_ref), (padv_s, vh_ref, vx_ref)):
        for hl in range(hg):
            cols = slice(hl * hd, (hl + 1) * hd)
            pad[hl, 0:8, :] = zero8
            pad[hl, 8:8 + seq_h, :] = h_ref[:, cols].astype(F32)
            pad[hl, 8 + seq_h:16 + seq_h, :] = zero8
            pad[hl, 16 + seq_h:16 + seq_t, :] = x_ref[:, cols].astype(F32)
            pad[hl, 16 + seq_t:24 + seq_t, :] = zero8
    conv_taps = [[[w_ref[i:i + 1, hl * hd:(hl + 1) * hd] for i in range(CONV_K)] for hl in range(hg)]
                 for w_ref in (cq_ref, ck_ref, cv_ref)]

    def conv_chunk(ch):
        base = ch * c + jnp.where(ch >= nc_h, 16, 8)
        n_ph = c // CONV_PHASE
        for pad, taps, dst, mode in ((padq_s, conv_taps[0], q_s, "q"), (padk_s, conv_taps[1], k_s, "k"),
                                     (padv_s, conv_taps[2], v_s, "v")):
            for hl in range(hg):
                shifted = {off: pad[hl, pl.ds(base + off, n_ph, stride=CONV_PHASE), :]
                           for off in range(-(CONV_K // 2), CONV_PHASE + CONV_K // 2)}
                for ph in range(CONV_PHASE):
                    acc = None
                    for i in range(CONV_K):
                        term = shifted[ph + i - CONV_K // 2] * taps[hl][i]
                        acc = term if acc is None else acc + term
                    y = _silu(acc)
                    if mode != "v":
                        y = y * lax.rsqrt(jnp.sum(y * y, axis=-1, keepdims=True) + EPS)
                        if mode == "q":
                            y = y * (hd ** -0.5)
                    dst[hl, pl.ds(ch * c + ph, n_ph, stride=CONV_PHASE), :] = y

    shift = (128 - pl.program_id(1) * (hg * GATE_SLOTS)) % 128
    ri = lax.broadcasted_iota(jnp.int32, (c, c), 0)
    ci = lax.broadcasted_iota(jnp.int32, (c, c), 1)
    lower_incl = (ci <= ri)
    upper_incl = (ci >= ri)

    eye = jnp.where(ri == ci, 1.0, 0.0).astype(F32)
    masks = ((lower_incl, ci < ri), (upper_incl, ci > ri))

    def bdot(a, b):
        return jnp.dot(a.astype(BF16), b.astype(BF16), preferred_element_type=F32)

    same_blk = (ri // INV_BLOCK) == (ci // INV_BLOCK)

    def tri_inverse(nms, lowers):
        nds = [jnp.where(same_blk, nm, 0.0) for nm in nms]
        ms = [eye + nd for nd in nds]
        ps = [bdot(nd, nd) for nd in nds]
        for _ in range(int(math.log2(INV_BLOCK)) - 2):
            rs = [bdot(p, jnp.concatenate([m, p], axis=1)) for m, p in zip(ms, ps)]
            ms = [m + r_[:, 0:c] for m, r_ in zip(ms, rs)]
            ps = [r_[:, c:2 * c] for r_ in rs]
        rs = [bdot(p, m) for m, p in zip(ms, ps)]
        ms = [m + r_ for m, r_ in zip(ms, rs)]
        s = INV_BLOCK
        while s < c:
            starts = list(range(0, c, s))
            pair = ((ri // (2 * s)) == (ci // (2 * s))) & ((ri // s) != (ci // s))
            zero = jnp.zeros((s, c), BF16)
            hits = [[(a // s) % 2 == (1 if lower else 0) for a in starts] for lower in lowers]

            def take(xm, hit):
                return jnp.concatenate([xm[a:a + s] for a, t in zip(starts, hit) if t], axis=0)

            m_sels = [take(m, hit) for m, hit in zip(ms, hits)]
            xs = [bdot(take(jnp.where(pair, nm, 0.0), hit), m).astype(BF16)
                  for nm, m, hit in zip(nms, ms, hits)]
            ys = []
            for x, m_sel, hit in zip(xs, m_sels, hits):
                slabs, k_ = [], 0
                for t in hit:
                    slabs.append(x[k_ * s:(k_ + 1) * s] if t else zero)
                    k_ += 1 if t else 0
                ys.append(jnp.dot(m_sel.astype(BF16), jnp.concatenate(slabs, axis=0),
                                  preferred_element_type=F32))
            new_ms = []
            for m, m_sel, y, hit in zip(ms, m_sels, ys, hits):
                new_sel = m_sel + y
                slabs, k_ = [], 0
                for a, t in zip(starts, hit):
                    slabs.append(new_sel[k_ * s:(k_ + 1) * s] if t else m[a:a + s])
                    k_ += 1 if t else 0
                new_ms.append(jnp.concatenate(slabs, axis=0))
            ms = new_ms
            s *= 2
        return ms

    n_prep = PREP_CHUNKS if nc % PREP_CHUNKS == 0 else 1

    def prep(grp, carry):
        chs = [grp * n_prep + j for j in range(n_prep)]
        rows = [pl.ds(pl.multiple_of(ch * c, c), c) for ch in chs]
        gbc = [pltpu.roll(gb_ref[r, :], shift, 1) for r in rows]
        cumc = [pltpu.roll(cum_ref[r, :], shift, 1) for r in rows]
        cumtc = [cumt_ref[:, r] for r in rows]
        qs, ks, vs, aas = {}, {}, {}, {}
        for j in range(n_prep):
            for hl in range(hg):
                cols = slice(hl * hd, (hl + 1) * hd)
                qs[j, hl] = q_s[hl, rows[j], :]
                ks[j, hl] = k_s[hl, rows[j], :]
                vs[j, hl] = v_s[hl, rows[j], :]
                qk = jnp.concatenate([qs[j, hl], ks[j, hl]], axis=0).astype(BF16)
                aas[j, hl] = lax.dot_general(qk, ks[j, hl].astype(BF16), (((1,), (1,)), ((), ())),
                                             preferred_element_type=F32)
        probs = [(j, hl, d) for j in range(n_prep) for hl in range(hg) for d in range(N_DIR)]
        nms, betas, cums, decays = [], [], [], []
        for j, hl, d in probs:
            incl, strict = masks[d]
            base = hl * GATE_SLOTS
            beta_b = jnp.broadcast_to(gbc[j][:, base + N_DIR + d:base + N_DIR + d + 1], (c, hd))
            cum_b = jnp.broadcast_to(cumc[j][:, base + d:base + d + 1], (c, hd))
            cum_r = cumtc[j][base + d:base + d + 1, :]
            decay = jnp.where(incl, jnp.exp(jnp.where(incl, cum_b - cum_r, 0.0)), 0.0)
            nms.append(jnp.where(strict, -(beta_b * aas[j, hl][c:2 * c] * decay), 0.0))
            betas.append(beta_b)
            cums.append(cum_b)
            decays.append(decay)
        ms = tri_inverse(nms, [d == 0 for _, _, d in probs])
        e_cs = [jnp.exp(cum_b) for cum_b in cums]
        uws = [bdot(m, jnp.concatenate([vs[j, hl] * beta_b, ks[j, hl] * (beta_b * e_c)], axis=1))
               for (j, hl, d), m, beta_b, e_c in zip(probs, ms, betas, e_cs)]
        for i, (j, hl, d) in enumerate(probs):
            cum_b, e_c, uw = cums[i], e_cs[i], uws[i]
            gl_b = cum_b[c - 1:c, :] if d == 0 else cum_b[0:1, :]
            kt = ks[j, hl] * jnp.exp(gl_b - cum_b)
            slot = (hl * N_DIR + d) * nc + chs[j]
            u_s[pl.ds(pl.multiple_of(slot * c, c), c), :] = uw[:, 0:hd]
            wq_s[pl.ds(pl.multiple_of(slot * 2 * c, c), c), :] = uw[:, hd:2 * hd].astype(BF16)
            wq_s[pl.ds(pl.multiple_of(slot * 2 * c + c, c), c), :] = (qs[j, hl] * e_c).astype(BF16)
            ktt_s[pl.ds(pl.multiple_of(slot * hd, hd), hd), :] = kt.T.astype(BF16)
            at_s[pl.ds(pl.multiple_of(slot * c, c), c), :] = (aas[j, hl][0:c] * decays[i]).astype(BF16)
            egl_s[pl.ds(pl.multiple_of(slot * 8, 8), 8), :] = jnp.broadcast_to(jnp.exp(gl_b), (8, hd))
        nxt = jnp.minimum(grp + 1, n_groups_prep - 1)
        for j in range(n_prep):
            conv_chunk(nxt * n_prep + j)
        return carry

    n_groups_prep = nc // n_prep
    for j in range(n_prep):
        conv_chunk(j)
    lax.fori_loop(0, n_groups_prep, prep, 0)

    st_s[...] = jnp.zeros(st_s.shape, F32)
    o_s[...] = jnp.zeros(o_s.shape, F32)

    def scan_step(ch_f, ch_b, with_q):
        chains = [(hl, d) for hl in range(hg) for d in range(N_DIR)]
        chs = [ch_f if d == 0 else ch_b for _, d in chains]
        slots = [i * nc + ch for i, ch in enumerate(chs)]
        s_olds = [st_s[i * hd:(i + 1) * hd, :] for i in range(len(chains))]
        n_lhs = 2 * c if with_q else c
        rs = [jnp.dot(wq_s[pl.ds(pl.multiple_of(slot * 2 * c, 2 * c), n_lhs), :], s_old.astype(BF16),
                      preferred_element_type=F32) for slot, s_old in zip(slots, s_olds)]
        v_bs = [(u_s[pl.ds(pl.multiple_of(slot * c, c), c), :] - r_[0:c]).astype(BF16)
                for slot, r_ in zip(slots, rs)]
        kvs = [jnp.dot(ktt_s[pl.ds(pl.multiple_of(slot * hd, hd), hd), :], v_b, preferred_element_type=F32)
               for slot, v_b in zip(slots, v_bs)]
        if with_q:
            avs = [jnp.dot(at_s[pl.ds(pl.multiple_of(slot * c, c), c), :], v_b, preferred_element_type=F32)
                   for slot, v_b in zip(slots, v_bs)]
        for i, (hl, d) in enumerate(chains):
            egl = egl_s[pl.ds(pl.multiple_of(slots[i] * 8, 8), 1), :]
            st_s[i * hd:(i + 1) * hd, :] = s_olds[i] * egl + kvs[i]
            if with_q:
                orow = pl.ds(pl.multiple_of(chs[i] * c, c), c)
                ocol = slice(hl * hd, (hl + 1) * hd)
                o_s[orow, ocol] = o_s[orow, ocol] + (rs[i][c:2 * c] + avs[i])

    onw = onorm_ref[...]

    def finish_chunk(z_ref, out_ref, ch, ch0):
        src = pl.ds(pl.multiple_of(ch * c, c), c)
        dst = pl.ds(pl.multiple_of((ch - ch0) * c, c), c)
        outs = []
        for hl in range(hg):
            cols = slice(hl * hd, (hl + 1) * hd)
            outs.append(_rms(o_s[src, cols]) * onw * _silu(z_ref[dst, cols].astype(F32)))
        y = outs[0] if hg == 1 else jnp.concatenate(outs, axis=1)
        out_ref[dst, :] = y.astype(out_ref.dtype)

    def scan_h(it, carry):
        scan_step(it, nc_h - 1 - it, ctx_out)
        return carry

    def scan_x_first(it, carry):
        scan_step(nc_h + it, nc - 1 - it, True)
        return carry

    def finish_pair(it):
        finish_chunk(zx_ref, ox_ref, nc_h + it, nc_h)
        finish_chunk(zx_ref, ox_ref, nc - 1 - it, nc_h)

    def scan_x_second(it, carry):
        finish_pair(it - 1)
        scan_step(nc_h + it, nc - 1 - it, True)
        return carry

    lax.fori_loop(0, nc_h, scan_h, 0)
    lax.fori_loop(0, nc_x // 2 + 1, scan_x_first, 0)
    lax.fori_loop(nc_x // 2 + 1, nc_x, scan_x_second, 0)
    finish_pair(nc_x - 1)
    if ctx_out:
        for ch in range(nc_h):
            finish_chunk(zh_ref, oh_ref, ch, 0)


def _delta(pxm, phm, gb, cum, cumt, conv_w, onorm, n_heads, ctx_out):
    nb, seq_x, _ = pxm.shape
    seq_h = phm.shape[1]
    hg = HEADS_PER_STEP
    hd = HEAD_DIM
    width = hg * hd
    n_groups = n_heads // hg
    lin_w = n_heads * hd
    seq_t = seq_x + seq_h
    nc = seq_t // CHUNK
    n_chain = hg * N_DIR
    n_gt = hg * GATE_SLOTS

    def main_spec(rows, part):
        return pl.BlockSpec((None, rows, width), lambda b, g: (b, 0, part * n_groups + g))

    in_specs = (
        [main_spec(seq_x, p) for p in range(4)]
        + [main_spec(seq_h, p) for p in range(4)]
        + [pl.BlockSpec((CONV_K, width), lambda b, g, p=p: (0, p * n_groups + g)) for p in range(3)]
        + [pl.BlockSpec((None, seq_t, 128), lambda b, g: (b, 0, 0)),
           pl.BlockSpec((None, seq_t, 128), lambda b, g: (b, 0, 0)),
           pl.BlockSpec((None, n_gt, seq_t), lambda b, g: (b, g, 0)),
           pl.BlockSpec((1, hd), lambda b, g: (0, 0))]
    )
    out_shape = [jax.ShapeDtypeStruct((nb, seq_x, lin_w), BF16)]
    out_specs = [pl.BlockSpec((None, seq_x, width), lambda b, g: (b, 0, g))]
    if ctx_out:
        out_shape.append(jax.ShapeDtypeStruct((nb, seq_h, lin_w), BF16))
        out_specs.append(pl.BlockSpec((None, seq_h, width), lambda b, g: (b, 0, g)))
    scratch = [
        pltpu.VMEM((hg, seq_t + 24, hd), F32),
        pltpu.VMEM((hg, seq_t + 24, hd), F32),
        pltpu.VMEM((hg, seq_t + 24, hd), F32),
        pltpu.VMEM((hg, seq_t, hd), F32),
        pltpu.VMEM((hg, seq_t, hd), F32),
        pltpu.VMEM((hg, seq_t, hd), F32),
        pltpu.VMEM((seq_t, width), F32),
        pltpu.VMEM((n_chain * nc * CHUNK, hd), F32),
        pltpu.VMEM((n_chain * nc * 2 * CHUNK, hd), BF16),
        pltpu.VMEM((n_chain * nc * hd, CHUNK), BF16),
        pltpu.VMEM((n_chain * nc * CHUNK, CHUNK), BF16),
        pltpu.VMEM((n_chain * nc * 8, hd), F32),
        pltpu.VMEM((n_chain * hd, hd), F32),
    ]
    args = [pxm] * 4 + [phm] * 4 + [conv_w] * 3 + [gb, cum, cumt, onorm]
    res = pl.pallas_call(
        functools.partial(_delta_kernel, seq_x=seq_x, seq_h=seq_h, hg=hg, ctx_out=ctx_out),
        out_shape=tuple(out_shape),
        grid=(nb, n_groups),
        in_specs=in_specs,
        out_specs=tuple(out_specs),
        scratch_shapes=scratch,
        compiler_params=_cparams(2),
        name="gated_delta",
    )(*args)
    return res if ctx_out else (res[0], None)


def _pool_kernel(p_ref, w_ref, sc_ref, o_ref, y_s, *, on_grid, n_groups, gw):
    n = p_ref.shape[0]
    blk = 2 * GRID_W if on_grid else n
    period = GRID_W if on_grid else n
    ri = lax.broadcasted_iota(jnp.int32, (blk, blk), 0)
    ci = lax.broadcasted_iota(jnp.int32, (blk, blk), 1)
    pos = lax.broadcasted_iota(jnp.int32, (blk, gw), 0) % period

    for g in range(n_groups):
        size = POOL_SIZES[g]
        half = size // 2
        cols = slice(g * gw, (g + 1) * gw)
        if on_grid:
            n_rows = n // GRID_W
            def grid_row(rr):
                return p_ref[rr * GRID_W:(rr + 1) * GRID_W, cols].astype(F32)

            acc, lo_p, hi_p = None, 0, -1
            for r in range(n_rows):
                lo = max(r - half, 0)
                hi = min(r + size - 1 - half, n_rows - 1)
                for rr in range(hi_p + 1, hi + 1):
                    acc = grid_row(rr) if acc is None else acc + grid_row(rr)
                for rr in range(lo_p, lo):
                    acc = acc - grid_row(rr)
                lo_p, hi_p = lo, hi
                y_s[r * GRID_W:(r + 1) * GRID_W, :] = acc / float(hi - lo + 1)
            src = y_s
            src_cols = slice(0, gw)
        else:
            src = p_ref
            src_cols = cols
        band = ((ci >= ri - half) & (ci <= ri + size - 1 - half) & (ci // period == ri // period))
        band = jnp.where(band, 1.0, 0.0).astype(BF16)
        lo_c = jnp.maximum(pos - half, 0)
        hi_c = jnp.minimum(pos + size - 1 - half, period - 1)
        cnt = (hi_c - lo_c + 1).astype(F32)
        w_g = w_ref[g]
        scale = sc_ref[:, cols]

        n_blk = n // blk
        group = POOL_BLOCKS if n_blk % POOL_BLOCKS == 0 else 1

        def body(t, carry):
            rows = [pl.ds(pl.multiple_of((t * group + j) * blk, blk), blk) for j in range(group)]
            pieces = []
            for r in rows:
                pieces.append(jnp.concatenate(_split3(src[r, src_cols].astype(F32)), axis=1))
            tots = [jnp.dot(band, pc, preferred_element_type=F32) for pc in pieces]
            dlts = []
            for r, tot in zip(rows, tots):
                mean = (tot[:, 0:gw] + tot[:, gw:2 * gw] + tot[:, 2 * gw:3 * gw]) / cnt
                dlts.append((mean - p_ref[r, cols].astype(F32)).astype(BF16))
            outs = [jnp.dot(dlt, w_g, preferred_element_type=F32) for dlt in dlts]
            for r, out in zip(rows, outs):
                o_ref[r, cols] = (out * scale).astype(o_ref.dtype)
            return carry

        lax.fori_loop(0, n_blk // group, body, 0)


def _pool(pm, pool_w, pool_scale, part0, on_grid):
    nb, rows, _ = pm.shape
    n_groups, gw, _ = pool_w.shape
    pw = n_groups * gw
    return pl.pallas_call(
        functools.partial(_pool_kernel, on_grid=on_grid, n_groups=n_groups, gw=gw),
        out_shape=jax.ShapeDtypeStruct((nb, rows, pw), BF16),
        grid=(nb,),
        in_specs=[pl.BlockSpec((None, rows, pw), lambda b: (b, 0, part0)),
                  pl.BlockSpec((n_groups, gw, gw), lambda b: (0, 0, 0)),
                  pl.BlockSpec((1, pw), lambda b: (0, 0))],
        out_specs=pl.BlockSpec((None, rows, pw), lambda b: (b, 0, 0)),
        scratch_shapes=[pltpu.VMEM((rows, gw), F32)],
        compiler_params=_cparams(1),
        name="pool_mixer",
    )(pm, pool_w, pool_scale)


def _outproj_kernel(lin_ref, pool_ref, x_ref, mod_ref, wa_ref, wb_ref, o_ref):
    y = (jnp.dot(lin_ref[...], wa_ref[...], preferred_element_type=F32)
         + jnp.dot(pool_ref[...], wb_ref[...], preferred_element_type=F32))
    o_ref[...] = x_ref[...] + mod_ref[2:3, :] * y


def _outproj(lin, pool, x3, mod, mod_base, w_out_all, layer):
    nb, rows, d = x3.shape
    tm = min(OUT_TOKEN_TILE, rows)
    ka = lin.shape[2]
    kb = pool.shape[2]
    assert ka == kb and w_out_all.shape[1] == ka + kb
    return pl.pallas_call(
        _outproj_kernel,
        out_shape=jax.ShapeDtypeStruct((nb, rows, d), F32),
        grid=(nb, rows // tm),
        in_specs=[pl.BlockSpec((None, tm, ka), lambda b, i: (b, i, 0)),
                  pl.BlockSpec((None, tm, kb), lambda b, i: (b, i, 0)),
                  pl.BlockSpec((None, tm, d), lambda b, i: (b, i, 0)),
                  pl.BlockSpec((None, N_MOD, d), lambda b, i: (mod_base + b, 0, 0)),
                  pl.BlockSpec((None, ka, d), lambda b, i: (layer, 0, 0)),
                  pl.BlockSpec((None, kb, d), lambda b, i: (layer, 1, 0))],
        out_specs=pl.BlockSpec((None, tm, d), lambda b, i: (b, i, 0)),
        compiler_params=_cparams(2),
        name="out_proj",
    )(lin, pool, x3, mod, w_out_all, w_out_all)


def _mlp_kernel(x_ref, mod_ref, nw_ref, wu_ref, wd_ref, fw_ref, o_ref, xn_ref, *, final_norm):
    f = pl.program_id(2)
    n_f = pl.num_programs(2)
    slabs = range(0, xn_ref.shape[0], ROW_SLAB)

    def step(first, last):
        wu = wu_ref[...]
        wd = wd_ref[...]
        if first:
            nw = nw_ref[...]
            shift = mod_ref[3:4, :]
            scale = mod_ref[4:5, :]
            for r in slabs:
                _norm_mod_slab(x_ref, nw, shift, scale, xn_ref, r, ROW_SLAB)
        hs = [jnp.dot(xn_ref[r:r + ROW_SLAB, :], wu, preferred_element_type=F32) for r in slabs]
        hs = [jnp.square(jnp.maximum(h, 0.0)).astype(BF16) for h in hs]
        parts = [jnp.dot(h, wd, preferred_element_type=F32) for h in hs]
        for r, part in zip(slabs, parts):
            acc = part if first else o_ref[r:r + ROW_SLAB, :] + part
            if last:
                acc = x_ref[r:r + ROW_SLAB, :] + mod_ref[5:6, :] * acc
                if final_norm:
                    acc = _rms(acc) * fw_ref[...]
            o_ref[r:r + ROW_SLAB, :] = acc

    @pl.when(f == 0)
    def _():
        step(True, False)

    @pl.when((f > 0) & (f < n_f - 1))
    def _():
        step(False, False)

    @pl.when(f == n_f - 1)
    def _():
        step(False, True)


def _mlp(x3, mod, mod_base, nw, w_up, w_down, layer, final_w, final_norm):
    nb, rows, d = x3.shape
    dff = w_up.shape[2]
    tm = min(TOKEN_TILE, rows)
    tf = FF_TILE
    return pl.pallas_call(
        functools.partial(_mlp_kernel, final_norm=final_norm),
        out_shape=jax.ShapeDtypeStruct((nb, rows, d), F32),
        grid=(nb, rows // tm, dff // tf),
        in_specs=[pl.BlockSpec((None, tm, d), lambda b, i, f: (b, i, 0)),
                  pl.BlockSpec((None, N_MOD, d), lambda b, i, f: (mod_base + b, 0, 0)),
                  pl.BlockSpec((1, d), lambda b, i, f: (0, 0)),
                  pl.BlockSpec((None, d, tf), lambda b, i, f: (layer, 0, f)),
                  pl.BlockSpec((None, tf, d), lambda b, i, f: (layer, f, 0)),
                  pl.BlockSpec((1, d), lambda b, i, f: (0, 0))],
        out_specs=pl.BlockSpec((None, tm, d), lambda b, i, f: (b, i, 0)),
        scratch_shapes=[pltpu.VMEM((tm, d), BF16)],
        compiler_params=_cparams(3),
        name="mlp",
    )(x3, mod, nw, w_up, w_down, final_w)


def _gate_layouts(w_in_l, a_log_l, dt_bias_l, lin_w, n_heads):
    d = w_in_l.shape[0]
    decay_start = 4 * lin_w
    ab = w_in_l[:, decay_start:decay_start + 2 * N_DIR * n_heads]
    ab = ab.reshape(d, 2 * N_DIR, n_heads)
    ab = jnp.transpose(ab, (0, 2, 1))
    ab = jnp.pad(ab, ((0, 0), (0, 0), (0, GATE_SLOTS - 2 * N_DIR)))
    w_ab = ab.reshape(d, n_heads * GATE_SLOTS)
    w_ab = jnp.pad(w_ab, ((0, 0), (0, 128 - n_heads * GATE_SLOTS))).astype(BF16)

    def per_head(p):
        return jnp.pad(p.T.astype(F32), ((0, 0), (0, GATE_SLOTS - N_DIR)))

    alog = per_head(a_log_l)
    dtb = per_head(dt_bias_l)
    alog_c = jnp.pad(alog.reshape(1, n_heads * GATE_SLOTS), ((0, 0), (0, 128 - n_heads * GATE_SLOTS)))
    dtb_c = jnp.pad(dtb.reshape(1, n_heads * GATE_SLOTS), ((0, 0), (0, 128 - n_heads * GATE_SLOTS)))
    alog_t = jnp.broadcast_to(alog.reshape(n_heads * GATE_SLOTS, 1), (n_heads * GATE_SLOTS, 128))
    dtb_t = jnp.broadcast_to(dtb.reshape(n_heads * GATE_SLOTS, 1), (n_heads * GATE_SLOTS, 128))
    return w_ab, alog_c, dtb_c, alog_t, dtb_t


def kernel(x, c, ctx, c_ctx, w_ada, b_ada, norm1_w, norm2_w, w_in, conv_w, a_log, dt_bias, onorm_w,
           pool_w, pool_scale, w_out, w_up, w_down, final_norm_w):
    nb, seq_x, d = x.shape
    seq_h = ctx.shape[1]
    depth = w_ada.shape[0]
    n_heads = a_log.shape[2]
    lin_w = n_heads * HEAD_DIM
    pool_width = pool_scale.shape[1]
    pool_start = 4 * lin_w + 2 * N_DIR * n_heads
    assert nb + 1 <= MOD_ROWS and seq_x % (2 * CHUNK) == 0 and seq_h % CHUNK == 0
    assert n_heads % HEADS_PER_STEP == 0 and seq_x % GRID_W == 0

    c_all = jnp.concatenate([c, c_ctx[None, :], jnp.zeros((MOD_ROWS - nb - 1, d), F32)], axis=0)
    mod_all = _ada(c_all, w_ada, b_ada).reshape(depth, MOD_ROWS, N_MOD, d)

    w_main = jnp.concatenate([w_in[:, :, :4 * lin_w], w_in[:, :, pool_start:]], axis=2).astype(BF16)
    w_o = w_out.astype(BF16)
    w_u = w_up.astype(BF16)
    w_d = w_down.astype(BF16)
    fw = final_norm_w.reshape(1, d)

    h = ctx
    for l in range(depth):
        ctx_out = l < depth - 1
        mod = mod_all[l]
        w_ab, alog_c, dtb_c, alog_t, dtb_t = _gate_layouts(w_in[l], a_log[l], dt_bias[l], lin_w, n_heads)
        n_gate_rows = n_heads * GATE_SLOTS
        nw1 = norm1_w[l].reshape(1, d)
        nw2 = norm2_w[l].reshape(1, d)
        pw = pool_w[l].astype(BF16)
        ps = pool_scale[l].reshape(1, pool_width)

        pxm, abx, abtx = _inproj(x, mod, 0, nw1, w_main, l, w_ab, n_gate_rows)
        h3 = h.reshape(1, nb * seq_h, d)
        phm, abh, abth = _inproj(h3, mod, nb, nw1, w_main, l, w_ab, n_gate_rows)
        phm = phm.reshape(nb, seq_h, -1)
        abh = abh.reshape(nb, seq_h, 128)
        abth = jnp.transpose(abth.reshape(-1, nb, seq_h), (1, 0, 2))

        gb, cum, cumt = _gates(abx, abtx, abh, abth, alog_c, dtb_c, alog_t, dtb_t)
        lin_x, lin_h = _delta(pxm, phm, gb, cum, cumt, conv_w[l], onorm_w[l].reshape(1, HEAD_DIM),
                              n_heads, ctx_out)
        part0 = (4 * lin_w) // pool_width
        pool_x = _pool(pxm, pw, ps, part0, True)
        x = _outproj(lin_x, pool_x, x, mod, 0, w_o, l)
        x = _mlp(x, mod, 0, nw2, w_u, w_d, l, fw, l == depth - 1)
        if ctx_out:
            pool_h = _pool(phm, pw, ps, part0, False)
            h3 = _outproj(lin_h.reshape(1, nb * seq_h, lin_w), pool_h.reshape(1, nb * seq_h, pool_width),
                          h3, mod, nb, w_o, l)
            h = _mlp(h3, mod, nb, nw2, w_u, w_d, l, fw, False).reshape(nb, seq_h, d)
    return x
```
